```python
import math
import jax
import jax.numpy as jnp
from jax import lax
import numpy as np

D_MODEL = 1024
BATCH = 4
SEQ = 8192
DEPTH = 2

CTX_LEN = 256
GRID_W = 64
EPS = 1e-6
HEAD_DIM = 64
ROPE_THETA = 10000.0
ATT_HEADS = 8
ATT_KV_HEADS = 2
ATT_GROUP = ATT_HEADS // ATT_KV_HEADS
WINDOW = 128
ATT_BLOCK = 128
ATT_Q_W = ATT_HEADS * HEAD_DIM
ATT_KV_W = ATT_KV_HEADS * HEAD_DIM
SSM_HEADS = 8
SSM_HEAD_DIM = 64
SSM_INNER = SSM_HEADS * SSM_HEAD_DIM
SSM_GROUPS = 2
SSM_STATE = 64
SSM_CONV = 3
SSM_CHUNK = 128
SSM_BC_W = SSM_GROUPS * SSM_STATE
SSM_XBC_W = SSM_INNER + 2 * SSM_BC_W
DT_MIN = 0.001
DT_MAX = 0.1
FNET_GROUPS = 8
FNET_GROUP_DIM = 64
FNET_WIDTH = FNET_GROUPS * FNET_GROUP_DIM
SCONV_WIDTH = 512
SCONV_K = 3
N_BRANCHES = 4
BRANCH_W = 512
IN_SIZES = (ATT_Q_W, ATT_KV_W, ATT_KV_W, SSM_INNER, SSM_XBC_W, 2 * SSM_HEADS, FNET_WIDTH, 3 * SCONV_WIDTH)
IN_PROJ_W = sum(IN_SIZES)
N_EXPERTS = 32
N_EXPERT_GROUPS = 8
EXPERTS_PER_GROUP = N_EXPERTS // N_EXPERT_GROUPS
TOP_K = 2
D_EXPERT = 512
EXPERT_BLOCK = 128

kernel_name = "hybrid_gated_mixers_moe_dit"


def _rmsnorm(x, g):
    xf = x.astype(jnp.float32)
    y = xf * lax.rsqrt(jnp.mean(xf * xf, axis=-1, keepdims=True) + EPS)
    return y.astype(x.dtype) * g


def _modulate(h, shift, scale):
    return h * (1 + scale) + shift


def _dwconv(x, w, b=None):
    k, ch = w.shape
    y = lax.conv_general_dilated(x, w[:, None, :], window_strides=(1,), padding=[((k - 1) // 2, k // 2)],
                                 dimension_numbers=("NWC", "WIO", "NWC"), feature_group_count=ch)
    return y if b is None else y + b


def _split_in(p):
    cuts = np.cumsum(IN_SIZES)[:-1].tolist()
    return jnp.split(p, cuts, axis=-1)


def _axial_rope(n):
    rows = n // GRID_W
    row = jnp.repeat(jnp.arange(rows, dtype=jnp.float32), GRID_W)
    col = jnp.tile(jnp.arange(GRID_W, dtype=jnp.float32), rows)
    pairs = HEAD_DIM // 4
    inv_freq = ROPE_THETA ** (-jnp.arange(pairs, dtype=jnp.float32) / pairs)
    ang = jnp.concatenate([row[:, None] * inv_freq, col[:, None] * inv_freq], axis=-1)
    return jnp.cos(ang), jnp.sin(ang)


def _rope(x, cos, sin):
    x1, x2 = jnp.split(x, 2, axis=-1)
    c = cos[:, None, :].astype(x.dtype)
    s = sin[:, None, :].astype(x.dtype)
    return jnp.concatenate([x1 * c - x2 * s, x2 * c + x1 * s], axis=-1)


def _window_attention(q, k, v, k_ctx, v_ctx, sink):
    b, s, kvh, g, dh = q.shape
    nb = s // ATT_BLOCK
    scale = dh ** -0.5
    qb = q.reshape(b, nb, ATT_BLOCK, kvh, g, dh)

    def band(t):
        tb = t.reshape(b, nb, ATT_BLOCK, kvh, dh)
        z = jnp.zeros_like(tb[:, :1])
        tp = jnp.concatenate([z, tb, z], axis=1)
        return jnp.concatenate([tp[:, :-2], tp[:, 1:-1], tp[:, 2:]], axis=2)

    kw, vw = band(k), band(v)
    sink_f = sink.astype(jnp.float32).reshape(kvh, g, 1, 1)
    qi = jnp.arange(ATT_BLOCK)
    kj = jnp.arange(3 * ATT_BLOCK)

    def block(args):
        qn, kn, vn, n = args
        s_loc = jnp.einsum("bqkgd,bjkd->bkgqj", qn, kn).astype(jnp.float32) * scale
        kabs = (n - 1) * ATT_BLOCK + kj
        qabs = n * ATT_BLOCK + qi
        valid = (jnp.abs(kabs[None, :] - qabs[:, None]) <= WINDOW) & (kabs >= 0)[None, :] & (kabs < s)[None, :]
        s_loc = jnp.where(valid, s_loc, -jnp.inf)
        s_ctx = jnp.einsum("bqkgd,bckd->bkgqc", qn, k_ctx).astype(jnp.float32) * scale
        sink_col = jnp.broadcast_to(sink_f, s_loc.shape[:-1] + (1,))
        probs = jax.nn.softmax(jnp.concatenate([s_loc, s_ctx, sink_col], axis=-1), axis=-1).astype(vn.dtype)
        p_loc = probs[..., :3 * ATT_BLOCK]
        p_ctx = probs[..., 3 * ATT_BLOCK:-1]
        return (jnp.einsum("bkgqj,bjkd->bqkgd", p_loc, vn)
                + jnp.einsum("bkgqc,bckd->bqkgd", p_ctx, v_ctx))

    outs = lax.map(block, (jnp.moveaxis(qb, 1, 0), jnp.moveaxis(kw, 1, 0), jnp.moveaxis(vw, 1, 0),
                           jnp.arange(nb)))
    return jnp.moveaxis(outs, 0, 1).reshape(b, s, kvh * g * dh)


def _context_attention(q, k, v, sink):
    b, lc, kvh, g, dh = q.shape
    sc = jnp.einsum("bqkgd,bckd->bkgqc", q, k).astype(jnp.float32) * dh ** -0.5
    sink_col = jnp.broadcast_to(sink.astype(jnp.float32).reshape(kvh, g, 1, 1), sc.shape[:-1] + (1,))
    probs = jax.nn.softmax(jnp.concatenate([sc, sink_col], axis=-1), axis=-1)[..., :-1].astype(v.dtype)
    return jnp.einsum("bkgqc,bckd->bqkgd", probs, v).reshape(b, lc, kvh * g * dh)


def _ssd(x, dt, a, bm, cm, init, with_output):
    b, l, h, p = x.shape
    n = bm.shape[-1]
    nc = l // SSM_CHUNK
    a_cs = jnp.cumsum((dt * a).reshape(b, nc, SSM_CHUNK, h), axis=2)
    xd = (x * dt[..., None]).reshape(b, nc, SSM_CHUNK, h, p)
    bc = bm.reshape(b, nc, SSM_CHUNK, h, n)
    decay_to_end = jnp.exp(a_cs[:, :, -1:] - a_cs)
    chunk_states = jnp.einsum("bcqhn,bcqhp->bchpn", bc * decay_to_end[..., None], xd)
    chunk_decay = jnp.exp(a_cs[:, :, -1])

    def step(state, inp):
        st, dec = inp
        return state * dec[:, :, None, None] + st, state

    final, prev = lax.scan(step, init, (jnp.moveaxis(chunk_states, 1, 0), jnp.moveaxis(chunk_decay, 1, 0)))
    if not with_output:
        return None, final
    prev = jnp.moveaxis(prev, 0, 1)
    cc = cm.reshape(b, nc, SSM_CHUNK, h, n)
    seg = a_cs[:, :, :, None, :] - a_cs[:, :, None, :, :]
    tri = jnp.tril(jnp.ones((SSM_CHUNK, SSM_CHUNK), dtype=bool))[None, None, :, :, None]
    lmat = jnp.exp(jnp.where(tri, seg, -jnp.inf))
    scores = jnp.einsum("bcihn,bcjhn->bcijh", cc, bc) * lmat
    y_diag = jnp.einsum("bcijh,bcjhp->bcihp", scores, xd)
    y_off = jnp.einsum("bcihn,bchpn->bcihp", cc, prev) * jnp.exp(a_cs)[..., None]
    return (y_diag + y_off).reshape(b, l, h, p), final


def _ssm_prep(xbc, dt_raw, conv_w, conv_b, dt_bias):
    b, l, _ = xbc.shape
    xbc = jax.nn.silu(_dwconv(xbc, conv_w, conv_b)).astype(jnp.float32)
    xs, bg, cg = jnp.split(xbc, [SSM_INNER, SSM_INNER + SSM_BC_W], axis=-1)
    rep = SSM_HEADS // SSM_GROUPS
    xs = xs.reshape(b, l, SSM_HEADS, SSM_HEAD_DIM)
    bm = jnp.repeat(bg.reshape(b, l, SSM_GROUPS, SSM_STATE), rep, axis=2)
    cm = jnp.repeat(cg.reshape(b, l, SSM_GROUPS, SSM_STATE), rep, axis=2)
    dt = jax.nn.softplus(dt_raw.astype(jnp.float32).reshape(b, l, 2, SSM_HEADS) + dt_bias.astype(jnp.float32))
    return xs, bm, cm, dt


def _ssm_bidir(xs, bm, cm, dt, a_log, init_f, init_b, with_output):
    a = -jnp.exp(a_log.astype(jnp.float32))
    flip = lambda t: jnp.flip(t, axis=1)
    y_f, fin_f = _ssd(xs, dt[:, :, 0], a[0], bm, cm, init_f, with_output)
    y_b, fin_b = _ssd(flip(xs), flip(dt[:, :, 1]), a[1], flip(bm), flip(cm), init_b, with_output)
    if not with_output:
        return None, fin_f, fin_b
    return y_f + flip(y_b), fin_f, fin_b


def _ssm_out(y, xs, z, d_skip, norm_w):
    b, l = z.shape[:2]
    y = (y + d_skip.astype(jnp.float32)[:, None] * xs).reshape(b, l, SSM_INNER)
    y = y * jax.nn.silu(z.astype(jnp.float32))
    return _rmsnorm(y, norm_w).astype(z.dtype)


def _fourier(u):
    b, l, _ = u.shape
    uf = u.astype(jnp.float32).reshape(b, l, FNET_GROUPS, FNET_GROUP_DIM)
    y = jnp.fft.fft2(uf, axes=(1, 3), norm="ortho").real
    return y.reshape(b, l, FNET_WIDTH).astype(u.dtype)


def _short_conv(p, w):
    bg, cg, xs = jnp.split(p, 3, axis=-1)
    return bg * _dwconv(cg * xs, w)


def _merge(h, branches, w_gate, b_gate, w_branch, w_o):
    b, l, d = h.shape
    gates = jax.nn.sigmoid(h @ w_gate + b_gate).reshape(b, l, N_BRANCHES, d)
    proj = jnp.einsum("blnw,nwd->blnd", branches, w_branch)
    return jnp.sum(gates * proj, axis=2) @ w_o


def _token_mixer(h, hc, w_in, sink, conv_w, conv_b, dt_bias, a_log, d_skip, ssm_norm_w, sconv_w,
                 w_gate, b_gate, w_branch, w_o, cos, sin, need_ctx):
    b, s, _ = h.shape
    lc = hc.shape[1]
    q, k, v, z, xbc, dtr, fu, su = _split_in(h @ w_in)
    qc, kc, vc, zc, xbcc, dtrc, fuc, suc = _split_in(hc @ w_in)
    q = _rope(q.reshape(b, s, ATT_HEADS, HEAD_DIM), cos, sin).reshape(b, s, ATT_KV_HEADS, ATT_GROUP, HEAD_DIM)
    k = _rope(k.reshape(b, s, ATT_KV_HEADS, HEAD_DIM), cos, sin)
    v = v.reshape(b, s, ATT_KV_HEADS, HEAD_DIM)
    kc = kc.reshape(b, lc, ATT_KV_HEADS, HEAD_DIM)
    vc = vc.reshape(b, lc, ATT_KV_HEADS, HEAD_DIM)
    att = _window_attention(q, k, v, kc, vc, sink)
    xs_c, bm_c, cm_c, dt_c = _ssm_prep(xbcc, dtrc, conv_w, conv_b, dt_bias)
    zero = jnp.zeros((b, SSM_HEADS, SSM_HEAD_DIM, SSM_STATE), jnp.float32)
    y_c, fin_f, fin_b = _ssm_bidir(xs_c, bm_c, cm_c, dt_c, a_log, zero, zero, need_ctx)
    xs, bm, cm, dt = _ssm_prep(xbc, dtr, conv_w, conv_b, dt_bias)
    y, _, _ = _ssm_bidir(xs, bm, cm, dt, a_log, fin_f, fin_b, True)
    ssm = _ssm_out(y, xs, z, d_skip, ssm_norm_w)
    branches = jnp.stack([att, ssm, _fourier(fu), _short_conv(su, sconv_w)], axis=2)
    out = _merge(h, branches, w_gate, b_gate, w_branch, w_o)
    if not need_ctx:
        return out, None
    att_c = _context_attention(qc.reshape(b, lc, ATT_KV_HEADS, ATT_GROUP, HEAD_DIM), kc, vc, sink)
    ssm_c = _ssm_out(y_c, xs_c, zc, d_skip, ssm_norm_w)
    branches_c = jnp.stack([att_c, ssm_c, _fourier(fuc), _short_conv(suc, sconv_w)], axis=2)
    return out, _merge(hc, branches_c, w_gate, b_gate, w_branch, w_o)


def _route(t, w_router, router_bias):
    n = t.shape[0]
    scores = jax.nn.sigmoid((t @ w_router).astype(jnp.float32))
    sel = (scores + router_bias.astype(jnp.float32)).reshape(n, N_EXPERT_GROUPS, EXPERTS_PER_GROUP)
    group_score = jnp.sum(lax.top_k(sel, 2)[0], axis=-1)
    best = jnp.argmax(group_score, axis=-1)
    in_group = jnp.take_along_axis(sel, best[:, None, None], axis=1)[:, 0]
    _, local = lax.top_k(in_group, TOP_K)
    idx = best[:, None] * EXPERTS_PER_GROUP + local
    w = jnp.take_along_axis(scores, idx, axis=1)
    return idx, w / jnp.sum(w, axis=-1, keepdims=True)


def _expert_ffn(t, idx, wts, w1, w3, w2):
    n, d = t.shape
    flat_e = idx.reshape(-1)
    a = flat_e.shape[0]
    order = jnp.argsort(flat_e)
    e_sorted = flat_e[order]
    tok = order // TOP_K
    counts = jnp.bincount(flat_e, length=N_EXPERTS)
    padded = (counts + EXPERT_BLOCK - 1) // EXPERT_BLOCK * EXPERT_BLOCK
    pad_end = jnp.cumsum(padded)
    grp_start = jnp.cumsum(counts) - counts
    dest = (pad_end - padded)[e_sorted] + jnp.arange(a) - grp_start[e_sorted]
    n_blocks = -(-a // EXPERT_BLOCK) + N_EXPERTS
    buf = jnp.zeros((n_blocks * EXPERT_BLOCK, d), t.dtype).at[dest].set(t[tok])
    block_expert = jnp.minimum(jnp.searchsorted(pad_end, jnp.arange(n_blocks) * EXPERT_BLOCK, side="right"),
                               N_EXPERTS - 1)

    def expert_block(args):
        xb, e = args
        return (jax.nn.silu(xb @ w1[e]) * (xb @ w3[e])) @ w2[e]

    yb = lax.map(expert_block, (buf.reshape(n_blocks, EXPERT_BLOCK, d), block_expert))
    y_rows = yb.reshape(-1, d)[dest] * wts.reshape(-1)[order][:, None].astype(t.dtype)
    return jnp.zeros_like(t).at[tok].add(y_rows)


def _moe(t, w_router, router_bias, w1, w3, w2):
    idx, wts = _route(t, w_router, router_bias)
    return _expert_ffn(t, idx, wts, w1, w3, w2)


def setup_inputs(seed: int = 0) -> dict:
    key = jax.random.key(seed)
    ks = jax.random.split(key, 28)
    f32 = jnp.float32
    nrm = lambda k, shape, scale: jax.random.normal(k, shape, f32) * scale
    L, D = DEPTH, D_MODEL
    dt0 = jnp.exp(jax.random.uniform(ks[12], (L, 2, SSM_HEADS), f32, math.log(DT_MIN), math.log(DT_MAX)))
    return {
        "x": nrm(ks[0], (BATCH, SEQ, D), 1.0),
        "c": nrm(ks[1], (BATCH, D), 1.0),
        "ctx": nrm(ks[2], (BATCH, CTX_LEN, D), 1.0),
        "c_ctx": nrm(ks[3], (D,), 1.0),
        "norm1_w": 1.0 + nrm(ks[4], (L, D), 0.02),
        "norm2_w": 1.0 + nrm(ks[5], (L, D), 0.02),
        "w_ada": nrm(ks[6], (L, D, 6 * D), 0.5 * D ** -0.5),
        "b_ada": nrm(ks[7], (L, 6 * D), 0.02),
        "w_in": nrm(ks[8], (L, D, IN_PROJ_W), D ** -0.5),
        "attn_sink": nrm(ks[9], (L, ATT_HEADS), 0.5),
        "ssm_conv_w": nrm(ks[10], (L, SSM_CONV, SSM_XBC_W), SSM_CONV ** -0.5),
        "ssm_conv_b": nrm(ks[11], (L, SSM_XBC_W), 0.02),
        "ssm_dt_bias": dt0 + jnp.log(-jnp.expm1(-dt0)),
        "ssm_a_log": jnp.log(jax.random.uniform(ks[13], (L, 2, SSM_HEADS), f32, 1.0, 16.0)),
        "ssm_d": 1.0 + nrm(ks[14], (L, SSM_HEADS), 0.1),
        "ssm_norm_w": 1.0 + nrm(ks[15], (L, SSM_INNER), 0.02),
        "sconv_w": nrm(ks[16], (L, SCONV_K, SCONV_WIDTH), SCONV_K ** -0.5),
        "w_gate": nrm(ks[17], (L, D, N_BRANCHES * D), D ** -0.5),
        "b_gate": nrm(ks[18], (L, N_BRANCHES * D), 0.02),
        "w_branch": nrm(ks[19], (L, N_BRANCHES, BRANCH_W, D), BRANCH_W ** -0.5),
        "w_o": nrm(ks[20], (L, D, D), D ** -0.5),
        "w_router": nrm(ks[21], (D, N_EXPERTS), D ** -0.5),
        "router_bias": nrm(ks[22], (N_EXPERTS,), 0.01),
        "moe_w1": nrm(ks[23], (L, N_EXPERTS, D, D_EXPERT), D ** -0.5),
        "moe_w3": nrm(ks[24], (L, N_EXPERTS, D, D_EXPERT), D ** -0.5),
        "moe_w2": nrm(ks[25], (L, N_EXPERTS, D_EXPERT, D), D_EXPERT ** -0.5),
        "final_norm_w": 1.0 + nrm(ks[26], (D,), 0.02),
    }


def reference(x, c, ctx, c_ctx, norm1_w, norm2_w, w_ada, b_ada, w_in, attn_sink, ssm_conv_w, ssm_conv_b,
              ssm_dt_bias, ssm_a_log, ssm_d, ssm_norm_w, sconv_w, w_gate, b_gate, w_branch, w_o,
              w_router, router_bias, moe_w1, moe_w3, moe_w2, final_norm_w):
    b, s, d = x.shape
    lc = ctx.shape[1]
    cos, sin = _axial_rope(s)
    sc = jax.nn.silu(c)
    scc = jax.nn.silu(c_ctx)
    xc = ctx
    for layer in range(DEPTH):
        need_ctx = layer + 1 < DEPTH
        mod = jnp.split(sc @ w_ada[layer] + b_ada[layer], 6, axis=-1)
        mod_c = jnp.split(scc @ w_ada[layer] + b_ada[layer], 6, axis=-1)
        h = _modulate(_rmsnorm(x, norm1_w[layer]), mod[0][:, None], mod[1][:, None])
        hc = _modulate(_rmsnorm(xc, norm1_w[layer]), mod_c[0], mod_c[1])
        y, yc = _token_mixer(h, hc, w_in[layer], attn_sink[layer], ssm_conv_w[layer], ssm_conv_b[layer],
                             ssm_dt_bias[layer], ssm_a_log[layer], ssm_d[layer], ssm_norm_w[layer],
                             sconv_w[layer], w_gate[layer], b_gate[layer], w_branch[layer], w_o[layer],
                             cos, sin, need_ctx)
        x = x + mod[2][:, None] * y
        h2 = _modulate(_rmsnorm(x, norm2_w[layer]), mod[3][:, None], mod[4][:, None])
        if need_ctx:
            xc = xc + mod_c[2] * yc
            h2c = _modulate(_rmsnorm(xc, norm2_w[layer]), mod_c[3], mod_c[4])
            tokens = jnp.concatenate([h2.reshape(-1, d), h2c.reshape(-1, d)], axis=0)
            f = _moe(tokens, w_router, router_bias, moe_w1[layer], moe_w3[layer], moe_w2[layer])
            x = x + mod[5][:, None] * f[:b * s].reshape(b, s, d)
            xc = xc + mod_c[5] * f[b * s:].reshape(b, lc, d)
        else:
            f = _moe(h2.reshape(-1, d), w_router, router_bias, moe_w1[layer], moe_w3[layer], moe_w2[layer])
            x = x + mod[5][:, None] * f.reshape(b, s, d)
    return _rmsnorm(x, final_norm_w)
```

```python
import functools
import math

import numpy as np
import jax
import jax.numpy as jnp
from jax import lax
from jax.experimental import pallas as pl
from jax.experimental.pallas import tpu as pltpu

F32 = jnp.float32
BF16 = jnp.bfloat16
I32 = jnp.int32

EPS = 1e-6
GRID_W = 64
ROPE_THETA = 10000.0
HEAD_DIM = 64
ATT_HEADS = 8
ATT_KV_HEADS = 2
ATT_GROUP = ATT_HEADS // ATT_KV_HEADS
WINDOW = 128
SSM_HEADS = 8
SSM_HEAD_DIM = 64
SSM_INNER = SSM_HEADS * SSM_HEAD_DIM
SSM_GROUPS = 2
SSM_STATE = 64
SSM_BC_W = SSM_GROUPS * SSM_STATE
SSM_XBC_W = SSM_INNER + 2 * SSM_BC_W
SSM_CHUNK = 128
FNET_GROUPS = 8
FNET_GROUP_DIM = 64
FNET_WIDTH = FNET_GROUPS * FNET_GROUP_DIM
SCONV_WIDTH = 512
BRANCH_W = 512
N_BRANCHES = 4
N_EXPERTS = 32
N_EXPERT_GROUPS = 8
EXPERTS_PER_GROUP = N_EXPERTS // N_EXPERT_GROUPS
D_EXPERT = 512

LANES = 128
BF16_ROWS = 16
VMEM_LIMIT = 56 * 1024 * 1024
FFT_N2 = 128
MOE_TILE = 256

Q_OFF, K_OFF, V_OFF, Z_OFF, XBC_OFF, DT_OFF, FU_OFF, SU_OFF = 0, 512, 640, 768, 1280, 2048, 2176, 2688
IN_W_PAD = SU_OFF + 3 * SCONV_WIDTH
DT_W = 2 * SSM_HEADS


def _dot(a, b):
    return jnp.dot(a.astype(BF16), b.astype(BF16), preferred_element_type=F32)


def _dot_nt(a, b):
    return lax.dot_general(a.astype(BF16), b.astype(BF16), (((1,), (1,)), ((), ())),
                           preferred_element_type=F32)


def _dot_f32(a, b):
    return jnp.dot(a, b, preferred_element_type=F32, precision=lax.Precision.HIGHEST)


def _sigmoid(x):
    return 1.0 / (1.0 + jnp.exp(-x))


def _silu(x):
    return x * _sigmoid(x)


def _norm_mod(x, g, shift, scale):
    y = x * lax.rsqrt(jnp.mean(x * x, axis=-1, keepdims=True) + EPS) * g
    return y * (1.0 + scale) + shift


def _params(*sem):
    return pltpu.CompilerParams(dimension_semantics=sem, vmem_limit_bytes=VMEM_LIMIT)


def _const_spec(shape):
    n = len(shape)
    return pl.BlockSpec(shape, lambda *_: (0,) * n, pipeline_mode=pl.Buffered(1))


def _ada_kernel(c_ref, w_ref, b_ref, o_ref):
    o_ref[0] = _dot(_silu(c_ref[...]), w_ref[0]) + b_ref[0]


def _ada(c8, w_ada, b_ada):
    nl, d, w = w_ada.shape
    tn = 512
    return pl.pallas_call(
        _ada_kernel, grid=(nl, w // tn),
        in_specs=[pl.BlockSpec((8, d), lambda l, j: (0, 0)),
                  pl.BlockSpec((1, d, tn), lambda l, j: (l, 0, j)),
                  pl.BlockSpec((1, 1, tn), lambda l, j: (l, 0, j))],
        out_specs=pl.BlockSpec((1, 8, tn), lambda l, j: (l, 0, j)),
        out_shape=jax.ShapeDtypeStruct((nl, 8, w), F32),
        compiler_params=_params("arbitrary", "arbitrary"), name="ada",
    )(c8, w_ada, b_ada.reshape(nl, 1, w))


def _inproj_kernel(x_ref, mod_ref, g_ref, w_ref, cos_ref, sin_ref,
                   q_ref, k_ref, v_ref, z_ref, xbc_ref, dt_ref, fu_ref, su_ref):
    m = mod_ref[0]
    hb = _norm_mod(x_ref[...], g_ref[...], m[0:1], m[1:2]).astype(BF16)
    cos = cos_ref[...]
    sin = sin_ref[...]
    lane = lax.broadcasted_iota(I32, cos.shape, 1)
    first_half = (lane % HEAD_DIM) < HEAD_DIM // 2

    def proj(start, width):
        return jnp.dot(hb, w_ref[:, start:start + width], preferred_element_type=F32)

    def rope(r):
        rot = jnp.where(first_half, pltpu.roll(r, LANES - HEAD_DIM // 2, 1), pltpu.roll(r, HEAD_DIM // 2, 1))
        return r * cos + rot * sin

    scale = HEAD_DIM ** -0.5
    for j in range(ATT_HEADS // 2):
        r = rope(proj(Q_OFF + j * LANES, LANES)) * scale
        q_ref[0, 2 * j] = r[:, :HEAD_DIM].astype(BF16)
        q_ref[0, 2 * j + 1] = r[:, HEAD_DIM:].astype(BF16)
    r = rope(proj(K_OFF, LANES))
    k_ref[0, 0] = r[:, :HEAD_DIM].astype(BF16)
    k_ref[0, 1] = r[:, HEAD_DIM:].astype(BF16)
    r = proj(V_OFF, LANES)
    v_ref[0, 0] = r[:, :HEAD_DIM].astype(BF16)
    v_ref[0, 1] = r[:, HEAD_DIM:].astype(BF16)
    z_ref[0] = proj(Z_OFF, SSM_INNER).astype(BF16)
    xbc_ref[0] = proj(XBC_OFF, SSM_XBC_W).astype(BF16)
    dt_ref[0] = proj(DT_OFF, LANES)
    fu_ref[0] = proj(FU_OFF, FNET_WIDTH).astype(BF16)
    su_ref[0] = proj(SU_OFF, 3 * SCONV_WIDTH).astype(BF16)


def _inproj(xflat, nb, seq, t, mod, mod_row, g, w, cos, sin):
    d = xflat.shape[1]
    nt = seq // t
    if mod_row is None:
        mod_map = lambda b, i: (b, 0, 0)
    else:
        mod_map = lambda b, i: (mod_row, 0, 0)
    sds = jax.ShapeDtypeStruct
    outs = (sds((nb, ATT_HEADS, seq, HEAD_DIM), BF16), sds((nb, ATT_KV_HEADS, seq, HEAD_DIM), BF16),
            sds((nb, ATT_KV_HEADS, seq, HEAD_DIM), BF16), sds((nb, seq, SSM_INNER), BF16),
            sds((nb, seq, SSM_XBC_W), BF16), sds((nb, seq, LANES), F32),
            sds((nb, seq, FNET_WIDTH), BF16), sds((nb, seq, 3 * SCONV_WIDTH), BF16))
    head_spec = lambda nh: pl.BlockSpec((1, nh, t, HEAD_DIM), lambda b, i: (b, 0, i, 0))
    row_spec = lambda wd: pl.BlockSpec((1, t, wd), lambda b, i: (b, i, 0))
    return pl.pallas_call(
        _inproj_kernel, grid=(nb, nt),
        in_specs=[pl.BlockSpec((t, d), lambda b, i: (b * nt + i, 0)),
                  pl.BlockSpec((1, 8, d), mod_map),
                  _const_spec((1, d)), _const_spec((d, IN_W_PAD)),
                  pl.BlockSpec((t, LANES), lambda b, i: (i, 0)),
                  pl.BlockSpec((t, LANES), lambda b, i: (i, 0))],
        out_specs=(head_spec(ATT_HEADS), head_spec(ATT_KV_HEADS), head_spec(ATT_KV_HEADS),
                   row_spec(SSM_INNER), row_spec(SSM_XBC_W), row_spec(LANES),
                   row_spec(FNET_WIDTH), row_spec(3 * SCONV_WIDTH)),
        out_shape=outs, compiler_params=_params("arbitrary", "arbitrary"), name="inproj",
    )(xflat, mod, g, w, cos, sin)


def _attn_kernel(sink_ref, q_ref, *refs, tq, seq, band):
    if band:
        kp_ref, kc_ref, kn_ref, vp_ref, vc_ref, vn_ref, kx_ref, vx_ref, o_ref = refs
    else:
        kx_ref, vx_ref, o_ref = refs
    kh = pl.program_id(1)
    i = pl.program_id(2)
    q = q_ref[0].reshape(ATT_GROUP * tq, HEAD_DIM)
    s_ctx = _dot_nt(q, kx_ref[0, 0])
    sink_col = jnp.concatenate(
        [jnp.full((tq, 1), sink_ref[kh * ATT_GROUP + g], F32) for g in range(ATT_GROUP)], axis=0)
    m = jnp.maximum(jnp.max(s_ctx, axis=1, keepdims=True), sink_col)
    if band:
        kb = jnp.concatenate([kp_ref[0, 0], kc_ref[0, 0], kn_ref[0, 0]], axis=0)
        vb = jnp.concatenate([vp_ref[0, 0], vc_ref[0, 0], vn_ref[0, 0]], axis=0)
        s_loc = _dot_nt(q, kb)
        row = lax.broadcasted_iota(I32, s_loc.shape, 0) & (tq - 1)
        col = lax.broadcasted_iota(I32, s_loc.shape, 1)
        rel = col - WINDOW - row
        kpos = i * tq - WINDOW + col
        bad = (jnp.abs(rel) > WINDOW) | (kpos < 0) | (kpos >= seq)
        s_loc = jnp.where(bad, -jnp.inf, s_loc)
        m = jnp.maximum(m, jnp.max(s_loc, axis=1, keepdims=True))
    p_ctx = jnp.exp(s_ctx - m)
    den = jnp.sum(p_ctx, axis=1, keepdims=True) + jnp.exp(sink_col - m)
    acc = _dot(p_ctx, vx_ref[0, 0])
    if band:
        p_loc = jnp.exp(s_loc - m)
        den = den + jnp.sum(p_loc, axis=1, keepdims=True)
        acc = acc + _dot(p_loc, vb)
    out = acc / den
    o_ref[0] = jnp.concatenate([out[g * tq:(g + 1) * tq] for g in range(ATT_GROUP)], axis=1).astype(BF16)


def _attention(sink, q, k, v, kx, vx, tq, band):
    nb, _, seq, _ = q.shape
    lc = kx.shape[2]
    nt = seq // tq
    r = tq // WINDOW
    nblk = seq // WINDOW
    qs = pl.BlockSpec((1, ATT_GROUP, tq, HEAD_DIM), lambda b, h, i: (b, h, i, 0))
    cur = pl.BlockSpec((1, 1, tq, HEAD_DIM), lambda b, h, i: (b, h, i, 0))
    prev = pl.BlockSpec((1, 1, WINDOW, HEAD_DIM), lambda b, h, i: (b, h, jnp.maximum(i * r - 1, 0), 0))
    nxt = pl.BlockSpec((1, 1, WINDOW, HEAD_DIM), lambda b, h, i: (b, h, jnp.minimum((i + 1) * r, nblk - 1), 0))
    ctx = pl.BlockSpec((1, 1, lc, HEAD_DIM), lambda b, h, i: (b, h, 0, 0))
    smem = pl.BlockSpec(memory_space=pltpu.SMEM)
    if band:
        in_specs = [smem, qs, prev, cur, nxt, prev, cur, nxt, ctx, ctx]
        args = (sink, q, k, k, k, v, v, v, kx, vx)
    else:
        in_specs = [smem, qs, ctx, ctx]
        args = (sink, q, kx, vx)
    return pl.pallas_call(
        functools.partial(_attn_kernel, tq=tq, seq=seq, band=band),
        grid=(nb, ATT_KV_HEADS, nt), in_specs=in_specs,
        out_specs=pl.BlockSpec((1, tq, ATT_GROUP * HEAD_DIM), lambda b, h, i: (b, i, h)),
        out_shape=jax.ShapeDtypeStruct((nb, seq, ATT_HEADS * HEAD_DIM), BF16),
        compiler_params=_params("arbitrary", "arbitrary", "arbitrary"), name="attn_band" if band else "attn_ctx",
    )(*args)


def _ssd_kernel(xp_ref, xc_ref, xn_ref, dt_ref, cw_ref, cb_ref, dtb_ref, alog_ref, e_ref, init_ref,
                *refs, reverse, with_y, finalize, nc):
    if finalize:
        yf_ref, z_ref, dsk_ref, nw_ref, y_ref, st_ref = refs
    elif with_y:
        y_ref, st_ref = refs
    else:
        (st_ref,) = refs
    q = SSM_CHUNK
    c = pl.program_id(1)
    ce = (nc - 1 - c) if reverse else c

    @pl.when(c == 0)
    def _():
        st_ref[...] = init_ref[...]

    xc = xc_ref[0].astype(F32)
    prow = jnp.where(ce == 0, 0.0, xp_ref[0].astype(F32)[BF16_ROWS - 1:BF16_ROWS])
    nrow = jnp.where(ce == nc - 1, 0.0, xn_ref[0].astype(F32)[0:1])
    rid = lax.broadcasted_iota(I32, xc.shape, 0)
    up = jnp.where(rid == 0, prow, pltpu.roll(xc, 1, 0))
    dn = jnp.where(rid == q - 1, nrow, pltpu.roll(xc, q - 1, 0))
    cw = cw_ref[...]
    act = _silu(cw[0:1] * up + cw[1:2] * xc + cw[2:3] * dn + cb_ref[...])
    xs = act[:, :SSM_INNER]
    bmat = act[:, SSM_INNER:SSM_INNER + SSM_BC_W]
    cmat = act[:, SSM_INNER + SSM_BC_W:]

    pre = dt_ref[0] + dtb_ref[...]
    dt = jnp.maximum(pre, 0.0) + jnp.log1p(jnp.exp(-jnp.abs(pre)))
    dta = dt * (-jnp.exp(alog_ref[...]))
    ri = lax.broadcasted_iota(I32, (q, q), 0)
    ci = lax.broadcasted_iota(I32, (q, q), 1)
    tri = (ri <= ci) if reverse else (ri >= ci)
    acs = _dot_f32(tri.astype(F32), dta)
    e = e_ref[...]
    dt_e = _dot_f32(dt, e)
    acs_e = _dot_f32(acs, e)
    xd = xs * dt_e
    tot_e = acs_e[0:1] if reverse else acs_e[q - 1:q]
    col0 = SSM_HEADS if reverse else 0

    def grp(mat, h):
        g = h // (SSM_HEADS // SSM_GROUPS)
        return mat[:, g * SSM_STATE:(g + 1) * SSM_STATE]

    def head(mat, h):
        return mat[:, h * SSM_HEAD_DIM:(h + 1) * SSM_HEAD_DIM]

    if with_y:
        acs_t = acs.T
        cb = [_dot_nt(cmat[:, g * SSM_STATE:(g + 1) * SSM_STATE], bmat[:, g * SSM_STATE:(g + 1) * SSM_STATE])
              for g in range(SSM_GROUPS)]
        y_diag, y_off = [], []
        for h in range(SSM_HEADS):
            seg = acs[:, col0 + h:col0 + h + 1] - acs_t[col0 + h:col0 + h + 1, :]
            lmat = jnp.exp(jnp.where(tri, seg, -jnp.inf))
            scores = cb[h // (SSM_HEADS // SSM_GROUPS)] * lmat
            y_diag.append(_dot(scores, head(xd, h)))
            y_off.append(_dot_nt(grp(cmat, h), st_ref[0, h]))
        y = jnp.concatenate(y_diag, axis=1) + jnp.concatenate(y_off, axis=1) * jnp.exp(acs_e)

    xdd_t = (xd * jnp.exp(tot_e - acs_e)).T
    for h in range(SSM_HEADS):
        upd = _dot(xdd_t[h * SSM_HEAD_DIM:(h + 1) * SSM_HEAD_DIM, :], grp(bmat, h))
        st_ref[0, h] = st_ref[0, h] * jnp.exp(head(tot_e, h)) + upd

    if finalize:
        yt = yf_ref[0] + y + dsk_ref[...] * xs
        yt = yt * _silu(z_ref[0].astype(F32))
        yt = yt * lax.rsqrt(jnp.mean(yt * yt, axis=-1, keepdims=True) + EPS) * nw_ref[...]
        y_ref[0] = yt.astype(BF16)
    elif with_y:
        y_ref[0] = y


def _ssd(xbc, dtr, cw, cb, dtb, alog, e, init, reverse, with_y, fin=None):
    nb, seq, _ = xbc.shape
    nc = seq // SSM_CHUNK
    per = SSM_CHUNK // BF16_ROWS
    nhalo = seq // BF16_ROWS
    ce = (lambda c: nc - 1 - c) if reverse else (lambda c: c)
    xw = SSM_XBC_W
    in_specs = [pl.BlockSpec((1, BF16_ROWS, xw), lambda b, c: (b, jnp.maximum(ce(c) * per - 1, 0), 0)),
                pl.BlockSpec((1, SSM_CHUNK, xw), lambda b, c: (b, ce(c), 0)),
                pl.BlockSpec((1, BF16_ROWS, xw), lambda b, c: (b, jnp.minimum((ce(c) + 1) * per, nhalo - 1), 0)),
                pl.BlockSpec((1, SSM_CHUNK, LANES), lambda b, c: (b, ce(c), 0)),
                _const_spec((8, xw)), _const_spec((1, xw)), _const_spec((1, LANES)), _const_spec((1, LANES)),
                _const_spec((LANES, SSM_INNER)),
                pl.BlockSpec((1, SSM_HEADS, SSM_HEAD_DIM, SSM_STATE), lambda b, c: (b, 0, 0, 0))]
    args = [xbc, xbc, xbc, dtr, cw, cb, dtb, alog, e, init]
    st_spec = pl.BlockSpec((1, SSM_HEADS, SSM_HEAD_DIM, SSM_STATE), lambda b, c: (b, 0, 0, 0))
    st_shape = jax.ShapeDtypeStruct((nb, SSM_HEADS, SSM_HEAD_DIM, SSM_STATE), F32)
    y_spec = pl.BlockSpec((1, SSM_CHUNK, SSM_INNER), lambda b, c: (b, ce(c), 0))
    finalize = fin is not None
    if finalize:
        yf, z, dsk, nw = fin
        in_specs += [y_spec, y_spec, _const_spec((1, SSM_INNER)), _const_spec((1, SSM_INNER))]
        args += [yf, z, dsk, nw]
    if with_y:
        out_specs = (y_spec, st_spec)
        out_shape = (jax.ShapeDtypeStruct((nb, seq, SSM_INNER), BF16 if finalize else F32), st_shape)
    else:
        out_specs = (st_spec,)
        out_shape = (st_shape,)
    res = pl.pallas_call(
        functools.partial(_ssd_kernel, reverse=reverse, with_y=with_y, finalize=finalize, nc=nc),
        grid=(nb, nc), in_specs=in_specs, out_specs=out_specs, out_shape=out_shape,
        compiler_params=_params("arbitrary", "arbitrary"), name="ssd_bwd" if reverse else "ssd_fwd",
    )(*args)
    return res if with_y else (None, res[0])


def _fft_a_kernel(u_ref, wc_ref, m1_ref, y_ref):
    v = _dot(u_ref[0], wc_ref[...])
    vs = jnp.concatenate([v[:, :FNET_WIDTH], v[:, FNET_WIDTH:]], axis=0)
    b = _dot(m1_ref[0], vs)
    y_ref[0, 0, 0] = b[:FFT_N2].astype(BF16)
    y_ref[0, 1, 0] = b[FFT_N2:].astype(BF16)


def _fft_c_kernel(y_ref, m2_ref, o_ref):
    o_ref[0] = _dot(m2_ref[...], y_ref[0]).astype(BF16)


def _fft_small_kernel(u_ref, wc_ref, m_ref, o_ref):
    v = _dot(u_ref[0], wc_ref[...])
    vs = jnp.concatenate([v[:, :FNET_WIDTH], v[:, FNET_WIDTH:]], axis=0)
    o_ref[0] = _dot(m_ref[...], vs).astype(BF16)


def _channel_dft():
    idx = np.arange(FNET_GROUP_DIM)
    ang = 2.0 * np.pi * np.outer(idx, idx) / FNET_GROUP_DIM
    eye = np.eye(FNET_GROUPS)
    return jnp.asarray(np.concatenate([np.kron(eye, np.cos(ang)), -np.kron(eye, np.sin(ang))], axis=1), BF16)


def _fourier_long(u):
    nb, seq, w = u.shape
    n2 = FFT_N2
    n1 = seq // n2
    t2 = np.arange(n2)
    k2 = np.arange(n2)
    t1 = np.arange(n1)
    ang = 2.0 * np.pi * (np.outer(k2, t2)[None] / n2 + (t1[:, None, None] * k2[None, :, None]) / seq)
    co, si = np.cos(ang), np.sin(ang)
    m1 = jnp.asarray(np.concatenate([np.concatenate([co, si], axis=2),
                                     np.concatenate([-si, co], axis=2)], axis=1), BF16)
    ang1 = 2.0 * np.pi * np.outer(t1, t1) / n1
    norm = 1.0 / math.sqrt(seq * FNET_GROUP_DIM)
    m2 = jnp.asarray(np.concatenate([np.cos(ang1), np.sin(ang1)], axis=1) * norm, BF16)
    y = pl.pallas_call(
        _fft_a_kernel, grid=(nb, n1),
        in_specs=[pl.BlockSpec((1, n2, w), lambda b, j: (b, 0, j)),
                  _const_spec((w, 2 * w)),
                  pl.BlockSpec((1, 2 * n2, 2 * n2), lambda b, j: (j, 0, 0))],
        out_specs=pl.BlockSpec((1, 2, 1, n2, w), lambda b, j: (b, 0, j, 0, 0)),
        out_shape=jax.ShapeDtypeStruct((nb, 2, n1, n2, w), BF16),
        compiler_params=_params("arbitrary", "arbitrary"), name="fft_a",
    )(u.reshape(nb, n2, n1 * w), _channel_dft(), m1)
    nl = 2048
    out = pl.pallas_call(
        _fft_c_kernel, grid=(nb, n2 * w // nl),
        in_specs=[pl.BlockSpec((1, 2 * n1, nl), lambda b, j: (b, 0, j)),
                  _const_spec((n1, 2 * n1))],
        out_specs=pl.BlockSpec((1, n1, nl), lambda b, j: (b, 0, j)),
        out_shape=jax.ShapeDtypeStruct((nb, n1, n2 * w), BF16),
        compiler_params=_params("arbitrary", "arbitrary"), name="fft_c",
    )(y.reshape(nb, 2 * n1, n2 * w), m2)
    return out.reshape(nb, seq, w)


def _fourier_short(u):
    nb, seq, w = u.shape
    t = np.arange(seq)
    ang = 2.0 * np.pi * np.outer(t, t) / seq
    norm = 1.0 / math.sqrt(seq * FNET_GROUP_DIM)
    m = jnp.asarray(np.concatenate([np.cos(ang), np.sin(ang)], axis=1) * norm, BF16)
    return pl.pallas_call(
        _fft_small_kernel, grid=(nb,),
        in_specs=[pl.BlockSpec((1, seq, w), lambda b: (b, 0, 0)),
                  _const_spec((w, 2 * w)), _const_spec((seq, 2 * seq))],
        out_specs=pl.BlockSpec((1, seq, w), lambda b: (b, 0, 0)),
        out_shape=jax.ShapeDtypeStruct((nb, seq, w), BF16),
        compiler_params=_params("arbitrary"), name="fft_small",
    )(u, _channel_dft(), m)


def _merge_kernel(x_ref, mod_ref, g1_ref, g2_ref, att_ref, ssm_ref, fft_ref, sup_ref, su_ref, sun_ref,
                  sw_ref, wg_ref, bg_ref, wb_ref, wo_ref, x1_ref, h2_ref, *, nt):
    d = x_ref.shape[1]
    t = x_ref.shape[0]
    i = pl.program_id(1)
    x = x_ref[...]
    m = mod_ref[0]
    hb = _norm_mod(x, g1_ref[...], m[0:1], m[1:2]).astype(BF16)

    def gated(su):
        return su[:, SCONV_WIDTH:2 * SCONV_WIDTH] * su[:, 2 * SCONV_WIDTH:]

    su = su_ref[0].astype(F32)
    p = gated(su)
    prow = jnp.where(i == 0, 0.0, gated(sup_ref[0].astype(F32)[BF16_ROWS - 1:BF16_ROWS]))
    nrow = jnp.where(i == nt - 1, 0.0, gated(sun_ref[0].astype(F32)[0:1]))
    rid = lax.broadcasted_iota(I32, p.shape, 0)
    up = jnp.where(rid == 0, prow, pltpu.roll(p, 1, 0))
    dn = jnp.where(rid == t - 1, nrow, pltpu.roll(p, t - 1, 0))
    sw = sw_ref[...]
    sconv = su[:, :SCONV_WIDTH] * (sw[0:1] * up + sw[1:2] * p + sw[2:3] * dn)

    branches = (att_ref[0], ssm_ref[0], fft_ref[0], sconv.astype(BF16))
    acc = jnp.zeros((t, d), F32)
    for n in range(N_BRANCHES):
        gate = _sigmoid(jnp.dot(hb, wg_ref[:, n * d:(n + 1) * d], preferred_element_type=F32)
                        + bg_ref[:, n * d:(n + 1) * d])
        acc = acc + gate * jnp.dot(branches[n], wb_ref[n], preferred_element_type=F32)
    y = jnp.dot(acc.astype(BF16), wo_ref[...], preferred_element_type=F32)
    x1 = x + m[2:3] * y
    x1_ref[...] = x1
    h2_ref[...] = _norm_mod(x1, g2_ref[...], m[3:4], m[4:5])


def _merge(xflat, nb, seq, t, mod, mod_row, g1, g2, att, ssm, fft, su, sw, wg, bg, wb, wo):
    ntok, d = xflat.shape
    nt = seq // t
    per = t // BF16_ROWS
    nhalo = seq // BF16_ROWS
    mod_map = (lambda b, i: (b, 0, 0)) if mod_row is None else (lambda b, i: (mod_row, 0, 0))
    row = pl.BlockSpec((t, d), lambda b, i: (b * nt + i, 0))
    br = pl.BlockSpec((1, t, BRANCH_W), lambda b, i: (b, i, 0))
    suw = 3 * SCONV_WIDTH
    in_specs = [row, pl.BlockSpec((1, 8, d), mod_map), _const_spec((1, d)), _const_spec((1, d)),
                br, br, br,
                pl.BlockSpec((1, BF16_ROWS, suw), lambda b, i: (b, jnp.maximum(i * per - 1, 0), 0)),
                pl.BlockSpec((1, t, suw), lambda b, i: (b, i, 0)),
                pl.BlockSpec((1, BF16_ROWS, suw), lambda b, i: (b, jnp.minimum((i + 1) * per, nhalo - 1), 0)),
                _const_spec((8, SCONV_WIDTH)), _const_spec((d, N_BRANCHES * d)), _const_spec((1, N_BRANCHES * d)),
                _const_spec((N_BRANCHES, BRANCH_W, d)), _const_spec((d, d))]
    return pl.pallas_call(
        functools.partial(_merge_kernel, nt=nt),
        grid=(nb, nt), in_specs=in_specs, out_specs=(row, row),
        out_shape=(jax.ShapeDtypeStruct((ntok, d), F32), jax.ShapeDtypeStruct((ntok, d), F32)),
        compiler_params=_params("arbitrary", "arbitrary"), name="merge",
    )(xflat, mod, g1, g2, att, ssm, fft, su, su, su, sw, wg, bg, wb, wo)


def _route_kernel(h_ref, wr_ref, rb_ref, cin_ref, ri_ref, rw_ref, cnt_ref):
    t = h_ref.shape[0]
    step = pl.program_id(0)

    @pl.when(step == 0)
    def _():
        cnt_ref[...] = cin_ref[...]

    sc = _sigmoid(_dot(h_ref[...], wr_ref[...]))
    sel = sc + rb_ref[...]
    s = [sel[:, j * LANES:(j + 1) * LANES] for j in range(EXPERTS_PER_GROUP)]
    u = [sc[:, j * LANES:(j + 1) * LANES] for j in range(EXPERTS_PER_GROUP)]
    gs = None
    for a in range(EXPERTS_PER_GROUP):
        for b in range(a + 1, EXPERTS_PER_GROUP):
            pair = s[a] + s[b]
            gs = pair if gs is None else jnp.maximum(gs, pair)
    lane = lax.broadcasted_iota(I32, (t, LANES), 1)
    gmax = jnp.max(gs, axis=1, keepdims=True)
    best = jnp.min(jnp.where(gs == gmax, lane, LANES), axis=1, keepdims=True)
    on = lane == best
    v = [jnp.sum(jnp.where(on, s[j], 0.0), axis=1, keepdims=True) for j in range(EXPERTS_PER_GROUP)]
    w = [jnp.sum(jnp.where(on, u[j], 0.0), axis=1, keepdims=True) for j in range(EXPERTS_PER_GROUP)]

    def first_argmax(vals):
        mx = vals[0]
        for x in vals[1:]:
            mx = jnp.maximum(mx, x)
        idx = jnp.full_like(best, len(vals) - 1)
        for j in range(len(vals) - 2, -1, -1):
            idx = jnp.where(vals[j] == mx, j, idx)
        return idx

    def pick(vals, idx):
        out = vals[-1]
        for j in range(len(vals) - 2, -1, -1):
            out = jnp.where(idx == j, vals[j], out)
        return out

    l1 = first_argmax(v)
    l2 = first_argmax([jnp.where(l1 == j, -jnp.inf, v[j]) for j in range(EXPERTS_PER_GROUP)])
    w1, w2 = pick(w, l1), pick(w, l2)
    tot = w1 + w2
    e1 = best * EXPERTS_PER_GROUP + l1
    e2 = best * EXPERTS_PER_GROUP + l2

    oh1 = (lane == e1).astype(F32)
    oh2 = (lane == e2).astype(F32)
    oh = oh1 + oh2
    ri_ = lax.broadcasted_iota(I32, (t, t), 0)
    ci_ = lax.broadcasted_iota(I32, (t, t), 1)
    before = _dot((ri_ > ci_).astype(F32), oh) + cnt_ref[0:1]
    r1 = jnp.sum(oh1 * before, axis=1, keepdims=True).astype(I32)
    r2 = jnp.sum(oh2 * before, axis=1, keepdims=True).astype(I32)
    cnt_ref[...] = cnt_ref[...] + jnp.sum(oh, axis=0, keepdims=True)

    ri_ref[...] = jnp.where(lane == 0, e1, jnp.where(lane == 1, e2, jnp.where(lane == 2, r1, jnp.where(lane == 3, r2, 0))))
    rw_ref[...] = jnp.where(lane == 0, w1 / tot, jnp.where(lane == 1, w2 / tot, 0.0))


def _route(h2, wr, rb, cnt_in):
    ntok, d = h2.shape
    t = 512 if ntok % 512 == 0 else 256
    return pl.pallas_call(
        _route_kernel, grid=(ntok // t,),
        in_specs=[pl.BlockSpec((t, d), lambda i: (i, 0)),
                  _const_spec((d, EXPERTS_PER_GROUP * LANES)), _const_spec((1, EXPERTS_PER_GROUP * LANES)),
                  _const_spec((8, LANES))],
        out_specs=(pl.BlockSpec((t, LANES), lambda i: (i, 0)), pl.BlockSpec((t, LANES), lambda i: (i, 0)),
                   pl.BlockSpec((8, LANES), lambda i: (0, 0))),
        out_shape=(jax.ShapeDtypeStruct((ntok, LANES), I32), jax.ShapeDtypeStruct((ntok, LANES), F32),
                   jax.ShapeDtypeStruct((8, LANES), F32)),
        compiler_params=_params("arbitrary"), name="route",
    )(h2, wr, rb, cnt_in)


def _dispatch_kernel(dest_ref, h_hbm, buf_hbm, xs_hbm, sem, *, td):
    del buf_hbm
    base = pl.program_id(0) * td

    def row_copy(src_row, dst_row):
        return pltpu.make_async_copy(h_hbm.at[pl.ds(src_row, 1)], xs_hbm.at[pl.ds(dst_row, 1)], sem)

    def issue(j, carry):
        row_copy(base + j, dest_ref[0, 0, 2 * j]).start()
        row_copy(base + j, dest_ref[0, 0, 2 * j + 1]).start()
        return carry

    def drain(j, carry):
        row_copy(0, 0).wait()
        return carry

    lax.fori_loop(0, td, issue, 0)
    lax.fori_loop(0, 2 * td, drain, 0)


def _dispatch(h2, dest, xs_buf):
    ntok, d = h2.shape
    nrows = xs_buf.shape[0]
    td = 256
    nt = ntok // td
    return pl.pallas_call(
        functools.partial(_dispatch_kernel, td=td), grid=(nt,),
        in_specs=[pl.BlockSpec((1, 1, 2 * td), lambda i: (i, 0, 0), memory_space=pltpu.SMEM),
                  pl.BlockSpec(memory_space=pl.ANY), pl.BlockSpec(memory_space=pl.ANY)],
        out_specs=pl.BlockSpec(memory_space=pl.ANY),
        out_shape=jax.ShapeDtypeStruct((nrows, d), F32),
        scratch_shapes=[pltpu.SemaphoreType.DMA(())],
        input_output_aliases={2: 0},
        compiler_params=_params("arbitrary"), name="dispatch",
    )(dest.reshape(nt, 1, 2 * td), h2, xs_buf)


def _ffn_kernel(be_ref, nu_ref, x_ref, w1_ref, w3_ref, w2_ref, y_ref):
    i = pl.program_id(0)

    @pl.when(i < nu_ref[0])
    def _():
        xb = x_ref[...].astype(BF16)
        a = jnp.dot(xb, w1_ref[0], preferred_element_type=F32)
        b = jnp.dot(xb, w3_ref[0], preferred_element_type=F32)
        y_ref[...] = jnp.dot((_silu(a) * b).astype(BF16), w2_ref[0], preferred_element_type=F32)

    @pl.when(i >= nu_ref[0])
    def _():
        y_ref[...] = jnp.zeros_like(y_ref)


def _ffn(xs, block_expert, n_used, w1, w3, w2):
    nrows, d = xs.shape
    de = w1.shape[2]
    tm = MOE_TILE
    grid_spec = pltpu.PrefetchScalarGridSpec(
        num_scalar_prefetch=2, grid=(nrows // tm,),
        in_specs=[pl.BlockSpec((tm, d), lambda i, be, nu: (jnp.minimum(i, nu[0] - 1), 0)),
                  pl.BlockSpec((1, d, de), lambda i, be, nu: (be[i], 0, 0)),
                  pl.BlockSpec((1, d, de), lambda i, be, nu: (be[i], 0, 0)),
                  pl.BlockSpec((1, de, d), lambda i, be, nu: (be[i], 0, 0))],
        out_specs=pl.BlockSpec((tm, d), lambda i, be, nu: (i, 0)))
    return pl.pallas_call(
        _ffn_kernel, grid_spec=grid_spec, out_shape=jax.ShapeDtypeStruct((nrows, d), F32),
        compiler_params=_params("arbitrary"), name="ffn",
    )(block_expert, n_used, xs, w1, w3, w2)


def _combine_kernel(dest_ref, x_ref, mod_ref, rw_ref, fw_ref, y_hbm, o_ref, buf, sem, *, tc, final):
    def row_copy(src_row, k, j):
        return pltpu.make_async_copy(y_hbm.at[pl.ds(src_row, 1)], buf.at[k, pl.ds(j, 1)], sem)

    def issue(j, carry):
        row_copy(dest_ref[0, 0, 2 * j], 0, j).start()
        row_copy(dest_ref[0, 0, 2 * j + 1], 1, j).start()
        return carry

    def drain(j, carry):
        row_copy(0, 0, 0).wait()
        return carry

    lax.fori_loop(0, tc, issue, 0)
    lax.fori_loop(0, 2 * tc, drain, 0)
    rw = rw_ref[...]
    f = rw[:, 0:1] * buf[0] + rw[:, 1:2] * buf[1]
    x = x_ref[...] + mod_ref[0][5:6] * f
    if final:
        x = x * lax.rsqrt(jnp.mean(x * x, axis=-1, keepdims=True) + EPS) * fw_ref[...]
    o_ref[...] = x


def _combine(x1, dest, rw, y, mod, mod_row, seq, fw, final):
    ntok, d = x1.shape
    tc = 256
    nt = ntok // tc
    per_seq = seq // tc
    mod_map = (lambda i: (i // per_seq, 0, 0)) if mod_row is None else (lambda i: (mod_row, 0, 0))
    return pl.pallas_call(
        functools.partial(_combine_kernel, tc=tc, final=final), grid=(nt,),
        in_specs=[pl.BlockSpec((1, 1, 2 * tc), lambda i: (i, 0, 0), memory_space=pltpu.SMEM),
                  pl.BlockSpec((tc, d), lambda i: (i, 0)),
                  pl.BlockSpec((1, 8, d), mod_map),
                  pl.BlockSpec((tc, LANES), lambda i: (i, 0)),
                  _const_spec((1, d)),
                  pl.BlockSpec(memory_space=pl.ANY)],
        out_specs=pl.BlockSpec((tc, d), lambda i: (i, 0)),
        out_shape=jax.ShapeDtypeStruct((ntok, d), F32),
        scratch_shapes=[pltpu.VMEM((2, tc, d), F32), pltpu.SemaphoreType.DMA(())],
        compiler_params=_params("arbitrary"), name="combine",
    )(dest.reshape(nt, 1, 2 * tc), x1, mod, rw, fw, y)


def _moe(streams, mod, wr, rb, w1, w3, w2, fw, final):
    d = streams[0][0].shape[1]
    cnt = jnp.zeros((8, LANES), F32)
    routed = []
    for _, h2, _, _ in streams:
        ri, rw, cnt = _route(h2, wr, rb, cnt)
        routed.append((ri, rw))
    counts = cnt[0, :N_EXPERTS].astype(I32)
    padded = (counts + MOE_TILE - 1) // MOE_TILE * MOE_TILE
    pad_end = jnp.cumsum(padded)
    offs = pad_end - padded
    ntok = sum(s[0].shape[0] for s in streams)
    n_tiles = -(-2 * ntok // MOE_TILE) + N_EXPERTS
    block_expert = jnp.minimum(jnp.searchsorted(pad_end, jnp.arange(n_tiles, dtype=I32) * MOE_TILE, side="right"),
                               N_EXPERTS - 1).astype(I32)
    n_used = (pad_end[-1:] // MOE_TILE).astype(I32)
    dests = [(offs[ri[:, 0:2]] + ri[:, 2:4]).reshape(-1) for ri, _ in routed]
    xs = jnp.zeros((n_tiles * MOE_TILE, d), F32)
    for (_, h2, _, _), dest in zip(streams, dests):
        xs = _dispatch(h2, dest, xs)
    y = _ffn(xs, block_expert, n_used, w1, w3, w2)
    return [_combine(x1, dest, rw, y, mod, mod_row, seq, fw, final)
            for (x1, _, seq, mod_row), dest, (_, rw) in zip(streams, dests, routed)]


def _pad_rows(a, rows):
    return jnp.concatenate([a, jnp.zeros((rows - a.shape[0],) + a.shape[1:], a.dtype)], axis=0)


def _pad_cols(a, cols, value=0.0):
    return jnp.concatenate([a, jnp.full(a.shape[:-1] + (cols - a.shape[-1],), value, a.dtype)], axis=-1)


def _rope_tables(seq):
    rows = seq // GRID_W
    row = jnp.repeat(jnp.arange(rows, dtype=F32), GRID_W)
    col = jnp.tile(jnp.arange(GRID_W, dtype=F32), rows)
    pairs = HEAD_DIM // 4
    inv_freq = ROPE_THETA ** (-jnp.arange(pairs, dtype=F32) / pairs)
    ang = jnp.concatenate([row[:, None] * inv_freq, col[:, None] * inv_freq], axis=-1)
    cos, sin = jnp.cos(ang), jnp.sin(ang)
    cos_t = jnp.tile(cos, (1, LANES // (HEAD_DIM // 2)))
    sin_t = jnp.tile(jnp.concatenate([-sin, sin], axis=-1), (1, LANES // HEAD_DIM))
    return cos_t, sin_t


def _head_expand(reverse):
    e = np.zeros((LANES, SSM_INNER), np.float32)
    for h in range(SSM_HEADS):
        e[(SSM_HEADS if reverse else 0) + h, h * SSM_HEAD_DIM:(h + 1) * SSM_HEAD_DIM] = 1.0
    return jnp.asarray(e)


def kernel(x, c, ctx, c_ctx, norm1_w, norm2_w, w_ada, b_ada, w_in, attn_sink, ssm_conv_w, ssm_conv_b,
           ssm_dt_bias, ssm_a_log, ssm_d, ssm_norm_w, sconv_w, w_gate, b_gate, w_branch, w_o,
           w_router, router_bias, moe_w1, moe_w3, moe_w2, final_norm_w):
    nb, seq, d = x.shape
    lc = ctx.shape[1]
    depth = w_in.shape[0]
    n_lat = nb * seq
    n_ctx = nb * lc
    assert seq % 512 == 0 and seq % (FFT_N2 * 8) == 0 and lc % 256 == 0 and nb < 8

    xl = x.reshape(n_lat, d)
    xc = ctx.reshape(n_ctx, d)
    c8 = _pad_rows(jnp.concatenate([c, c_ctx[None]], axis=0), 8)
    mods = _ada(c8, w_ada, b_ada)
    mods = _pad_cols(mods.reshape(depth, 8, 6, d).swapaxes(-1, -2), 8).swapaxes(-1, -2)

    cos_l, sin_l = _rope_tables(seq)
    cos_c, sin_c = jnp.ones((lc, LANES), F32), jnp.zeros((lc, LANES), F32)
    e_f, e_b = _head_expand(False), _head_expand(True)

    wr = w_router.reshape(d, N_EXPERT_GROUPS, EXPERTS_PER_GROUP).transpose(0, 2, 1)
    wr = _pad_cols(wr, LANES).reshape(d, EXPERTS_PER_GROUP * LANES).astype(BF16)
    rb = _pad_cols(router_bias.reshape(N_EXPERT_GROUPS, EXPERTS_PER_GROUP).T, LANES, -1e30).reshape(1, -1)

    for layer in range(depth):
        last = layer + 1 == depth
        mod = mods[layer]
        wi = w_in[layer]
        w_pad = jnp.concatenate([wi[:, :2048], _pad_cols(wi[:, 2048:2048 + DT_W], LANES), wi[:, 2048 + DT_W:]],
                                axis=1).astype(BF16)
        g1 = norm1_w[layer][None]
        g2 = norm2_w[layer][None]
        cw = _pad_rows(ssm_conv_w[layer], 8)
        cb = ssm_conv_b[layer][None]
        dtb = _pad_cols(ssm_dt_bias[layer].reshape(1, DT_W), LANES)
        alog = _pad_cols(ssm_a_log[layer].reshape(1, DT_W), LANES)
        dsk = jnp.repeat(ssm_d[layer], SSM_HEAD_DIM)[None]
        nw = ssm_norm_w[layer][None]
        sw = _pad_rows(sconv_w[layer], 8)
        wg = w_gate[layer].astype(BF16)
        bg = b_gate[layer][None]
        wb = w_branch[layer].astype(BF16)
        wo = w_o[layer].astype(BF16)
        sink = attn_sink[layer]

        q, k, v, z, xbc, dtr, fu, su = _inproj(xl, nb, seq, 512, mod, None, g1, w_pad, cos_l, sin_l)
        qc, kc, vc, zc, xbcc, dtrc, fuc, suc = _inproj(xc, nb, lc, 256, mod, nb, g1, w_pad, cos_c, sin_c)

        att = _attention(sink, q, k, v, kc, vc, 256, True)

        zero_state = jnp.zeros((nb, SSM_HEADS, SSM_HEAD_DIM, SSM_STATE), F32)
        ssd_c = functools.partial(_ssd, xbcc, dtrc, cw, cb, dtb, alog)
        ssd_l = functools.partial(_ssd, xbc, dtr, cw, cb, dtb, alog)
        if last:
            _, fin_f = ssd_c(e_f, zero_state, False, False)
            _, fin_b = ssd_c(e_b, zero_state, True, False)
        else:
            yc_f, fin_f = ssd_c(e_f, zero_state, False, True)
            ssm_c, fin_b = ssd_c(e_b, zero_state, True, True, fin=(yc_f, zc, dsk, nw))
        y_f, _ = ssd_l(e_f, fin_f, False, True)
        ssm, _ = ssd_l(e_b, fin_b, True, True, fin=(y_f, z, dsk, nw))

        fft = _fourier_long(fu)

        x1, h2 = _merge(xl, nb, seq, 256, mod, None, g1, g2, att, ssm, fft, su, sw, wg, bg, wb, wo)
        streams = [(x1, h2, seq, None)]
        if not last:
            att_c = _attention(sink, qc, None, None, kc, vc, lc, False)
            fft_c = _fourier_short(fuc)
            x1c, h2c = _merge(xc, nb, lc, 256, mod, nb, g1, g2, att_c, ssm_c, fft_c, suc, sw, wg, bg, wb, wo)
            streams.append((x1c, h2c, lc, nb))
        new = _moe(streams, mod, wr, rb, moe_w1[layer].astype(BF16), moe_w3[layer].astype(BF16),
                   moe_w2[layer].astype(BF16), final_norm_w[None], last)
        xl = new[0]
        if not last:
            xc = new[1]
    return xl.reshape(nb, seq, d)
```

```python
import functools
import math

import numpy as np
import jax
import jax.numpy as jnp
from jax import lax
from jax.experimental import pallas as pl
from jax.experimental.pallas import tpu as pltpu

F32 = jnp.float32
BF16 = jnp.bfloat16
I32 = jnp.int32

EPS = 1e-6
GRID_W = 64
ROPE_THETA = 10000.0
HEAD_DIM = 64
ATT_HEADS = 8
ATT_KV_HEADS = 2
ATT_GROUP = ATT_HEADS // ATT_KV_HEADS
WINDOW = 128
SSM_HEADS = 8
SSM_HEAD_DIM = 64
SSM_INNER = SSM_HEADS * SSM_HEAD_DIM
SSM_GROUPS = 2
SSM_STATE = 64
SSM_BC_W = SSM_GROUPS * SSM_STATE
SSM_XBC_W = SSM_INNER + 2 * SSM_BC_W
SSM_CHUNK = 128
FNET_GROUPS = 8
FNET_GROUP_DIM = 64
FNET_WIDTH = FNET_GROUPS * FNET_GROUP_DIM
SCONV_WIDTH = 512
BRANCH_W = 512
N_BRANCHES = 4
N_EXPERTS = 32
N_EXPERT_GROUPS = 8
EXPERTS_PER_GROUP = N_EXPERTS // N_EXPERT_GROUPS
D_EXPERT = 512

LANES = 128
BF16_ROWS = 16
VMEM_LIMIT = 56 * 1024 * 1024
FFT_N2 = 128
MOE_TILE = 256
ROW_DMA_UNROLL = 8

Q_OFF, K_OFF, V_OFF, Z_OFF, XBC_OFF, DT_OFF, FU_OFF, SU_OFF = 0, 512, 640, 768, 1280, 2048, 2176, 2688
IN_W_PAD = SU_OFF + 3 * SCONV_WIDTH
DT_W = 2 * SSM_HEADS


def _dot(a, b):
    return jnp.dot(a.astype(BF16), b.astype(BF16), preferred_element_type=F32)


def _dot_nt(a, b):
    return lax.dot_general(a.astype(BF16), b.astype(BF16), (((1,), (1,)), ((), ())),
                           preferred_element_type=F32)


def _dot_split(a_bf16, x):
    hi = x.astype(BF16)
    rest = x - hi.astype(F32)
    mid = rest.astype(BF16)
    lo = (rest - mid.astype(F32)).astype(BF16)
    return sum(jnp.dot(a_bf16, t, preferred_element_type=F32) for t in (hi, mid, lo))


def _sigmoid(x):
    return 1.0 / (1.0 + jnp.exp(-x))


def _silu(x):
    return x * _sigmoid(x)


def _norm_mod(x, g, shift, scale):
    y = x * lax.rsqrt(jnp.mean(x * x, axis=-1, keepdims=True) + EPS) * g
    return y * (1.0 + scale) + shift


def _params(*sem):
    return pltpu.CompilerParams(dimension_semantics=sem, vmem_limit_bytes=VMEM_LIMIT)


def _const_spec(shape):
    n = len(shape)
    return pl.BlockSpec(shape, lambda *_: (0,) * n, pipeline_mode=pl.Buffered(1))


def _ada_kernel(c_ref, w_ref, b_ref, o_ref):
    o_ref[0] = _dot(_silu(c_ref[...]), w_ref[0]) + b_ref[0]


def _ada(c8, w_ada, b_ada):
    nl, d, w = w_ada.shape
    tn = 512
    return pl.pallas_call(
        _ada_kernel, grid=(nl, w // tn),
        in_specs=[pl.BlockSpec((8, d), lambda l, j: (0, 0)),
                  pl.BlockSpec((1, d, tn), lambda l, j: (l, 0, j)),
                  pl.BlockSpec((1, 1, tn), lambda l, j: (l, 0, j))],
        out_specs=pl.BlockSpec((1, 8, tn), lambda l, j: (l, 0, j)),
        out_shape=jax.ShapeDtypeStruct((nl, 8, w), F32),
        compiler_params=_params("arbitrary", "arbitrary"), name="ada",
    )(c8, w_ada, b_ada.reshape(nl, 1, w))


def _inproj_kernel(x_ref, mod_ref, g_ref, w_ref, cos_ref, sin_ref,
                   q_ref, k_ref, v_ref, z_ref, xbc_ref, dt_ref, fu_ref, su_ref):
    m = mod_ref[0]
    hb = _norm_mod(x_ref[...], g_ref[...], m[0:1], m[1:2]).astype(BF16)
    cos = cos_ref[...]
    sin = sin_ref[...]
    lane = lax.broadcasted_iota(I32, cos.shape, 1)
    first_half = (lane % HEAD_DIM) < HEAD_DIM // 2

    def proj(start, width):
        return jnp.dot(hb, w_ref[:, start:start + width], preferred_element_type=F32)

    def rope(r):
        rot = jnp.where(first_half, pltpu.roll(r, LANES - HEAD_DIM // 2, 1), pltpu.roll(r, HEAD_DIM // 2, 1))
        return r * cos + rot * sin

    scale = HEAD_DIM ** -0.5
    for j in range(ATT_HEADS // 2):
        r = rope(proj(Q_OFF + j * LANES, LANES)) * scale
        q_ref[0, 2 * j] = r[:, :HEAD_DIM].astype(BF16)
        q_ref[0, 2 * j + 1] = r[:, HEAD_DIM:].astype(BF16)
    r = rope(proj(K_OFF, LANES))
    k_ref[0, 0] = r[:, :HEAD_DIM].astype(BF16)
    k_ref[0, 1] = r[:, HEAD_DIM:].astype(BF16)
    r = proj(V_OFF, LANES)
    v_ref[0, 0] = r[:, :HEAD_DIM].astype(BF16)
    v_ref[0, 1] = r[:, HEAD_DIM:].astype(BF16)
    z_ref[0] = proj(Z_OFF, SSM_INNER).astype(BF16)
    xbc_ref[0] = proj(XBC_OFF, SSM_XBC_W).astype(BF16)
    dt_ref[0] = proj(DT_OFF, LANES)
    fu_ref[0] = proj(FU_OFF, FNET_WIDTH).astype(BF16)
    su_ref[0] = proj(SU_OFF, 3 * SCONV_WIDTH).astype(BF16)


def _inproj(xflat, nb, seq, t, mod, mod_row, g, w, cos, sin):
    d = xflat.shape[1]
    nt = seq // t
    if mod_row is None:
        mod_map = lambda b, i: (b, 0, 0)
    else:
        mod_map = lambda b, i: (mod_row, 0, 0)
    sds = jax.ShapeDtypeStruct
    outs = (sds((nb, ATT_HEADS, seq, HEAD_DIM), BF16), sds((nb, ATT_KV_HEADS, seq, HEAD_DIM), BF16),
            sds((nb, ATT_KV_HEADS, seq, HEAD_DIM), BF16), sds((nb, seq, SSM_INNER), BF16),
            sds((nb, seq, SSM_XBC_W), BF16), sds((nb, seq, LANES), F32),
            sds((nb, seq, FNET_WIDTH), BF16), sds((nb, seq, 3 * SCONV_WIDTH), BF16))
    head_spec = lambda nh: pl.BlockSpec((1, nh, t, HEAD_DIM), lambda b, i: (b, 0, i, 0))
    row_spec = lambda wd: pl.BlockSpec((1, t, wd), lambda b, i: (b, i, 0))
    return pl.pallas_call(
        _inproj_kernel, grid=(nb, nt),
        in_specs=[pl.BlockSpec((t, d), lambda b, i: (b * nt + i, 0)),
                  pl.BlockSpec((1, 8, d), mod_map),
                  _const_spec((1, d)), _const_spec((d, IN_W_PAD)),
                  pl.BlockSpec((t, LANES), lambda b, i: (i, 0)),
                  pl.BlockSpec((t, LANES), lambda b, i: (i, 0))],
        out_specs=(head_spec(ATT_HEADS), head_spec(ATT_KV_HEADS), head_spec(ATT_KV_HEADS),
                   row_spec(SSM_INNER), row_spec(SSM_XBC_W), row_spec(LANES),
                   row_spec(FNET_WIDTH), row_spec(3 * SCONV_WIDTH)),
        out_shape=outs, compiler_params=_params("arbitrary", "arbitrary"), name="inproj",
    )(xflat, mod, g, w, cos, sin)


def _attn_kernel(sink_ref, q_ref, *refs, tq, seq, band):
    if band:
        bias_ref, kp_ref, kc_ref, kn_ref, vp_ref, vc_ref, vn_ref, kx_ref, vx_ref, o_ref = refs
    else:
        kx_ref, vx_ref, o_ref = refs
    kh = pl.program_id(1)
    q = q_ref[0].reshape(ATT_GROUP * tq, HEAD_DIM)
    s_ctx = _dot_nt(q, kx_ref[0, 0])
    sink_col = jnp.concatenate(
        [jnp.full((tq, 1), sink_ref[kh * ATT_GROUP + g], F32) for g in range(ATT_GROUP)], axis=0)
    m = jnp.maximum(jnp.max(s_ctx, axis=1, keepdims=True), sink_col)
    if band:
        kb = jnp.concatenate([kp_ref[0, 0], kc_ref[0, 0], kn_ref[0, 0]], axis=0)
        vb = jnp.concatenate([vp_ref[0, 0], vc_ref[0, 0], vn_ref[0, 0]], axis=0)
        s_loc = _dot_nt(q, kb) + jnp.tile(bias_ref[0], (ATT_GROUP, 1))
        m = jnp.maximum(m, jnp.max(s_loc, axis=1, keepdims=True))
    p_ctx = jnp.exp(s_ctx - m)
    den = jnp.sum(p_ctx, axis=1, keepdims=True) + jnp.exp(sink_col - m)
    acc = _dot(p_ctx, vx_ref[0, 0])
    if band:
        p_loc = jnp.exp(s_loc - m)
        den = den + jnp.sum(p_loc, axis=1, keepdims=True)
        acc = acc + _dot(p_loc, vb)
    out = acc / den
    o_ref[0] = jnp.concatenate([out[g * tq:(g + 1) * tq] for g in range(ATT_GROUP)], axis=1).astype(BF16)


def _attention(sink, q, k, v, kx, vx, tq, band):
    nb, _, seq, _ = q.shape
    lc = kx.shape[2]
    nt = seq // tq
    r = tq // WINDOW
    nblk = seq // WINDOW
    qs = pl.BlockSpec((1, ATT_GROUP, tq, HEAD_DIM), lambda b, h, i: (b, h, i, 0))
    cur = pl.BlockSpec((1, 1, tq, HEAD_DIM), lambda b, h, i: (b, h, i, 0))
    prev = pl.BlockSpec((1, 1, WINDOW, HEAD_DIM), lambda b, h, i: (b, h, jnp.maximum(i * r - 1, 0), 0))
    nxt = pl.BlockSpec((1, 1, WINDOW, HEAD_DIM), lambda b, h, i: (b, h, jnp.minimum((i + 1) * r, nblk - 1), 0))
    ctx = pl.BlockSpec((1, 1, lc, HEAD_DIM), lambda b, h, i: (b, h, 0, 0))
    smem = pl.BlockSpec(memory_space=pltpu.SMEM)
    if band:
        nk = tq + 2 * WINDOW
        rel = np.arange(nk)[None, :] - WINDOW - np.arange(tq)[:, None]
        inside = np.abs(rel) <= WINDOW
        col = np.arange(nk)[None, :]
        kinds = [inside & ((col >= WINDOW) | (not first)) & ((col < tq + WINDOW) | (not lastt))
                 for lastt in (False, True) for first in (False, True)]
        bias = jnp.asarray(np.where(np.stack(kinds), 0.0, -np.inf), F32)
        bias_spec = pl.BlockSpec((1, tq, nk), lambda b, h, i: ((i == 0).astype(I32) + 2 * (i == nt - 1).astype(I32), 0, 0))
        in_specs = [smem, qs, bias_spec, prev, cur, nxt, prev, cur, nxt, ctx, ctx]
        args = (sink, q, bias, k, k, k, v, v, v, kx, vx)
    else:
        in_specs = [smem, qs, ctx, ctx]
        args = (sink, q, kx, vx)
    return pl.pallas_call(
        functools.partial(_attn_kernel, tq=tq, seq=seq, band=band),
        grid=(nb, ATT_KV_HEADS, nt), in_specs=in_specs,
        out_specs=pl.BlockSpec((1, tq, ATT_GROUP * HEAD_DIM), lambda b, h, i: (b, i, h)),
        out_shape=jax.ShapeDtypeStruct((nb, seq, ATT_HEADS * HEAD_DIM), BF16),
        compiler_params=_params("arbitrary", "arbitrary", "arbitrary"), name="attn_band" if band else "attn_ctx",
    )(*args)


def _ssd_kernel(xp_ref, xc_ref, xn_ref, dt_ref, cw_ref, cb_ref, dtb_ref, alog_ref, init_ref,
                *refs, reverse, with_y, finalize, nc):
    if finalize:
        yf_ref, z_ref, dsk_ref, nw_ref, y_ref, st_ref = refs
    elif with_y:
        y_ref, st_ref = refs
    else:
        (st_ref,) = refs
    q = SSM_CHUNK
    c = pl.program_id(1)
    ce = (nc - 1 - c) if reverse else c

    @pl.when(c == 0)
    def _():
        st_ref[...] = init_ref[...]

    xc = xc_ref[0].astype(F32)
    prow = jnp.where(ce == 0, 0.0, xp_ref[0].astype(F32)[BF16_ROWS - 1:BF16_ROWS])
    nrow = jnp.where(ce == nc - 1, 0.0, xn_ref[0].astype(F32)[0:1])
    rid = lax.broadcasted_iota(I32, xc.shape, 0)
    up = jnp.where(rid == 0, prow, pltpu.roll(xc, 1, 0))
    dn = jnp.where(rid == q - 1, nrow, pltpu.roll(xc, q - 1, 0))
    cw = cw_ref[...]
    act = _silu(cw[0:1] * up + cw[1:2] * xc + cw[2:3] * dn + cb_ref[...])
    xs = act[:, :SSM_INNER]
    bmat = act[:, SSM_INNER:SSM_INNER + SSM_BC_W]
    cmat = act[:, SSM_INNER + SSM_BC_W:]

    pre = dt_ref[0] + dtb_ref[...]
    dt = jnp.maximum(pre, 0.0) + jnp.log1p(jnp.exp(-jnp.abs(pre)))
    dta = dt * (-jnp.exp(alog_ref[...]))
    ri = lax.broadcasted_iota(I32, (q, q), 0)
    ci = lax.broadcasted_iota(I32, (q, q), 1)
    tri = (ri <= ci) if reverse else (ri >= ci)
    acs = _dot_split(tri.astype(BF16), dta)
    col0 = SSM_HEADS if reverse else 0
    lo_half = lax.broadcasted_iota(I32, (q, LANES), 1) < SSM_HEAD_DIM

    def bcast(mat, h):
        return jnp.broadcast_to(mat[:, col0 + h:col0 + h + 1], (q, LANES))

    def lane_expand(cols):
        return jnp.concatenate([jnp.where(lo_half, cols[2 * j], cols[2 * j + 1]) for j in range(SSM_HEADS // 2)],
                               axis=1)

    acs_b = [bcast(acs, h) for h in range(SSM_HEADS)]
    acs_e = lane_expand(acs_b)
    xd = xs * lane_expand([bcast(dt, h) for h in range(SSM_HEADS)])
    tot_e = acs_e[0:1] if reverse else acs_e[q - 1:q]
    hpg = SSM_HEADS // SSM_GROUPS
    gw = hpg * SSM_HEAD_DIM

    def grp(mat, g):
        return mat[:, g * SSM_STATE:(g + 1) * SSM_STATE]

    if with_y:
        acs_t = acs.T
        cb = [_dot_nt(grp(cmat, g), grp(bmat, g)) for g in range(SSM_GROUPS)]
        y_diag = []
        for j in range(SSM_HEADS // 2):
            sc = []
            for h in (2 * j, 2 * j + 1):
                seg = acs_b[h] - acs_t[col0 + h:col0 + h + 1, :]
                sc.append((cb[h // hpg] * jnp.exp(jnp.where(tri, seg, -jnp.inf))).astype(BF16))
            slab = xd[:, j * LANES:(j + 1) * LANES].astype(BF16)
            rhs = jnp.concatenate([jnp.where(lo_half, slab, 0), jnp.where(lo_half, 0, slab)], axis=0)
            y_diag.append(jnp.dot(jnp.concatenate(sc, axis=1), rhs, preferred_element_type=F32))
        y_off = [_dot(grp(cmat, g), st_ref[0, g]) for g in range(SSM_GROUPS)]
        y = jnp.concatenate(y_diag, axis=1) + jnp.concatenate(y_off, axis=1) * jnp.exp(acs_e)

    xdd = xd * jnp.exp(tot_e - acs_e)
    b_t = bmat.T
    for g in range(SSM_GROUPS):
        upd = _dot(b_t[g * SSM_STATE:(g + 1) * SSM_STATE, :], xdd[:, g * gw:(g + 1) * gw])
        st_ref[0, g] = st_ref[0, g] * jnp.exp(tot_e[:, g * gw:(g + 1) * gw]) + upd

    if finalize:
        yt = yf_ref[0] + y + dsk_ref[...] * xs
        yt = yt * _silu(z_ref[0].astype(F32))
        yt = yt * lax.rsqrt(jnp.mean(yt * yt, axis=-1, keepdims=True) + EPS) * nw_ref[...]
        y_ref[0] = yt.astype(BF16)
    elif with_y:
        y_ref[0] = y


def _ssd(xbc, dtr, cw, cb, dtb, alog, init, reverse, with_y, fin=None):
    nb, seq, _ = xbc.shape
    st_block = (1, SSM_GROUPS, SSM_STATE, SSM_INNER // SSM_GROUPS)
    nc = seq // SSM_CHUNK
    per = SSM_CHUNK // BF16_ROWS
    nhalo = seq // BF16_ROWS
    ce = (lambda c: nc - 1 - c) if reverse else (lambda c: c)
    xw = SSM_XBC_W
    in_specs = [pl.BlockSpec((1, BF16_ROWS, xw), lambda b, c: (b, jnp.maximum(ce(c) * per - 1, 0), 0)),
                pl.BlockSpec((1, SSM_CHUNK, xw), lambda b, c: (b, ce(c), 0)),
                pl.BlockSpec((1, BF16_ROWS, xw), lambda b, c: (b, jnp.minimum((ce(c) + 1) * per, nhalo - 1), 0)),
                pl.BlockSpec((1, SSM_CHUNK, LANES), lambda b, c: (b, ce(c), 0)),
                _const_spec((8, xw)), _const_spec((1, xw)), _const_spec((1, LANES)), _const_spec((1, LANES)),
                pl.BlockSpec(st_block, lambda b, c: (b, 0, 0, 0))]
    args = [xbc, xbc, xbc, dtr, cw, cb, dtb, alog, init]
    st_spec = pl.BlockSpec(st_block, lambda b, c: (b, 0, 0, 0))
    st_shape = jax.ShapeDtypeStruct((nb,) + st_block[1:], F32)
    y_spec = pl.BlockSpec((1, SSM_CHUNK, SSM_INNER), lambda b, c: (b, ce(c), 0))
    finalize = fin is not None
    if finalize:
        yf, z, dsk, nw = fin
        in_specs += [y_spec, y_spec, _const_spec((1, SSM_INNER)), _const_spec((1, SSM_INNER))]
        args += [yf, z, dsk, nw]
    if with_y:
        out_specs = (y_spec, st_spec)
        out_shape = (jax.ShapeDtypeStruct((nb, seq, SSM_INNER), BF16 if finalize else F32), st_shape)
    else:
        out_specs = (st_spec,)
        out_shape = (st_shape,)
    res = pl.pallas_call(
        functools.partial(_ssd_kernel, reverse=reverse, with_y=with_y, finalize=finalize, nc=nc),
        grid=(nb, nc), in_specs=in_specs, out_specs=out_specs, out_shape=out_shape,
        compiler_params=_params("arbitrary", "arbitrary"), name="ssd_bwd" if reverse else "ssd_fwd",
    )(*args)
    return res if with_y else (None, res[0])


def _fft_a_kernel(u_ref, wc_ref, m1_ref, y_ref, wc_bf):
    @pl.when((pl.program_id(0) == 0) & (pl.program_id(1) == 0))
    def _():
        wc_bf[...] = wc_ref[...].astype(BF16)

    v = _dot(u_ref[0], wc_bf[...])
    vs = jnp.concatenate([v[:, :FNET_WIDTH], v[:, FNET_WIDTH:]], axis=0)
    b = _dot(m1_ref[0], vs)
    y_ref[0, 0, 0] = b[:FFT_N2].astype(BF16)
    y_ref[0, 1, 0] = b[FFT_N2:].astype(BF16)


def _fft_c_kernel(y_ref, m2_ref, o_ref):
    o_ref[0] = _dot(m2_ref[...], y_ref[0]).astype(BF16)


def _fft_small_kernel(u_ref, wc_ref, m_ref, o_ref):
    v = _dot(u_ref[0], wc_ref[...])
    vs = jnp.concatenate([v[:, :FNET_WIDTH], v[:, FNET_WIDTH:]], axis=0)
    o_ref[0] = _dot(m_ref[...], vs).astype(BF16)


def _channel_dft():
    idx = np.arange(FNET_GROUP_DIM)
    ang = 2.0 * np.pi * np.outer(idx, idx) / FNET_GROUP_DIM
    eye = np.eye(FNET_GROUPS)
    return jnp.asarray(np.concatenate([np.kron(eye, np.cos(ang)), -np.kron(eye, np.sin(ang))], axis=1), F32)


def _fourier_long(u):
    nb, seq, w = u.shape
    n2 = FFT_N2
    n1 = seq // n2
    t2 = np.arange(n2)
    k2 = np.arange(n2)
    t1 = np.arange(n1)
    ang = 2.0 * np.pi * (np.outer(k2, t2)[None] / n2 + (t1[:, None, None] * k2[None, :, None]) / seq)
    co, si = np.cos(ang), np.sin(ang)
    m1 = jnp.asarray(np.concatenate([np.concatenate([co, si], axis=2),
                                     np.concatenate([-si, co], axis=2)], axis=1), F32)
    ang1 = 2.0 * np.pi * np.outer(t1, t1) / n1
    norm = 1.0 / math.sqrt(seq * FNET_GROUP_DIM)
    m2 = jnp.asarray(np.concatenate([np.cos(ang1), np.sin(ang1)], axis=1) * norm, F32)
    y = pl.pallas_call(
        _fft_a_kernel, grid=(nb, n1),
        in_specs=[pl.BlockSpec((1, n2, w), lambda b, j: (b, 0, j)),
                  _const_spec((w, 2 * w)),
                  pl.BlockSpec((1, 2 * n2, 2 * n2), lambda b, j: (j, 0, 0))],
        out_specs=pl.BlockSpec((1, 2, 1, n2, w), lambda b, j: (b, 0, j, 0, 0)),
        out_shape=jax.ShapeDtypeStruct((nb, 2, n1, n2, w), BF16),
        scratch_shapes=[pltpu.VMEM((w, 2 * w), BF16)],
        compiler_params=_params("arbitrary", "arbitrary"), name="fft_a",
    )(u.reshape(nb, n2, n1 * w), _channel_dft(), m1)
    nl = 2048
    out = pl.pallas_call(
        _fft_c_kernel, grid=(nb, n2 * w // nl),
        in_specs=[pl.BlockSpec((1, 2 * n1, nl), lambda b, j: (b, 0, j)),
                  _const_spec((n1, 2 * n1))],
        out_specs=pl.BlockSpec((1, n1, nl), lambda b, j: (b, 0, j)),
        out_shape=jax.ShapeDtypeStruct((nb, n1, n2 * w), BF16),
        compiler_params=_params("arbitrary", "arbitrary"), name="fft_c",
    )(y.reshape(nb, 2 * n1, n2 * w), m2)
    return out.reshape(nb, seq, w)


def _fourier_short(u):
    nb, seq, w = u.shape
    t = np.arange(seq)
    ang = 2.0 * np.pi * np.outer(t, t) / seq
    norm = 1.0 / math.sqrt(seq * FNET_GROUP_DIM)
    m = jnp.asarray(np.concatenate([np.cos(ang), np.sin(ang)], axis=1) * norm, F32)
    return pl.pallas_call(
        _fft_small_kernel, grid=(nb,),
        in_specs=[pl.BlockSpec((1, seq, w), lambda b: (b, 0, 0)),
                  _const_spec((w, 2 * w)), _const_spec((seq, 2 * seq))],
        out_specs=pl.BlockSpec((1, seq, w), lambda b: (b, 0, 0)),
        out_shape=jax.ShapeDtypeStruct((nb, seq, w), BF16),
        compiler_params=_params("arbitrary"), name="fft_small",
    )(u, _channel_dft(), m)


def _merge_kernel(x_ref, mod_ref, g1_ref, g2_ref, att_ref, ssm_ref, fft_ref, sup_ref, su_ref, sun_ref,
                  sw_ref, wg_ref, bg_ref, wb_ref, wo_ref, x1_ref, h2_ref, *, nt):
    d = x_ref.shape[1]
    t = x_ref.shape[0]
    i = pl.program_id(1)
    x = x_ref[...]
    m = mod_ref[0]
    hb = _norm_mod(x, g1_ref[...], m[0:1], m[1:2]).astype(BF16)

    def gated(su):
        return su[:, SCONV_WIDTH:2 * SCONV_WIDTH] * su[:, 2 * SCONV_WIDTH:]

    su = su_ref[0].astype(F32)
    p = gated(su)
    prow = jnp.where(i == 0, 0.0, gated(sup_ref[0].astype(F32)[BF16_ROWS - 1:BF16_ROWS]))
    nrow = jnp.where(i == nt - 1, 0.0, gated(sun_ref[0].astype(F32)[0:1]))
    rid = lax.broadcasted_iota(I32, p.shape, 0)
    up = jnp.where(rid == 0, prow, pltpu.roll(p, 1, 0))
    dn = jnp.where(rid == t - 1, nrow, pltpu.roll(p, t - 1, 0))
    sw = sw_ref[...]
    sconv = su[:, :SCONV_WIDTH] * (sw[0:1] * up + sw[1:2] * p + sw[2:3] * dn)

    branches = (att_ref[0], ssm_ref[0], fft_ref[0], sconv.astype(BF16))
    acc = jnp.zeros((t, d), F32)
    for n in range(N_BRANCHES):
        gate = _sigmoid(jnp.dot(hb, wg_ref[:, n * d:(n + 1) * d], preferred_element_type=F32)
                        + bg_ref[:, n * d:(n + 1) * d])
        acc = acc + gate * jnp.dot(branches[n], wb_ref[n], preferred_element_type=F32)
    y = jnp.dot(acc.astype(BF16), wo_ref[...], preferred_element_type=F32)
    x1 = x + m[2:3] * y
    x1_ref[...] = x1
    h2_ref[...] = _norm_mod(x1, g2_ref[...], m[3:4], m[4:5])


def _merge(xflat, nb, seq, t, mod, mod_row, g1, g2, att, ssm, fft, su, sw, wg, bg, wb, wo):
    ntok, d = xflat.shape
    nt = seq // t
    per = t // BF16_ROWS
    nhalo = seq // BF16_ROWS
    mod_map = (lambda b, i: (b, 0, 0)) if mod_row is None else (lambda b, i: (mod_row, 0, 0))
    row = pl.BlockSpec((t, d), lambda b, i: (b * nt + i, 0))
    br = pl.BlockSpec((1, t, BRANCH_W), lambda b, i: (b, i, 0))
    suw = 3 * SCONV_WIDTH
    in_specs = [row, pl.BlockSpec((1, 8, d), mod_map), _const_spec((1, d)), _const_spec((1, d)),
                br, br, br,
                pl.BlockSpec((1, BF16_ROWS, suw), lambda b, i: (b, jnp.maximum(i * per - 1, 0), 0)),
                pl.BlockSpec((1, t, suw), lambda b, i: (b, i, 0)),
                pl.BlockSpec((1, BF16_ROWS, suw), lambda b, i: (b, jnp.minimum((i + 1) * per, nhalo - 1), 0)),
                _const_spec((8, SCONV_WIDTH)), _const_spec((d, N_BRANCHES * d)), _const_spec((1, N_BRANCHES * d)),
                _const_spec((N_BRANCHES, BRANCH_W, d)), _const_spec((d, d))]
    return pl.pallas_call(
        functools.partial(_merge_kernel, nt=nt),
        grid=(nb, nt), in_specs=in_specs, out_specs=(row, row),
        out_shape=(jax.ShapeDtypeStruct((ntok, d), F32), jax.ShapeDtypeStruct((ntok, d), F32)),
        compiler_params=_params("arbitrary", "arbitrary"), name="merge",
    )(xflat, mod, g1, g2, att, ssm, fft, su, su, su, sw, wg, bg, wb, wo)


def _route_kernel(h_ref, wr_ref, rb_ref, cin_ref, ri_ref, rw_ref, cnt_ref):
    t = h_ref.shape[0]
    step = pl.program_id(0)

    @pl.when(step == 0)
    def _():
        cnt_ref[...] = cin_ref[...]

    sc = _sigmoid(_dot(h_ref[...], wr_ref[...]))
    sel = sc + rb_ref[...]
    s = [sel[:, j * LANES:(j + 1) * LANES] for j in range(EXPERTS_PER_GROUP)]
    u = [sc[:, j * LANES:(j + 1) * LANES] for j in range(EXPERTS_PER_GROUP)]
    gs = None
    for a in range(EXPERTS_PER_GROUP):
        for b in range(a + 1, EXPERTS_PER_GROUP):
            pair = s[a] + s[b]
            gs = pair if gs is None else jnp.maximum(gs, pair)
    lane = lax.broadcasted_iota(I32, (t, LANES), 1)
    gmax = jnp.max(gs, axis=1, keepdims=True)
    best = jnp.min(jnp.where(gs == gmax, lane, LANES), axis=1, keepdims=True)
    on = lane == best
    v = [jnp.sum(jnp.where(on, s[j], 0.0), axis=1, keepdims=True) for j in range(EXPERTS_PER_GROUP)]
    w = [jnp.sum(jnp.where(on, u[j], 0.0), axis=1, keepdims=True) for j in range(EXPERTS_PER_GROUP)]

    def first_argmax(vals):
        mx = vals[0]
        for x in vals[1:]:
            mx = jnp.maximum(mx, x)
        idx = jnp.full_like(best, len(vals) - 1)
        for j in range(len(vals) - 2, -1, -1):
            idx = jnp.where(vals[j] == mx, j, idx)
        return idx

    def pick(vals, idx):
        out = vals[-1]
        for j in range(len(vals) - 2, -1, -1):
            out = jnp.where(idx == j, vals[j], out)
        return out

    l1 = first_argmax(v)
    l2 = first_argmax([jnp.where(l1 == j, -jnp.inf, v[j]) for j in range(EXPERTS_PER_GROUP)])
    w1, w2 = pick(w, l1), pick(w, l2)
    tot = w1 + w2
    e1 = best * EXPERTS_PER_GROUP + l1
    e2 = best * EXPERTS_PER_GROUP + l2

    oh1 = (lane == e1).astype(F32)
    oh2 = (lane == e2).astype(F32)
    oh = oh1 + oh2
    ri_ = lax.broadcasted_iota(I32, (t, t), 0)
    ci_ = lax.broadcasted_iota(I32, (t, t), 1)
    before = _dot((ri_ > ci_).astype(F32), oh) + cnt_ref[0:1]
    r1 = jnp.sum(oh1 * before, axis=1, keepdims=True).astype(I32)
    r2 = jnp.sum(oh2 * before, axis=1, keepdims=True).astype(I32)
    cnt_ref[...] = cnt_ref[...] + jnp.sum(oh, axis=0, keepdims=True)

    ri_ref[...] = jnp.where(lane == 0, e1, jnp.where(lane == 1, e2, jnp.where(lane == 2, r1, jnp.where(lane == 3, r2, 0))))
    rw_ref[...] = jnp.where(lane == 0, w1 / tot, jnp.where(lane == 1, w2 / tot, 0.0))


def _route(h2, wr, rb, cnt_in):
    ntok, d = h2.shape
    t = 512 if ntok % 512 == 0 else 256
    return pl.pallas_call(
        _route_kernel, grid=(ntok // t,),
        in_specs=[pl.BlockSpec((t, d), lambda i: (i, 0)),
                  _const_spec((d, EXPERTS_PER_GROUP * LANES)), _const_spec((1, EXPERTS_PER_GROUP * LANES)),
                  _const_spec((8, LANES))],
        out_specs=(pl.BlockSpec((t, LANES), lambda i: (i, 0)), pl.BlockSpec((t, LANES), lambda i: (i, 0)),
                   pl.BlockSpec((8, LANES), lambda i: (0, 0))),
        out_shape=(jax.ShapeDtypeStruct((ntok, LANES), I32), jax.ShapeDtypeStruct((ntok, LANES), F32),
                   jax.ShapeDtypeStruct((8, LANES), F32)),
        compiler_params=_params("arbitrary"), name="route",
    )(h2, wr, rb, cnt_in)


def _dispatch_kernel(dest_ref, zt_ref, *refs, td, starts):
    n_streams = len(starts) - 1
    h_refs = refs[:n_streams]
    xs_hbm, zbuf, zsem, sem = refs[n_streams:]
    i = pl.program_id(0)

    @pl.when(i == 0)
    def _():
        zbuf[...] = jnp.zeros_like(zbuf)

        def zero_copy(j):
            start = pl.multiple_of(zt_ref[j] * MOE_TILE, MOE_TILE)
            return pltpu.make_async_copy(zbuf, xs_hbm.at[pl.ds(start, MOE_TILE)], zsem)

        def zissue(j, carry):
            @pl.when(zt_ref[j] >= 0)
            def _():
                zero_copy(j).start()
            return carry

        def zdrain(j, carry):
            @pl.when(zt_ref[j] >= 0)
            def _():
                zero_copy(j).wait()
            return carry

        lax.fori_loop(0, zt_ref.shape[0], zissue, 0)
        lax.fori_loop(0, zt_ref.shape[0], zdrain, 0)

    for s in range(n_streams):
        h_ref = h_refs[s]

        def row_copy(src_row, dst_row, h_ref=h_ref):
            return pltpu.make_async_copy(h_ref.at[pl.ds(src_row, 1)], xs_hbm.at[pl.ds(dst_row, 1)], sem)

        @pl.when((i >= starts[s]) & (i < starts[s + 1]))
        def _(row_copy=row_copy):
            def issue(j, carry):
                row_copy(j, dest_ref[0, 0, 2 * j]).start()
                row_copy(j, dest_ref[0, 0, 2 * j + 1]).start()
                return carry

            def drain(j, carry):
                row_copy(0, 0).wait()
                return carry

            lax.fori_loop(0, td, issue, 0, unroll=ROW_DMA_UNROLL)
            lax.fori_loop(0, 2 * td, drain, 0, unroll=ROW_DMA_UNROLL)


def _dispatch(h2s, dest, ztiles, nrows):
    d = h2s[0].shape[1]
    td = 256
    starts = [0]
    for h in h2s:
        starts.append(starts[-1] + h.shape[0] // td)
    nt = starts[-1]

    def stream_spec(s):
        return pl.BlockSpec((td, d), lambda i: (jnp.clip(i - starts[s], 0, starts[s + 1] - starts[s] - 1), 0))

    return pl.pallas_call(
        functools.partial(_dispatch_kernel, td=td, starts=tuple(starts)), grid=(nt,),
        in_specs=[pl.BlockSpec((1, 1, 2 * td), lambda i: (i, 0, 0), memory_space=pltpu.SMEM),
                  pl.BlockSpec(memory_space=pltpu.SMEM)] + [stream_spec(s) for s in range(len(h2s))],
        out_specs=pl.BlockSpec(memory_space=pl.ANY),
        out_shape=jax.ShapeDtypeStruct((nrows, d), F32),
        scratch_shapes=[pltpu.VMEM((MOE_TILE, d), F32), pltpu.SemaphoreType.DMA(()), pltpu.SemaphoreType.DMA(())],
        compiler_params=_params("arbitrary"), name="dispatch",
    )(dest.reshape(nt, 1, 2 * td), ztiles, *h2s)


def _ffn_kernel(be_ref, nu_ref, x_ref, w1_ref, w3_ref, w2_ref, y_ref, w1_bf, w3_bf, w2_bf):
    i = pl.program_id(0)
    used = i < nu_ref[0]

    @pl.when(used & ((i == 0) | (be_ref[i] != be_ref[jnp.maximum(i - 1, 0)])))
    def _():
        w1_bf[...] = w1_ref[0, 0].astype(BF16)
        w3_bf[...] = w3_ref[0, 0].astype(BF16)
        w2_bf[...] = w2_ref[0, 0].astype(BF16)

    @pl.when(used)
    def _():
        xb = x_ref[...].astype(BF16)
        a = jnp.dot(xb, w1_bf[...], preferred_element_type=F32)
        b = jnp.dot(xb, w3_bf[...], preferred_element_type=F32)
        y_ref[...] = jnp.dot((_silu(a) * b).astype(BF16), w2_bf[...], preferred_element_type=F32)

    @pl.when(i >= nu_ref[0])
    def _():
        y_ref[...] = jnp.zeros_like(y_ref)


def _ffn(xs, block_expert, n_used, w1, w3, w2, layer):
    nrows, d = xs.shape
    de = w1.shape[3]
    tm = MOE_TILE
    grid_spec = pltpu.PrefetchScalarGridSpec(
        num_scalar_prefetch=2, grid=(nrows // tm,),
        in_specs=[pl.BlockSpec((tm, d), lambda i, be, nu: (jnp.minimum(i, nu[0] - 1), 0)),
                  pl.BlockSpec((1, 1, d, de), lambda i, be, nu: (layer, be[i], 0, 0)),
                  pl.BlockSpec((1, 1, d, de), lambda i, be, nu: (layer, be[i], 0, 0)),
                  pl.BlockSpec((1, 1, de, d), lambda i, be, nu: (layer, be[i], 0, 0))],
        out_specs=pl.BlockSpec((tm, d), lambda i, be, nu: (i, 0)),
        scratch_shapes=[pltpu.VMEM((d, de), BF16), pltpu.VMEM((d, de), BF16), pltpu.VMEM((de, d), BF16)])
    return pl.pallas_call(
        _ffn_kernel, grid_spec=grid_spec, out_shape=jax.ShapeDtypeStruct((nrows, d), F32),
        compiler_params=_params("arbitrary"), name="ffn",
    )(block_expert, n_used, xs, w1, w3, w2)


def _combine_kernel(dest_ref, x_ref, mod_ref, rw_ref, fw_ref, y_hbm, o_ref, buf, sem, *, tc, final):
    def row_copy(src_row, k, j):
        return pltpu.make_async_copy(y_hbm.at[pl.ds(src_row, 1)], buf.at[k, pl.ds(j, 1)], sem)

    def issue(j, carry):
        row_copy(dest_ref[0, 0, 2 * j], 0, j).start()
        row_copy(dest_ref[0, 0, 2 * j + 1], 1, j).start()
        return carry

    def drain(j, carry):
        row_copy(0, 0, 0).wait()
        return carry

    lax.fori_loop(0, tc, issue, 0, unroll=ROW_DMA_UNROLL)
    lax.fori_loop(0, 2 * tc, drain, 0, unroll=ROW_DMA_UNROLL)
    rw = rw_ref[...]
    f = rw[:, 0:1] * buf[0] + rw[:, 1:2] * buf[1]
    x = x_ref[...] + mod_ref[0][5:6] * f
    if final:
        x = x * lax.rsqrt(jnp.mean(x * x, axis=-1, keepdims=True) + EPS) * fw_ref[...]
    o_ref[...] = x


def _combine(x1, dest, rw, y, mod, mod_row, seq, fw, final):
    ntok, d = x1.shape
    tc = 256
    nt = ntok // tc
    per_seq = seq // tc
    mod_map = (lambda i: (i // per_seq, 0, 0)) if mod_row is None else (lambda i: (mod_row, 0, 0))
    return pl.pallas_call(
        functools.partial(_combine_kernel, tc=tc, final=final), grid=(nt,),
        in_specs=[pl.BlockSpec((1, 1, 2 * tc), lambda i: (i, 0, 0), memory_space=pltpu.SMEM),
                  pl.BlockSpec((tc, d), lambda i: (i, 0)),
                  pl.BlockSpec((1, 8, d), mod_map),
                  pl.BlockSpec((tc, LANES), lambda i: (i, 0)),
                  _const_spec((1, d)),
                  pl.BlockSpec(memory_space=pl.ANY)],
        out_specs=pl.BlockSpec((tc, d), lambda i: (i, 0)),
        out_shape=jax.ShapeDtypeStruct((ntok, d), F32),
        scratch_shapes=[pltpu.VMEM((2, tc, d), F32), pltpu.SemaphoreType.DMA(())],
        compiler_params=_params("arbitrary"), name="combine",
    )(dest.reshape(nt, 1, 2 * tc), x1, mod, rw, fw, y)


def _moe(streams, mod, wr, rb, w1, w3, w2, layer, fw, final):
    d = streams[0][0].shape[1]
    cnt = jnp.zeros((8, LANES), F32)
    routed = []
    for _, h2, _, _ in streams:
        ri, rw, cnt = _route(h2, wr, rb, cnt)
        routed.append((ri, rw))
    counts = cnt[0, :N_EXPERTS].astype(I32)
    padded = (counts + MOE_TILE - 1) // MOE_TILE * MOE_TILE
    pad_end = jnp.cumsum(padded)
    offs = pad_end - padded
    ntok = sum(s[0].shape[0] for s in streams)
    n_tiles = -(-2 * ntok // MOE_TILE) + N_EXPERTS
    tile_start = jnp.arange(n_tiles, dtype=I32) * MOE_TILE
    block_expert = jnp.minimum(jnp.sum((pad_end[None, :] <= tile_start[:, None]).astype(I32), axis=1), N_EXPERTS - 1)
    n_used = (pad_end[-1:] // MOE_TILE).astype(I32)
    dests = [(offs[ri[:, 0:2]] + ri[:, 2:4]).reshape(-1) for ri, _ in routed]
    tail = n_used + jnp.arange(N_EXPERTS, dtype=I32)
    ztiles = jnp.concatenate([jnp.where(padded > 0, pad_end // MOE_TILE - 1, -1),
                              jnp.where(tail < n_tiles, tail, -1)]).astype(I32)
    xs = _dispatch([s[1] for s in streams], jnp.concatenate(dests), ztiles, n_tiles * MOE_TILE)
    y = _ffn(xs, block_expert, n_used, w1, w3, w2, layer)
    return [_combine(x1, dest, rw, y, mod, mod_row, seq, fw, final)
            for (x1, _, seq, mod_row), dest, (_, rw) in zip(streams, dests, routed)]


def _pad_rows(a, rows):
    return jnp.concatenate([a, jnp.zeros((rows - a.shape[0],) + a.shape[1:], a.dtype)], axis=0)


def _pad_cols(a, cols, value=0.0):
    return jnp.concatenate([a, jnp.full(a.shape[:-1] + (cols - a.shape[-1],), value, a.dtype)], axis=-1)


def _rope_tables(seq):
    rows = seq // GRID_W
    row = jnp.repeat(jnp.arange(rows, dtype=F32), GRID_W)
    col = jnp.tile(jnp.arange(GRID_W, dtype=F32), rows)
    pairs = HEAD_DIM // 4
    inv_freq = ROPE_THETA ** (-jnp.arange(pairs, dtype=F32) / pairs)
    ang = jnp.concatenate([row[:, None] * inv_freq, col[:, None] * inv_freq], axis=-1)
    cos, sin = jnp.cos(ang), jnp.sin(ang)
    cos_t = jnp.tile(cos, (1, LANES // (HEAD_DIM // 2)))
    sin_t = jnp.tile(jnp.concatenate([-sin, sin], axis=-1), (1, LANES // HEAD_DIM))
    return cos_t, sin_t


def kernel(x, c, ctx, c_ctx, norm1_w, norm2_w, w_ada, b_ada, w_in, attn_sink, ssm_conv_w, ssm_conv_b,
           ssm_dt_bias, ssm_a_log, ssm_d, ssm_norm_w, sconv_w, w_gate, b_gate, w_branch, w_o,
           w_router, router_bias, moe_w1, moe_w3, moe_w2, final_norm_w):
    nb, seq, d = x.shape
    lc = ctx.shape[1]
    depth = w_in.shape[0]
    n_lat = nb * seq
    n_ctx = nb * lc
    assert seq % 512 == 0 and seq % (FFT_N2 * 8) == 0 and lc % 256 == 0 and nb < 8

    xl = x.reshape(n_lat, d)
    xc = ctx.reshape(n_ctx, d)
    c8 = _pad_rows(jnp.concatenate([c, c_ctx[None]], axis=0), 8)
    mods = _ada(c8, w_ada, b_ada)
    mods = _pad_cols(mods.reshape(depth, 8, 6, d).swapaxes(-1, -2), 8).swapaxes(-1, -2)

    cos_l, sin_l = _rope_tables(seq)
    cos_c, sin_c = jnp.ones((lc, LANES), F32), jnp.zeros((lc, LANES), F32)

    wr = w_router.reshape(d, N_EXPERT_GROUPS, EXPERTS_PER_GROUP).transpose(0, 2, 1)
    wr = _pad_cols(wr, LANES).reshape(d, EXPERTS_PER_GROUP * LANES).astype(BF16)
    rb = _pad_cols(router_bias.reshape(N_EXPERT_GROUPS, EXPERTS_PER_GROUP).T, LANES, -1e30).reshape(1, -1)

    for layer in range(depth):
        last = layer + 1 == depth
        mod = mods[layer]
        wi = w_in[layer]
        w_pad = jnp.concatenate([wi[:, :2048], _pad_cols(wi[:, 2048:2048 + DT_W], LANES), wi[:, 2048 + DT_W:]],
                                axis=1).astype(BF16)
        g1 = norm1_w[layer][None]
        g2 = norm2_w[layer][None]
        cw = _pad_rows(ssm_conv_w[layer], 8)
        cb = ssm_conv_b[layer][None]
        dtb = _pad_cols(ssm_dt_bias[layer].reshape(1, DT_W), LANES)
        alog = _pad_cols(ssm_a_log[layer].reshape(1, DT_W), LANES)
        dsk = jnp.repeat(ssm_d[layer], SSM_HEAD_DIM)[None]
        nw = ssm_norm_w[layer][None]
        sw = _pad_rows(sconv_w[layer], 8)
        wg = w_gate[layer].astype(BF16)
        bg = b_gate[layer][None]
        wb = w_branch[layer].astype(BF16)
        wo = w_o[layer].astype(BF16)
        sink = attn_sink[layer]

        q, k, v, z, xbc, dtr, fu, su = _inproj(xl, nb, seq, 512, mod, None, g1, w_pad, cos_l, sin_l)
        qc, kc, vc, zc, xbcc, dtrc, fuc, suc = _inproj(xc, nb, lc, 256, mod, nb, g1, w_pad, cos_c, sin_c)

        att = _attention(sink, q, k, v, kc, vc, 256, True)

        zero_state = jnp.zeros((nb, SSM_GROUPS, SSM_STATE, SSM_INNER // SSM_GROUPS), F32)
        ssd_c = functools.partial(_ssd, xbcc, dtrc, cw, cb, dtb, alog)
        ssd_l = functools.partial(_ssd, xbc, dtr, cw, cb, dtb, alog)
        if last:
            _, fin_f = ssd_c(zero_state, False, False)
            _, fin_b = ssd_c(zero_state, True, False)
        else:
            yc_f, fin_f = ssd_c(zero_state, False, True)
            ssm_c, fin_b = ssd_c(zero_state, True, True, fin=(yc_f, zc, dsk, nw))
        y_f, _ = ssd_l(fin_f, False, True)
        ssm, _ = ssd_l(fin_b, True, True, fin=(y_f, z, dsk, nw))

        fft = _fourier_long(fu)

        x1, h2 = _merge(xl, nb, seq, 256, mod, None, g1, g2, att, ssm, fft, su, sw, wg, bg, wb, wo)
        streams = [(x1, h2, seq, None)]
        if not last:
            att_c = _attention(sink, qc, None, None, kc, vc, lc, False)
            fft_c = _fourier_short(fuc)
            x1c, h2c = _merge(xc, nb, lc, 256, mod, nb, g1, g2, att_c, ssm_c, fft_c, suc, sw, wg, bg, wb, wo)
            streams.append((x1c, h2c, lc, nb))
        new = _moe(streams, mod, wr, rb, moe_w1, moe_w3, moe_w2, layer, final_norm_w[None], last)
        xl = new[0]
        if not last:
            xc = new[1]
    return xl.reshape(nb, seq, d)
```

```python
import functools
import math

import numpy as np
import jax
import jax.numpy as jnp
from jax import lax
from jax.experimental import pallas as pl
from jax.experimental.pallas import tpu as pltpu

F32 = jnp.float32
BF16 = jnp.bfloat16
I32 = jnp.int32

EPS = 1e-6
GRID_W = 64
ROPE_THETA = 10000.0
HEAD_DIM = 64
ATT_HEADS = 8
ATT_KV_HEADS = 2
ATT_GROUP = ATT_HEADS // ATT_KV_HEADS
WINDOW = 128
SSM_HEADS = 8
SSM_HEAD_DIM = 64
SSM_INNER = SSM_HEADS * SSM_HEAD_DIM
SSM_GROUPS = 2
SSM_STATE = 64
SSM_BC_W = SSM_GROUPS * SSM_STATE
SSM_XBC_W = SSM_INNER + 2 * SSM_BC_W
SSM_CHUNK = 128
FNET_GROUPS = 8
FNET_GROUP_DIM = 64
FNET_WIDTH = FNET_GROUPS * FNET_GROUP_DIM
SCONV_WIDTH = 512
BRANCH_W = 512
N_BRANCHES = 4
N_EXPERTS = 32
N_EXPERT_GROUPS = 8
EXPERTS_PER_GROUP = N_EXPERTS // N_EXPERT_GROUPS
D_EXPERT = 512

LANES = 128
BF16_ROWS = 16
VMEM_LIMIT = 56 * 1024 * 1024
FFT_N2 = 128
MOE_TILE = 256
ROW_DMA_UNROLL = 8

Q_OFF, K_OFF, V_OFF, Z_OFF, XBC_OFF, DT_OFF, FU_OFF, SU_OFF = 0, 512, 640, 768, 1280, 2048, 2176, 2688
IN_W_PAD = SU_OFF + 3 * SCONV_WIDTH
DT_W = 2 * SSM_HEADS


def _dot(a, b):
    return jnp.dot(a.astype(BF16), b.astype(BF16), preferred_element_type=F32)


def _dot_nt(a, b):
    return lax.dot_general(a.astype(BF16), b.astype(BF16), (((1,), (1,)), ((), ())),
                           preferred_element_type=F32)


def _dot_split(a_bf16, x):
    hi = x.astype(BF16)
    rest = x - hi.astype(F32)
    mid = rest.astype(BF16)
    lo = (rest - mid.astype(F32)).astype(BF16)
    return sum(jnp.dot(a_bf16, t, preferred_element_type=F32) for t in (hi, mid, lo))


def _sigmoid(x):
    return 1.0 / (1.0 + jnp.exp(-x))


def _silu(x):
    return x * _sigmoid(x)


def _norm_mod(x, g, shift, scale):
    y = x * lax.rsqrt(jnp.mean(x * x, axis=-1, keepdims=True) + EPS) * g
    return y * (1.0 + scale) + shift


def _params(*sem):
    return pltpu.CompilerParams(dimension_semantics=sem, vmem_limit_bytes=VMEM_LIMIT)


def _const_spec(shape):
    n = len(shape)
    return pl.BlockSpec(shape, lambda *_: (0,) * n, pipeline_mode=pl.Buffered(1))


def _ada_kernel(c_ref, w_ref, b_ref, o_ref):
    o_ref[0] = _dot(_silu(c_ref[...]), w_ref[0]) + b_ref[0]


def _ada(c8, w_ada, b_ada):
    nl, d, w = w_ada.shape
    tn = 512
    return pl.pallas_call(
        _ada_kernel, grid=(nl, w // tn),
        in_specs=[pl.BlockSpec((8, d), lambda l, j: (0, 0)),
                  pl.BlockSpec((1, d, tn), lambda l, j: (l, 0, j)),
                  pl.BlockSpec((1, 1, tn), lambda l, j: (l, 0, j))],
        out_specs=pl.BlockSpec((1, 8, tn), lambda l, j: (l, 0, j)),
        out_shape=jax.ShapeDtypeStruct((nl, 8, w), F32),
        compiler_params=_params("arbitrary", "arbitrary"), name="ada",
    )(c8, w_ada, b_ada.reshape(nl, 1, w))


def _inproj_kernel(x_ref, mod_ref, g_ref, w_ref, cos_ref, sin_ref,
                   q_ref, k_ref, v_ref, z_ref, xbc_ref, dt_ref, fu_ref, su_ref):
    m = mod_ref[0]
    hb = _norm_mod(x_ref[...], g_ref[...], m[0:1], m[1:2]).astype(BF16)
    cos = cos_ref[...]
    sin = sin_ref[...]
    lane = lax.broadcasted_iota(I32, cos.shape, 1)
    first_half = (lane % HEAD_DIM) < HEAD_DIM // 2

    def proj(start, width):
        return jnp.dot(hb, w_ref[:, start:start + width], preferred_element_type=F32)

    def rope(r):
        rot = jnp.where(first_half, pltpu.roll(r, LANES - HEAD_DIM // 2, 1), pltpu.roll(r, HEAD_DIM // 2, 1))
        return r * cos + rot * sin

    scale = HEAD_DIM ** -0.5
    for j in range(ATT_HEADS // 2):
        r = rope(proj(Q_OFF + j * LANES, LANES)) * scale
        q_ref[0, 2 * j] = r[:, :HEAD_DIM].astype(BF16)
        q_ref[0, 2 * j + 1] = r[:, HEAD_DIM:].astype(BF16)
    r = rope(proj(K_OFF, LANES))
    k_ref[0, 0] = r[:, :HEAD_DIM].astype(BF16)
    k_ref[0, 1] = r[:, HEAD_DIM:].astype(BF16)
    r = proj(V_OFF, LANES)
    v_ref[0, 0] = r[:, :HEAD_DIM].astype(BF16)
    v_ref[0, 1] = r[:, HEAD_DIM:].astype(BF16)
    z_ref[0] = proj(Z_OFF, SSM_INNER).astype(BF16)
    xbc_ref[0] = proj(XBC_OFF, SSM_XBC_W).astype(BF16)
    dt_ref[0] = proj(DT_OFF, LANES)
    fu_ref[0] = proj(FU_OFF, FNET_WIDTH).astype(BF16)
    su_ref[0] = proj(SU_OFF, 3 * SCONV_WIDTH).astype(BF16)


def _inproj(xflat, nb, seq, t, mod, mod_row, g, w, cos, sin):
    d = xflat.shape[1]
    nt = seq // t
    if mod_row is None:
        mod_map = lambda b, i: (b, 0, 0)
    else:
        mod_map = lambda b, i: (mod_row, 0, 0)
    sds = jax.ShapeDtypeStruct
    outs = (sds((nb, ATT_HEADS, seq, HEAD_DIM), BF16), sds((nb, ATT_KV_HEADS, seq, HEAD_DIM), BF16),
            sds((nb, ATT_KV_HEADS, seq, HEAD_DIM), BF16), sds((nb, seq, SSM_INNER), BF16),
            sds((nb, seq, SSM_XBC_W), BF16), sds((nb, seq, LANES), F32),
            sds((nb, seq, FNET_WIDTH), BF16), sds((nb, seq, 3 * SCONV_WIDTH), BF16))
    head_spec = lambda nh: pl.BlockSpec((1, nh, t, HEAD_DIM), lambda b, i: (b, 0, i, 0))
    row_spec = lambda wd: pl.BlockSpec((1, t, wd), lambda b, i: (b, i, 0))
    return pl.pallas_call(
        _inproj_kernel, grid=(nb, nt),
        in_specs=[pl.BlockSpec((t, d), lambda b, i: (b * nt + i, 0)),
                  pl.BlockSpec((1, 8, d), mod_map),
                  _const_spec((1, d)), _const_spec((d, IN_W_PAD)),
                  pl.BlockSpec((t, LANES), lambda b, i: (i, 0)),
                  pl.BlockSpec((t, LANES), lambda b, i: (i, 0))],
        out_specs=(head_spec(ATT_HEADS), head_spec(ATT_KV_HEADS), head_spec(ATT_KV_HEADS),
                   row_spec(SSM_INNER), row_spec(SSM_XBC_W), row_spec(LANES),
                   row_spec(FNET_WIDTH), row_spec(3 * SCONV_WIDTH)),
        out_shape=outs, compiler_params=_params("arbitrary", "arbitrary"), name="inproj",
    )(xflat, mod, g, w, cos, sin)


def _attn_kernel(sink_ref, q_ref, *refs, tq, seq, band):
    if band:
        bias_ref, kp_ref, kc_ref, kn_ref, vp_ref, vc_ref, vn_ref, kx_ref, vx_ref, o_ref = refs
    else:
        kx_ref, vx_ref, o_ref = refs
    kh = pl.program_id(1)
    if band:
        keys = jnp.concatenate([kp_ref[0, 0], kc_ref[0, 0], kn_ref[0, 0], kx_ref[0, 0]], axis=0)
        vals = jnp.concatenate([vp_ref[0, 0], vc_ref[0, 0], vn_ref[0, 0], vx_ref[0, 0]], axis=0)
        bias = bias_ref[0]
    else:
        keys, vals = kx_ref[0, 0], vx_ref[0, 0]
    ones_col = (lax.broadcasted_iota(I32, vals.shape, 1) == 0).astype(BF16)
    v_ext = jnp.concatenate([vals, ones_col], axis=1)
    outs = []
    for g in range(ATT_GROUP):
        s = _dot_nt(q_ref[0, g], keys)
        if band:
            s = s + bias
        sink = sink_ref[kh * ATT_GROUP + g]
        m = jnp.maximum(jnp.max(s, axis=1, keepdims=True), sink)
        acc = jnp.dot(jnp.exp(s - m).astype(BF16), v_ext, preferred_element_type=F32)
        den = acc[:, HEAD_DIM:HEAD_DIM + 1] + jnp.exp(sink - m)
        outs.append(acc[:, :HEAD_DIM] / den)
    o_ref[0] = jnp.concatenate(outs, axis=1).astype(BF16)


def _attention(sink, q, k, v, kx, vx, tq, band):
    nb, _, seq, _ = q.shape
    lc = kx.shape[2]
    nt = seq // tq
    r = tq // WINDOW
    nblk = seq // WINDOW
    qs = pl.BlockSpec((1, ATT_GROUP, tq, HEAD_DIM), lambda b, h, i: (b, h, i, 0))
    cur = pl.BlockSpec((1, 1, tq, HEAD_DIM), lambda b, h, i: (b, h, i, 0))
    prev = pl.BlockSpec((1, 1, WINDOW, HEAD_DIM), lambda b, h, i: (b, h, jnp.maximum(i * r - 1, 0), 0))
    nxt = pl.BlockSpec((1, 1, WINDOW, HEAD_DIM), lambda b, h, i: (b, h, jnp.minimum((i + 1) * r, nblk - 1), 0))
    ctx = pl.BlockSpec((1, 1, lc, HEAD_DIM), lambda b, h, i: (b, h, 0, 0))
    smem = pl.BlockSpec(memory_space=pltpu.SMEM)
    if band:
        nk = tq + 2 * WINDOW
        rel = np.arange(nk)[None, :] - WINDOW - np.arange(tq)[:, None]
        inside = np.abs(rel) <= WINDOW
        col = np.arange(nk)[None, :]
        kinds = [inside & ((col >= WINDOW) | (not first)) & ((col < tq + WINDOW) | (not lastt))
                 for lastt in (False, True) for first in (False, True)]
        kinds = np.concatenate([np.stack(kinds), np.ones((4, tq, lc), bool)], axis=2)
        bias = jnp.asarray(np.where(kinds, 0.0, -np.inf), F32)
        bias_spec = pl.BlockSpec((1, tq, nk + lc),
                                 lambda b, h, i: ((i == 0).astype(I32) + 2 * (i == nt - 1).astype(I32), 0, 0))
        in_specs = [smem, qs, bias_spec, prev, cur, nxt, prev, cur, nxt, ctx, ctx]
        args = (sink, q, bias, k, k, k, v, v, v, kx, vx)
    else:
        in_specs = [smem, qs, ctx, ctx]
        args = (sink, q, kx, vx)
    return pl.pallas_call(
        functools.partial(_attn_kernel, tq=tq, seq=seq, band=band),
        grid=(nb, ATT_KV_HEADS, nt), in_specs=in_specs,
        out_specs=pl.BlockSpec((1, tq, ATT_GROUP * HEAD_DIM), lambda b, h, i: (b, i, h)),
        out_shape=jax.ShapeDtypeStruct((nb, seq, ATT_HEADS * HEAD_DIM), BF16),
        compiler_params=_params("arbitrary", "arbitrary", "arbitrary"), name="attn_band" if band else "attn_ctx",
    )(*args)


def _ssd_kernel(xp_ref, xc_ref, xn_ref, dt_ref, cw_ref, cb_ref, dtb_ref, alog_ref, init_ref,
                *refs, reverse, with_y, finalize, nc):
    if finalize:
        yf_ref, z_ref, dsk_ref, nw_ref, y_ref, st_ref = refs
    elif with_y:
        y_ref, st_ref = refs
    else:
        (st_ref,) = refs
    q = SSM_CHUNK
    c = pl.program_id(1)
    ce = (nc - 1 - c) if reverse else c

    @pl.when(c == 0)
    def _():
        st_ref[...] = init_ref[...]

    xc = xc_ref[0].astype(F32)
    prow = jnp.where(ce == 0, 0.0, xp_ref[0].astype(F32)[BF16_ROWS - 1:BF16_ROWS])
    nrow = jnp.where(ce == nc - 1, 0.0, xn_ref[0].astype(F32)[0:1])
    rid = lax.broadcasted_iota(I32, xc.shape, 0)
    up = jnp.where(rid == 0, prow, pltpu.roll(xc, 1, 0))
    dn = jnp.where(rid == q - 1, nrow, pltpu.roll(xc, q - 1, 0))
    cw = cw_ref[...]
    act = _silu(cw[0:1] * up + cw[1:2] * xc + cw[2:3] * dn + cb_ref[...])
    xs = act[:, :SSM_INNER]
    bmat = act[:, SSM_INNER:SSM_INNER + SSM_BC_W]
    cmat = act[:, SSM_INNER + SSM_BC_W:]

    pre = dt_ref[0] + dtb_ref[...]
    dt = jnp.maximum(pre, 0.0) + jnp.log1p(jnp.exp(-jnp.abs(pre)))
    dta = dt * (-jnp.exp(alog_ref[...]))
    ri = lax.broadcasted_iota(I32, (q, q), 0)
    ci = lax.broadcasted_iota(I32, (q, q), 1)
    tri = (ri <= ci) if reverse else (ri >= ci)
    acs = _dot_split(tri.astype(BF16), dta)
    col0 = SSM_HEADS if reverse else 0
    lo_half = lax.broadcasted_iota(I32, (q, LANES), 1) < SSM_HEAD_DIM

    def bcast(mat, h):
        return jnp.broadcast_to(mat[:, col0 + h:col0 + h + 1], (q, LANES))

    def lane_expand(cols):
        return jnp.concatenate([jnp.where(lo_half, cols[2 * j], cols[2 * j + 1]) for j in range(SSM_HEADS // 2)],
                               axis=1)

    acs_b = [bcast(acs, h) for h in range(SSM_HEADS)]
    acs_e = lane_expand(acs_b)
    xd = xs * lane_expand([bcast(dt, h) for h in range(SSM_HEADS)])
    tot_e = acs_e[0:1] if reverse else acs_e[q - 1:q]
    hpg = SSM_HEADS // SSM_GROUPS
    gw = hpg * SSM_HEAD_DIM

    def grp(mat, g):
        return mat[:, g * SSM_STATE:(g + 1) * SSM_STATE]

    if with_y:
        acs_t = acs.T
        cb = [_dot_nt(grp(cmat, g), grp(bmat, g)) for g in range(SSM_GROUPS)]
        y_diag = []
        for j in range(SSM_HEADS // 2):
            sc = []
            for h in (2 * j, 2 * j + 1):
                seg = acs_b[h] - acs_t[col0 + h:col0 + h + 1, :]
                sc.append((cb[h // hpg] * jnp.exp(jnp.where(tri, seg, -jnp.inf))).astype(BF16))
            slab = xd[:, j * LANES:(j + 1) * LANES].astype(BF16)
            rhs = jnp.concatenate([jnp.where(lo_half, slab, 0), jnp.where(lo_half, 0, slab)], axis=0)
            y_diag.append(jnp.dot(jnp.concatenate(sc, axis=1), rhs, preferred_element_type=F32))
        y_off = [_dot(grp(cmat, g), st_ref[0, g]) for g in range(SSM_GROUPS)]
        y = jnp.concatenate(y_diag, axis=1) + jnp.concatenate(y_off, axis=1) * jnp.exp(acs_e)

    xdd = xd * jnp.exp(tot_e - acs_e)
    b_t = bmat.T
    for g in range(SSM_GROUPS):
        upd = _dot(b_t[g * SSM_STATE:(g + 1) * SSM_STATE, :], xdd[:, g * gw:(g + 1) * gw])
        st_ref[0, g] = st_ref[0, g] * jnp.exp(tot_e[:, g * gw:(g + 1) * gw]) + upd

    if finalize:
        yt = yf_ref[0] + y + dsk_ref[...] * xs
        yt = yt * _silu(z_ref[0].astype(F32))
        yt = yt * lax.rsqrt(jnp.mean(yt * yt, axis=-1, keepdims=True) + EPS) * nw_ref[...]
        y_ref[0] = yt.astype(BF16)
    elif with_y:
        y_ref[0] = y


def _ssd(xbc, dtr, cw, cb, dtb, alog, init, reverse, with_y, fin=None):
    nb, seq, _ = xbc.shape
    st_block = (1, SSM_GROUPS, SSM_STATE, SSM_INNER // SSM_GROUPS)
    nc = seq // SSM_CHUNK
    per = SSM_CHUNK // BF16_ROWS
    nhalo = seq // BF16_ROWS
    ce = (lambda c: nc - 1 - c) if reverse else (lambda c: c)
    xw = SSM_XBC_W
    in_specs = [pl.BlockSpec((1, BF16_ROWS, xw), lambda b, c: (b, jnp.maximum(ce(c) * per - 1, 0), 0)),
                pl.BlockSpec((1, SSM_CHUNK, xw), lambda b, c: (b, ce(c), 0)),
                pl.BlockSpec((1, BF16_ROWS, xw), lambda b, c: (b, jnp.minimum((ce(c) + 1) * per, nhalo - 1), 0)),
                pl.BlockSpec((1, SSM_CHUNK, LANES), lambda b, c: (b, ce(c), 0)),
                _const_spec((8, xw)), _const_spec((1, xw)), _const_spec((1, LANES)), _const_spec((1, LANES)),
                pl.BlockSpec(st_block, lambda b, c: (b, 0, 0, 0))]
    args = [xbc, xbc, xbc, dtr, cw, cb, dtb, alog, init]
    st_spec = pl.BlockSpec(st_block, lambda b, c: (b, 0, 0, 0))
    st_shape = jax.ShapeDtypeStruct((nb,) + st_block[1:], F32)
    y_spec = pl.BlockSpec((1, SSM_CHUNK, SSM_INNER), lambda b, c: (b, ce(c), 0))
    finalize = fin is not None
    if finalize:
        yf, z, dsk, nw = fin
        in_specs += [y_spec, y_spec, _const_spec((1, SSM_INNER)), _const_spec((1, SSM_INNER))]
        args += [yf, z, dsk, nw]
    if with_y:
        out_specs = (y_spec, st_spec)
        out_shape = (jax.ShapeDtypeStruct((nb, seq, SSM_INNER), BF16 if finalize else F32), st_shape)
    else:
        out_specs = (st_spec,)
        out_shape = (st_shape,)
    res = pl.pallas_call(
        functools.partial(_ssd_kernel, reverse=reverse, with_y=with_y, finalize=finalize, nc=nc),
        grid=(nb, nc), in_specs=in_specs, out_specs=out_specs, out_shape=out_shape,
        compiler_params=_params("arbitrary", "arbitrary"), name="ssd_bwd" if reverse else "ssd_fwd",
    )(*args)
    return res if with_y else (None, res[0])


def _fft_a_kernel(u_ref, wc_ref, m1_ref, y_ref, wc_bf):
    @pl.when((pl.program_id(0) == 0) & (pl.program_id(1) == 0))
    def _():
        wc_bf[...] = wc_ref[...].astype(BF16)

    v = _dot(u_ref[0], wc_bf[...])
    vs = jnp.concatenate([v[:, :FNET_WIDTH], v[:, FNET_WIDTH:]], axis=0)
    b = _dot(m1_ref[0], vs)
    y_ref[0, 0, 0] = b[:FFT_N2].astype(BF16)
    y_ref[0, 1, 0] = b[FFT_N2:].astype(BF16)


def _fft_c_kernel(y_ref, m2_ref, o_ref):
    o_ref[0] = _dot(m2_ref[...], y_ref[0]).astype(BF16)


def _fft_small_kernel(u_ref, wc_ref, m_ref, o_ref):
    v = _dot(u_ref[0], wc_ref[...])
    vs = jnp.concatenate([v[:, :FNET_WIDTH], v[:, FNET_WIDTH:]], axis=0)
    o_ref[0] = _dot(m_ref[...], vs).astype(BF16)


def _channel_dft():
    idx = np.arange(FNET_GROUP_DIM)
    ang = 2.0 * np.pi * np.outer(idx, idx) / FNET_GROUP_DIM
    eye = np.eye(FNET_GROUPS)
    return jnp.asarray(np.concatenate([np.kron(eye, np.cos(ang)), -np.kron(eye, np.sin(ang))], axis=1), F32)


def _fourier_long(u):
    nb, seq, w = u.shape
    n2 = FFT_N2
    n1 = seq // n2
    t2 = np.arange(n2)
    k2 = np.arange(n2)
    t1 = np.arange(n1)
    ang = 2.0 * np.pi * (np.outer(k2, t2)[None] / n2 + (t1[:, None, None] * k2[None, :, None]) / seq)
    co, si = np.cos(ang), np.sin(ang)
    m1 = jnp.asarray(np.concatenate([np.concatenate([co, si], axis=2),
                                     np.concatenate([-si, co], axis=2)], axis=1), F32)
    ang1 = 2.0 * np.pi * np.outer(t1, t1) / n1
    norm = 1.0 / math.sqrt(seq * FNET_GROUP_DIM)
    m2 = jnp.asarray(np.concatenate([np.cos(ang1), np.sin(ang1)], axis=1) * norm, F32)
    y = pl.pallas_call(
        _fft_a_kernel, grid=(nb, n1),
        in_specs=[pl.BlockSpec((1, n2, w), lambda b, j: (b, 0, j)),
                  _const_spec((w, 2 * w)),
                  pl.BlockSpec((1, 2 * n2, 2 * n2), lambda b, j: (j, 0, 0))],
        out_specs=pl.BlockSpec((1, 2, 1, n2, w), lambda b, j: (b, 0, j, 0, 0)),
        out_shape=jax.ShapeDtypeStruct((nb, 2, n1, n2, w), BF16),
        scratch_shapes=[pltpu.VMEM((w, 2 * w), BF16)],
        compiler_params=_params("arbitrary", "arbitrary"), name="fft_a",
    )(u.reshape(nb, n2, n1 * w), _channel_dft(), m1)
    nl = 2048
    out = pl.pallas_call(
        _fft_c_kernel, grid=(nb, n2 * w // nl),
        in_specs=[pl.BlockSpec((1, 2 * n1, nl), lambda b, j: (b, 0, j)),
                  _const_spec((n1, 2 * n1))],
        out_specs=pl.BlockSpec((1, n1, nl), lambda b, j: (b, 0, j)),
        out_shape=jax.ShapeDtypeStruct((nb, n1, n2 * w), BF16),
        compiler_params=_params("arbitrary", "arbitrary"), name="fft_c",
    )(y.reshape(nb, 2 * n1, n2 * w), m2)
    return out.reshape(nb, seq, w)


def _fourier_short(u):
    nb, seq, w = u.shape
    t = np.arange(seq)
    ang = 2.0 * np.pi * np.outer(t, t) / seq
    norm = 1.0 / math.sqrt(seq * FNET_GROUP_DIM)
    m = jnp.asarray(np.concatenate([np.cos(ang), np.sin(ang)], axis=1) * norm, F32)
    return pl.pallas_call(
        _fft_small_kernel, grid=(nb,),
        in_specs=[pl.BlockSpec((1, seq, w), lambda b: (b, 0, 0)),
                  _const_spec((w, 2 * w)), _const_spec((seq, 2 * seq))],
        out_specs=pl.BlockSpec((1, seq, w), lambda b: (b, 0, 0)),
        out_shape=jax.ShapeDtypeStruct((nb, seq, w), BF16),
        compiler_params=_params("arbitrary"), name="fft_small",
    )(u, _channel_dft(), m)


def _merge_kernel(x_ref, mod_ref, g1_ref, g2_ref, att_ref, ssm_ref, fft_ref, sup_ref, su_ref, sun_ref,
                  sw_ref, wg_ref, bg_ref, wb_ref, wo_ref, x1_ref, h2_ref, *, nt):
    d = x_ref.shape[1]
    t = x_ref.shape[0]
    i = pl.program_id(1)
    x = x_ref[...]
    m = mod_ref[0]
    hb = _norm_mod(x, g1_ref[...], m[0:1], m[1:2]).astype(BF16)

    def gated(su):
        return su[:, SCONV_WIDTH:2 * SCONV_WIDTH] * su[:, 2 * SCONV_WIDTH:]

    su = su_ref[0].astype(F32)
    p = gated(su)
    prow = jnp.where(i == 0, 0.0, gated(sup_ref[0].astype(F32)[BF16_ROWS - 1:BF16_ROWS]))
    nrow = jnp.where(i == nt - 1, 0.0, gated(sun_ref[0].astype(F32)[0:1]))
    rid = lax.broadcasted_iota(I32, p.shape, 0)
    up = jnp.where(rid == 0, prow, pltpu.roll(p, 1, 0))
    dn = jnp.where(rid == t - 1, nrow, pltpu.roll(p, t - 1, 0))
    sw = sw_ref[...]
    sconv = su[:, :SCONV_WIDTH] * (sw[0:1] * up + sw[1:2] * p + sw[2:3] * dn)

    branches = (att_ref[0], ssm_ref[0], fft_ref[0], sconv.astype(BF16))
    acc = jnp.zeros((t, d), F32)
    for n in range(N_BRANCHES):
        gate = _sigmoid(jnp.dot(hb, wg_ref[:, n * d:(n + 1) * d], preferred_element_type=F32)
                        + bg_ref[:, n * d:(n + 1) * d])
        acc = acc + gate * jnp.dot(branches[n], wb_ref[n], preferred_element_type=F32)
    y = jnp.dot(acc.astype(BF16), wo_ref[...], preferred_element_type=F32)
    x1 = x + m[2:3] * y
    x1_ref[...] = x1
    h2_ref[...] = _norm_mod(x1, g2_ref[...], m[3:4], m[4:5])


def _merge(xflat, nb, seq, t, mod, mod_row, g1, g2, att, ssm, fft, su, sw, wg, bg, wb, wo):
    ntok, d = xflat.shape
    nt = seq // t
    per = t // BF16_ROWS
    nhalo = seq // BF16_ROWS
    mod_map = (lambda b, i: (b, 0, 0)) if mod_row is None else (lambda b, i: (mod_row, 0, 0))
    row = pl.BlockSpec((t, d), lambda b, i: (b * nt + i, 0))
    br = pl.BlockSpec((1, t, BRANCH_W), lambda b, i: (b, i, 0))
    suw = 3 * SCONV_WIDTH
    in_specs = [row, pl.BlockSpec((1, 8, d), mod_map), _const_spec((1, d)), _const_spec((1, d)),
                br, br, br,
                pl.BlockSpec((1, BF16_ROWS, suw), lambda b, i: (b, jnp.maximum(i * per - 1, 0), 0)),
                pl.BlockSpec((1, t, suw), lambda b, i: (b, i, 0)),
                pl.BlockSpec((1, BF16_ROWS, suw), lambda b, i: (b, jnp.minimum((i + 1) * per, nhalo - 1), 0)),
                _const_spec((8, SCONV_WIDTH)), _const_spec((d, N_BRANCHES * d)), _const_spec((1, N_BRANCHES * d)),
                _const_spec((N_BRANCHES, BRANCH_W, d)), _const_spec((d, d))]
    return pl.pallas_call(
        functools.partial(_merge_kernel, nt=nt),
        grid=(nb, nt), in_specs=in_specs, out_specs=(row, row),
        out_shape=(jax.ShapeDtypeStruct((ntok, d), F32), jax.ShapeDtypeStruct((ntok, d), F32)),
        compiler_params=_params("arbitrary", "arbitrary"), name="merge",
    )(xflat, mod, g1, g2, att, ssm, fft, su, su, su, sw, wg, bg, wb, wo)


def _route_kernel(h_ref, wr_ref, rb_ref, cin_ref, ri_ref, rw_ref, cnt_ref):
    t = h_ref.shape[0]
    step = pl.program_id(0)

    @pl.when(step == 0)
    def _():
        cnt_ref[...] = cin_ref[...]

    sc = _sigmoid(_dot(h_ref[...], wr_ref[...]))
    sel = sc + rb_ref[...]
    s = [sel[:, j * LANES:(j + 1) * LANES] for j in range(EXPERTS_PER_GROUP)]
    u = [sc[:, j * LANES:(j + 1) * LANES] for j in range(EXPERTS_PER_GROUP)]
    gs = None
    for a in range(EXPERTS_PER_GROUP):
        for b in range(a + 1, EXPERTS_PER_GROUP):
            pair = s[a] + s[b]
            gs = pair if gs is None else jnp.maximum(gs, pair)
    lane = lax.broadcasted_iota(I32, (t, LANES), 1)
    gmax = jnp.max(gs, axis=1, keepdims=True)
    best = jnp.min(jnp.where(gs == gmax, lane, LANES), axis=1, keepdims=True)
    on = lane == best
    v = [jnp.sum(jnp.where(on, s[j], 0.0), axis=1, keepdims=True) for j in range(EXPERTS_PER_GROUP)]
    w = [jnp.sum(jnp.where(on, u[j], 0.0), axis=1, keepdims=True) for j in range(EXPERTS_PER_GROUP)]

    def first_argmax(vals):
        mx = vals[0]
        for x in vals[1:]:
            mx = jnp.maximum(mx, x)
        idx = jnp.full_like(best, len(vals) - 1)
        for j in range(len(vals) - 2, -1, -1):
            idx = jnp.where(vals[j] == mx, j, idx)
        return idx

    def pick(vals, idx):
        out = vals[-1]
        for j in range(len(vals) - 2, -1, -1):
            out = jnp.where(idx == j, vals[j], out)
        return out

    l1 = first_argmax(v)
    l2 = first_argmax([jnp.where(l1 == j, -jnp.inf, v[j]) for j in range(EXPERTS_PER_GROUP)])
    w1, w2 = pick(w, l1), pick(w, l2)
    tot = w1 + w2
    e1 = best * EXPERTS_PER_GROUP + l1
    e2 = best * EXPERTS_PER_GROUP + l2

    oh1 = (lane == e1).astype(F32)
    oh2 = (lane == e2).astype(F32)
    oh = oh1 + oh2
    ri_ = lax.broadcasted_iota(I32, (t, t), 0)
    ci_ = lax.broadcasted_iota(I32, (t, t), 1)
    before = _dot((ri_ > ci_).astype(F32), oh) + cnt_ref[0:1]
    r1 = jnp.sum(oh1 * before, axis=1, keepdims=True).astype(I32)
    r2 = jnp.sum(oh2 * before, axis=1, keepdims=True).astype(I32)
    cnt_ref[...] = cnt_ref[...] + jnp.sum(oh, axis=0, keepdims=True)

    ri_ref[...] = jnp.where(lane == 0, e1, jnp.where(lane == 1, e2, jnp.where(lane == 2, r1, jnp.where(lane == 3, r2, 0))))
    rw_ref[...] = jnp.where(lane == 0, w1 / tot, jnp.where(lane == 1, w2 / tot, 0.0))


def _route(h2, wr, rb, cnt_in):
    ntok, d = h2.shape
    t = 512 if ntok % 512 == 0 else 256
    return pl.pallas_call(
        _route_kernel, grid=(ntok // t,),
        in_specs=[pl.BlockSpec((t, d), lambda i: (i, 0)),
                  _const_spec((d, EXPERTS_PER_GROUP * LANES)), _const_spec((1, EXPERTS_PER_GROUP * LANES)),
                  _const_spec((8, LANES))],
        out_specs=(pl.BlockSpec((t, LANES), lambda i: (i, 0)), pl.BlockSpec((t, LANES), lambda i: (i, 0)),
                   pl.BlockSpec((8, LANES), lambda i: (0, 0))),
        out_shape=(jax.ShapeDtypeStruct((ntok, LANES), I32), jax.ShapeDtypeStruct((ntok, LANES), F32),
                   jax.ShapeDtypeStruct((8, LANES), F32)),
        compiler_params=_params("arbitrary"), name="route",
    )(h2, wr, rb, cnt_in)


def _dispatch_kernel(dest_ref, zt_ref, *refs, td, starts):
    n_streams = len(starts) - 1
    h_refs = refs[:n_streams]
    xs_hbm, zbuf, zsem, sem = refs[n_streams:]
    i = pl.program_id(0)

    @pl.when(i == 0)
    def _():
        zbuf[...] = jnp.zeros_like(zbuf)

        def zero_copy(j):
            start = pl.multiple_of(zt_ref[j] * MOE_TILE, MOE_TILE)
            return pltpu.make_async_copy(zbuf, xs_hbm.at[pl.ds(start, MOE_TILE)], zsem)

        def zissue(j, carry):
            @pl.when(zt_ref[j] >= 0)
            def _():
                zero_copy(j).start()
            return carry

        def zdrain(j, carry):
            @pl.when(zt_ref[j] >= 0)
            def _():
                zero_copy(j).wait()
            return carry

        lax.fori_loop(0, zt_ref.shape[0], zissue, 0)
        lax.fori_loop(0, zt_ref.shape[0], zdrain, 0)

    for s in range(n_streams):
        h_ref = h_refs[s]

        def row_copy(src_row, dst_row, h_ref=h_ref):
            return pltpu.make_async_copy(h_ref.at[pl.ds(src_row, 1)], xs_hbm.at[pl.ds(dst_row, 1)], sem)

        @pl.when((i >= starts[s]) & (i < starts[s + 1]))
        def _(row_copy=row_copy):
            def issue(j, carry):
                row_copy(j, dest_ref[0, 0, 2 * j]).start(priority=0)
                row_copy(j, dest_ref[0, 0, 2 * j + 1]).start(priority=1)
                return carry

            def drain(j, carry):
                row_copy(0, 0).wait()
                return carry

            lax.fori_loop(0, td, issue, 0, unroll=ROW_DMA_UNROLL)
            lax.fori_loop(0, 2 * td, drain, 0, unroll=ROW_DMA_UNROLL)


def _dispatch(h2s, dest, ztiles, nrows):
    d = h2s[0].shape[1]
    td = 256
    starts = [0]
    for h in h2s:
        starts.append(starts[-1] + h.shape[0] // td)
    nt = starts[-1]

    def stream_spec(s):
        return pl.BlockSpec((td, d), lambda i: (jnp.clip(i - starts[s], 0, starts[s + 1] - starts[s] - 1), 0))

    return pl.pallas_call(
        functools.partial(_dispatch_kernel, td=td, starts=tuple(starts)), grid=(nt,),
        in_specs=[pl.BlockSpec((1, 1, 2 * td), lambda i: (i, 0, 0), memory_space=pltpu.SMEM),
                  pl.BlockSpec(memory_space=pltpu.SMEM)] + [stream_spec(s) for s in range(len(h2s))],
        out_specs=pl.BlockSpec(memory_space=pl.ANY),
        out_shape=jax.ShapeDtypeStruct((nrows, d), F32),
        scratch_shapes=[pltpu.VMEM((MOE_TILE, d), F32), pltpu.SemaphoreType.DMA(()), pltpu.SemaphoreType.DMA(())],
        compiler_params=_params("arbitrary"), name="dispatch",
    )(dest.reshape(nt, 1, 2 * td), ztiles, *h2s)


def _ffn_kernel(be_ref, nu_ref, x_ref, w1_ref, w3_ref, w2_ref, y_ref, w1_bf, w3_bf, w2_bf):
    i = pl.program_id(0)
    used = i < nu_ref[0]

    @pl.when(used & ((i == 0) | (be_ref[i] != be_ref[jnp.maximum(i - 1, 0)])))
    def _():
        w1_bf[...] = w1_ref[0, 0].astype(BF16)
        w3_bf[...] = w3_ref[0, 0].astype(BF16)
        w2_bf[...] = w2_ref[0, 0].astype(BF16)

    @pl.when(used)
    def _():
        xb = x_ref[...].astype(BF16)
        a = jnp.dot(xb, w1_bf[...], preferred_element_type=F32)
        b = jnp.dot(xb, w3_bf[...], preferred_element_type=F32)
        y_ref[...] = jnp.dot((_silu(a) * b).astype(BF16), w2_bf[...], preferred_element_type=F32)

    @pl.when(i >= nu_ref[0])
    def _():
        y_ref[...] = jnp.zeros_like(y_ref)


def _ffn(xs, block_expert, n_used, w1, w3, w2, layer):
    nrows, d = xs.shape
    de = w1.shape[3]
    tm = MOE_TILE
    grid_spec = pltpu.PrefetchScalarGridSpec(
        num_scalar_prefetch=2, grid=(nrows // tm,),
        in_specs=[pl.BlockSpec((tm, d), lambda i, be, nu: (jnp.minimum(i, nu[0] - 1), 0)),
                  pl.BlockSpec((1, 1, d, de), lambda i, be, nu: (layer, be[i], 0, 0)),
                  pl.BlockSpec((1, 1, d, de), lambda i, be, nu: (layer, be[i], 0, 0)),
                  pl.BlockSpec((1, 1, de, d), lambda i, be, nu: (layer, be[i], 0, 0))],
        out_specs=pl.BlockSpec((tm, d), lambda i, be, nu: (i, 0)),
        scratch_shapes=[pltpu.VMEM((d, de), BF16), pltpu.VMEM((d, de), BF16), pltpu.VMEM((de, d), BF16)])
    return pl.pallas_call(
        _ffn_kernel, grid_spec=grid_spec, out_shape=jax.ShapeDtypeStruct((nrows, d), F32),
        compiler_params=_params("arbitrary"), name="ffn",
    )(block_expert, n_used, xs, w1, w3, w2)


def _combine_kernel(dest_ref, x_ref, mod_ref, rw_ref, fw_ref, y_hbm, o_ref, buf, sem, *, tc, final):
    def row_copy(src_row, k, j):
        return pltpu.make_async_copy(y_hbm.at[pl.ds(src_row, 1)], buf.at[k, pl.ds(j, 1)], sem)

    def issue(j, carry):
        row_copy(dest_ref[0, 0, 2 * j], 0, j).start(priority=0)
        row_copy(dest_ref[0, 0, 2 * j + 1], 1, j).start(priority=1)
        return carry

    def drain(j, carry):
        row_copy(0, 0, 0).wait()
        return carry

    lax.fori_loop(0, tc, issue, 0, unroll=ROW_DMA_UNROLL)
    lax.fori_loop(0, 2 * tc, drain, 0, unroll=ROW_DMA_UNROLL)
    rw = rw_ref[...]
    f = rw[:, 0:1] * buf[0] + rw[:, 1:2] * buf[1]
    x = x_ref[...] + mod_ref[0][5:6] * f
    if final:
        x = x * lax.rsqrt(jnp.mean(x * x, axis=-1, keepdims=True) + EPS) * fw_ref[...]
    o_ref[...] = x


def _combine(x1, dest, rw, y, mod, mod_row, seq, fw, final):
    ntok, d = x1.shape
    tc = 256
    nt = ntok // tc
    per_seq = seq // tc
    mod_map = (lambda i: (i // per_seq, 0, 0)) if mod_row is None else (lambda i: (mod_row, 0, 0))
    return pl.pallas_call(
        functools.partial(_combine_kernel, tc=tc, final=final), grid=(nt,),
        in_specs=[pl.BlockSpec((1, 1, 2 * tc), lambda i: (i, 0, 0), memory_space=pltpu.SMEM),
                  pl.BlockSpec((tc, d), lambda i: (i, 0)),
                  pl.BlockSpec((1, 8, d), mod_map),
                  pl.BlockSpec((tc, LANES), lambda i: (i, 0)),
                  _const_spec((1, d)),
                  pl.BlockSpec(memory_space=pl.ANY)],
        out_specs=pl.BlockSpec((tc, d), lambda i: (i, 0)),
        out_shape=jax.ShapeDtypeStruct((ntok, d), F32),
        scratch_shapes=[pltpu.VMEM((2, tc, d), F32), pltpu.SemaphoreType.DMA(())],
        compiler_params=_params("arbitrary"), name="combine",
    )(dest.reshape(nt, 1, 2 * tc), x1, mod, rw, fw, y)


def _moe(streams, mod, wr, rb, w1, w3, w2, layer, fw, final):
    d = streams[0][0].shape[1]
    cnt = jnp.zeros((8, LANES), F32)
    routed = []
    for _, h2, _, _ in streams:
        ri, rw, cnt = _route(h2, wr, rb, cnt)
        routed.append((ri, rw))
    counts = cnt[0, :N_EXPERTS].astype(I32)
    padded = (counts + MOE_TILE - 1) // MOE_TILE * MOE_TILE
    pad_end = jnp.cumsum(padded)
    offs = pad_end - padded
    ntok = sum(s[0].shape[0] for s in streams)
    n_tiles = -(-2 * ntok // MOE_TILE) + N_EXPERTS
    tile_start = jnp.arange(n_tiles, dtype=I32) * MOE_TILE
    block_expert = jnp.minimum(jnp.sum((pad_end[None, :] <= tile_start[:, None]).astype(I32), axis=1), N_EXPERTS - 1)
    n_used = (pad_end[-1:] // MOE_TILE).astype(I32)
    dests = [(offs[ri[:, 0:2]] + ri[:, 2:4]).reshape(-1) for ri, _ in routed]
    tail = n_used + jnp.arange(N_EXPERTS, dtype=I32)
    ztiles = jnp.concatenate([jnp.where(padded > 0, pad_end // MOE_TILE - 1, -1),
                              jnp.where(tail < n_tiles, tail, -1)]).astype(I32)
    xs = _dispatch([s[1] for s in streams], jnp.concatenate(dests), ztiles, n_tiles * MOE_TILE)
    y = _ffn(xs, block_expert, n_used, w1, w3, w2, layer)
    return [_combine(x1, dest, rw, y, mod, mod_row, seq, fw, final)
            for (x1, _, seq, mod_row), dest, (_, rw) in zip(streams, dests, routed)]


def _pad_rows(a, rows):
    return jnp.concatenate([a, jnp.zeros((rows - a.shape[0],) + a.shape[1:], a.dtype)], axis=0)


def _pad_cols(a, cols, value=0.0):
    return jnp.concatenate([a, jnp.full(a.shape[:-1] + (cols - a.shape[-1],), value, a.dtype)], axis=-1)


def _rope_tables(seq):
    rows = seq // GRID_W
    row = jnp.repeat(jnp.arange(rows, dtype=F32), GRID_W)
    col = jnp.tile(jnp.arange(GRID_W, dtype=F32), rows)
    pairs = HEAD_DIM // 4
    inv_freq = ROPE_THETA ** (-jnp.arange(pairs, dtype=F32) / pairs)
    ang = jnp.concatenate([row[:, None] * inv_freq, col[:, None] * inv_freq], axis=-1)
    cos, sin = jnp.cos(ang), jnp.sin(ang)
    cos_t = jnp.tile(cos, (1, LANES // (HEAD_DIM // 2)))
    sin_t = jnp.tile(jnp.concatenate([-sin, sin], axis=-1), (1, LANES // HEAD_DIM))
    return cos_t, sin_t


def kernel(x, c, ctx, c_ctx, norm1_w, norm2_w, w_ada, b_ada, w_in, attn_sink, ssm_conv_w, ssm_conv_b,
           ssm_dt_bias, ssm_a_log, ssm_d, ssm_norm_w, sconv_w, w_gate, b_gate, w_branch, w_o,
           w_router, router_bias, moe_w1, moe_w3, moe_w2, final_norm_w):
    nb, seq, d = x.shape
    lc = ctx.shape[1]
    depth = w_in.shape[0]
    n_lat = nb * seq
    n_ctx = nb * lc
    assert seq % 512 == 0 and seq % (FFT_N2 * 8) == 0 and lc % 256 == 0 and nb < 8

    xl = x.reshape(n_lat, d)
    xc = ctx.reshape(n_ctx, d)
    c8 = _pad_rows(jnp.concatenate([c, c_ctx[None]], axis=0), 8)
    mods = _ada(c8, w_ada, b_ada)
    mods = _pad_cols(mods.reshape(depth, 8, 6, d).swapaxes(-1, -2), 8).swapaxes(-1, -2)

    cos_l, sin_l = _rope_tables(seq)
    cos_c, sin_c = jnp.ones((lc, LANES), F32), jnp.zeros((lc, LANES), F32)

    wr = w_router.reshape(d, N_EXPERT_GROUPS, EXPERTS_PER_GROUP).transpose(0, 2, 1)
    wr = _pad_cols(wr, LANES).reshape(d, EXPERTS_PER_GROUP * LANES).astype(BF16)
    rb = _pad_cols(router_bias.reshape(N_EXPERT_GROUPS, EXPERTS_PER_GROUP).T, LANES, -1e30).reshape(1, -1)

    for layer in range(depth):
        last = layer + 1 == depth
        mod = mods[layer]
        wi = w_in[layer]
        w_pad = jnp.concatenate([wi[:, :2048], _pad_cols(wi[:, 2048:2048 + DT_W], LANES), wi[:, 2048 + DT_W:]],
                                axis=1).astype(BF16)
        g1 = norm1_w[layer][None]
        g2 = norm2_w[layer][None]
        cw = _pad_rows(ssm_conv_w[layer], 8)
        cb = ssm_conv_b[layer][None]
        dtb = _pad_cols(ssm_dt_bias[layer].reshape(1, DT_W), LANES)
        alog = _pad_cols(ssm_a_log[layer].reshape(1, DT_W), LANES)
        dsk = jnp.repeat(ssm_d[layer], SSM_HEAD_DIM)[None]
        nw = ssm_norm_w[layer][None]
        sw = _pad_rows(sconv_w[layer], 8)
        wg = w_gate[layer].astype(BF16)
        bg = b_gate[layer][None]
        wb = w_branch[layer].astype(BF16)
        wo = w_o[layer].astype(BF16)
        sink = attn_sink[layer]

        q, k, v, z, xbc, dtr, fu, su = _inproj(xl, nb, seq, 512, mod, None, g1, w_pad, cos_l, sin_l)
        qc, kc, vc, zc, xbcc, dtrc, fuc, suc = _inproj(xc, nb, lc, 256, mod, nb, g1, w_pad, cos_c, sin_c)

        att = _attention(sink, q, k, v, kc, vc, 256, True)

        zero_state = jnp.zeros((nb, SSM_GROUPS, SSM_STATE, SSM_INNER // SSM_GROUPS), F32)
        ssd_c = functools.partial(_ssd, xbcc, dtrc, cw, cb, dtb, alog)
        ssd_l = functools.partial(_ssd, xbc, dtr, cw, cb, dtb, alog)
        if last:
            _, fin_f = ssd_c(zero_state, False, False)
            _, fin_b = ssd_c(zero_state, True, False)
        else:
            yc_f, fin_f = ssd_c(zero_state, False, True)
            ssm_c, fin_b = ssd_c(zero_state, True, True, fin=(yc_f, zc, dsk, nw))
        y_f, _ = ssd_l(fin_f, False, True)
        ssm, _ = ssd_l(fin_b, True, True, fin=(y_f, z, dsk, nw))

        fft = _fourier_long(fu)

        x1, h2 = _merge(xl, nb, seq, 256, mod, None, g1, g2, att, ssm, fft, su, sw, wg, bg, wb, wo)
        streams = [(x1, h2, seq, None)]
        if not last:
            att_c = _attention(sink, qc, None, None, kc, vc, lc, False)
            fft_c = _fourier_short(fuc)
            x1c, h2c = _merge(xc, nb, lc, 256, mod, nb, g1, g2, att_c, ssm_c, fft_c, suc, sw, wg, bg, wb, wo)
            streams.append((x1c, h2c, lc, nb))
        new = _moe(streams, mod, wr, rb, moe_w1, moe_w3, moe_w2, layer, final_norm_w[None], last)
        xl = new[0]
        if not last:
            xc = new[1]
    return xl.reshape(nb, seq, d)
```

```python
import functools
import math

import numpy as np
import jax
import jax.numpy as jnp
from jax import lax
from jax.experimental import pallas as pl
from jax.experimental.pallas import tpu as pltpu

F32 = jnp.float32
BF16 = jnp.bfloat16
I32 = jnp.int32

EPS = 1e-6
GRID_W = 64
ROPE_THETA = 10000.0
HEAD_DIM = 64
ATT_HEADS = 8
ATT_KV_HEADS = 2
ATT_GROUP = ATT_HEADS // ATT_KV_HEADS
WINDOW = 128
SSM_HEADS = 8
SSM_HEAD_DIM = 64
SSM_INNER = SSM_HEADS * SSM_HEAD_DIM
SSM_GROUPS = 2
SSM_STATE = 64
SSM_BC_W = SSM_GROUPS * SSM_STATE
SSM_XBC_W = SSM_INNER + 2 * SSM_BC_W
SSM_CHUNK = 128
FNET_GROUPS = 8
FNET_GROUP_DIM = 64
FNET_WIDTH = FNET_GROUPS * FNET_GROUP_DIM
SCONV_WIDTH = 512
BRANCH_W = 512
N_BRANCHES = 4
N_EXPERTS = 32
N_EXPERT_GROUPS = 8
EXPERTS_PER_GROUP = N_EXPERTS // N_EXPERT_GROUPS
D_EXPERT = 512

LANES = 128
BF16_ROWS = 16
VMEM_LIMIT = 56 * 1024 * 1024
FFT_N2 = 128
FFT_T1_PER_STEP = 4
MOE_TILE = 256
ROW_DMA_UNROLL = 8

Q_OFF, K_OFF, V_OFF, Z_OFF, XBC_OFF, DT_OFF, FU_OFF, SU_OFF = 0, 512, 640, 768, 1280, 2048, 2176, 2688
IN_W_PAD = SU_OFF + 3 * SCONV_WIDTH
DT_W = 2 * SSM_HEADS


def _dot(a, b):
    return jnp.dot(a.astype(BF16), b.astype(BF16), preferred_element_type=F32)


def _dot_nt(a, b):
    return lax.dot_general(a.astype(BF16), b.astype(BF16), (((1,), (1,)), ((), ())),
                           preferred_element_type=F32)


def _dot_split(a_bf16, x):
    hi = x.astype(BF16)
    rest = x - hi.astype(F32)
    mid = rest.astype(BF16)
    lo = (rest - mid.astype(F32)).astype(BF16)
    return sum(jnp.dot(a_bf16, t, preferred_element_type=F32) for t in (hi, mid, lo))


def _sigmoid(x):
    return 1.0 / (1.0 + jnp.exp(-x))


def _silu(x):
    return x * _sigmoid(x)


def _norm_mod(x, g, shift, scale):
    y = x * lax.rsqrt(jnp.mean(x * x, axis=-1, keepdims=True) + EPS) * g
    return y * (1.0 + scale) + shift


def _params(*sem):
    return pltpu.CompilerParams(dimension_semantics=sem, vmem_limit_bytes=VMEM_LIMIT)


def _const_spec(shape):
    n = len(shape)
    return pl.BlockSpec(shape, lambda *_: (0,) * n, pipeline_mode=pl.Buffered(1))


def _ada_kernel(c_ref, w_ref, b_ref, o_ref):
    o_ref[0] = _dot(_silu(c_ref[...]), w_ref[0]) + b_ref[0]


def _ada(c8, w_ada, b_ada):
    nl, d, w = w_ada.shape
    tn = 512
    return pl.pallas_call(
        _ada_kernel, grid=(nl, w // tn),
        in_specs=[pl.BlockSpec((8, d), lambda l, j: (0, 0)),
                  pl.BlockSpec((1, d, tn), lambda l, j: (l, 0, j)),
                  pl.BlockSpec((1, 1, tn), lambda l, j: (l, 0, j))],
        out_specs=pl.BlockSpec((1, 8, tn), lambda l, j: (l, 0, j)),
        out_shape=jax.ShapeDtypeStruct((nl, 8, w), F32),
        compiler_params=_params("arbitrary", "arbitrary"), name="ada",
    )(c8, w_ada, b_ada.reshape(nl, 1, w))


def _inproj_kernel(x_ref, mod_ref, g_ref, w_ref, cos_ref, sin_ref,
                   q_ref, k_ref, v_ref, z_ref, xbc_ref, dt_ref, fu_ref, su_ref):
    m = mod_ref[0]
    hb = _norm_mod(x_ref[...], g_ref[...], m[0:1], m[1:2]).astype(BF16)
    cos = cos_ref[...]
    sin = sin_ref[...]
    lane = lax.broadcasted_iota(I32, cos.shape, 1)
    first_half = (lane % HEAD_DIM) < HEAD_DIM // 2

    def proj(start, width):
        return jnp.dot(hb, w_ref[:, start:start + width], preferred_element_type=F32)

    def rope(r):
        rot = jnp.where(first_half, pltpu.roll(r, LANES - HEAD_DIM // 2, 1), pltpu.roll(r, HEAD_DIM // 2, 1))
        return r * cos + rot * sin

    scale = HEAD_DIM ** -0.5
    for j in range(ATT_HEADS // 2):
        r = rope(proj(Q_OFF + j * LANES, LANES)) * scale
        q_ref[0, 2 * j] = r[:, :HEAD_DIM].astype(BF16)
        q_ref[0, 2 * j + 1] = r[:, HEAD_DIM:].astype(BF16)
    r = rope(proj(K_OFF, LANES))
    k_ref[0, 0] = r[:, :HEAD_DIM].astype(BF16)
    k_ref[0, 1] = r[:, HEAD_DIM:].astype(BF16)
    r = proj(V_OFF, LANES)
    v_ref[0, 0] = r[:, :HEAD_DIM].astype(BF16)
    v_ref[0, 1] = r[:, HEAD_DIM:].astype(BF16)
    z_ref[0] = proj(Z_OFF, SSM_INNER).astype(BF16)
    xbc_ref[0] = proj(XBC_OFF, SSM_XBC_W).astype(BF16)
    dt_ref[0] = proj(DT_OFF, LANES)
    fu_ref[0] = proj(FU_OFF, FNET_WIDTH).astype(BF16)
    su_ref[0] = proj(SU_OFF, 3 * SCONV_WIDTH).astype(BF16)


def _inproj(xflat, nb, seq, t, mod, mod_row, g, w, cos, sin):
    d = xflat.shape[1]
    nt = seq // t
    if mod_row is None:
        mod_map = lambda b, i: (b, 0, 0)
    else:
        mod_map = lambda b, i: (mod_row, 0, 0)
    sds = jax.ShapeDtypeStruct
    outs = (sds((nb, ATT_HEADS, seq, HEAD_DIM), BF16), sds((nb, ATT_KV_HEADS, seq, HEAD_DIM), BF16),
            sds((nb, ATT_KV_HEADS, seq, HEAD_DIM), BF16), sds((nb, seq, SSM_INNER), BF16),
            sds((nb, seq, SSM_XBC_W), BF16), sds((nb, seq, LANES), F32),
            sds((nb, seq, FNET_WIDTH), BF16), sds((nb, seq, 3 * SCONV_WIDTH), BF16))
    head_spec = lambda nh: pl.BlockSpec((1, nh, t, HEAD_DIM), lambda b, i: (b, 0, i, 0))
    row_spec = lambda wd: pl.BlockSpec((1, t, wd), lambda b, i: (b, i, 0))
    return pl.pallas_call(
        _inproj_kernel, grid=(nb, nt),
        in_specs=[pl.BlockSpec((t, d), lambda b, i: (b * nt + i, 0)),
                  pl.BlockSpec((1, 8, d), mod_map),
                  _const_spec((1, d)), _const_spec((d, IN_W_PAD)),
                  pl.BlockSpec((t, LANES), lambda b, i: (i, 0)),
                  pl.BlockSpec((t, LANES), lambda b, i: (i, 0))],
        out_specs=(head_spec(ATT_HEADS), head_spec(ATT_KV_HEADS), head_spec(ATT_KV_HEADS),
                   row_spec(SSM_INNER), row_spec(SSM_XBC_W), row_spec(LANES),
                   row_spec(FNET_WIDTH), row_spec(3 * SCONV_WIDTH)),
        out_shape=outs, compiler_params=_params("arbitrary", "arbitrary"), name="inproj",
    )(xflat, mod, g, w, cos, sin)


def _attn_kernel(sink_ref, q_ref, *refs, tq, seq, band):
    if band:
        bias_ref, kp_ref, kc_ref, kn_ref, vp_ref, vc_ref, vn_ref, kx_ref, vx_ref, o_ref = refs
    else:
        kx_ref, vx_ref, o_ref = refs
    kh = pl.program_id(1)
    if band:
        keys = jnp.concatenate([kp_ref[0, 0], kc_ref[0, 0], kn_ref[0, 0], kx_ref[0, 0]], axis=0)
        vals = jnp.concatenate([vp_ref[0, 0], vc_ref[0, 0], vn_ref[0, 0], vx_ref[0, 0]], axis=0)
        bias = bias_ref[0]
    else:
        keys, vals = kx_ref[0, 0], vx_ref[0, 0]
    ones_col = (lax.broadcasted_iota(I32, vals.shape, 1) == 0).astype(BF16)
    v_ext = jnp.concatenate([vals, ones_col], axis=1)
    outs = []
    for g in range(ATT_GROUP):
        s = _dot_nt(q_ref[0, g], keys)
        if band:
            s = s + bias
        sink = sink_ref[kh * ATT_GROUP + g]
        m = jnp.maximum(jnp.max(s, axis=1, keepdims=True), sink)
        acc = jnp.dot(jnp.exp(s - m).astype(BF16), v_ext, preferred_element_type=F32)
        den = acc[:, HEAD_DIM:HEAD_DIM + 1] + jnp.exp(sink - m)
        outs.append(acc[:, :HEAD_DIM] / den)
    o_ref[0] = jnp.concatenate(outs, axis=1).astype(BF16)


def _attention(sink, q, k, v, kx, vx, tq, band):
    nb, _, seq, _ = q.shape
    lc = kx.shape[2]
    nt = seq // tq
    r = tq // WINDOW
    nblk = seq // WINDOW
    qs = pl.BlockSpec((1, ATT_GROUP, tq, HEAD_DIM), lambda b, h, i: (b, h, i, 0))
    cur = pl.BlockSpec((1, 1, tq, HEAD_DIM), lambda b, h, i: (b, h, i, 0))
    prev = pl.BlockSpec((1, 1, WINDOW, HEAD_DIM), lambda b, h, i: (b, h, jnp.maximum(i * r - 1, 0), 0))
    nxt = pl.BlockSpec((1, 1, WINDOW, HEAD_DIM), lambda b, h, i: (b, h, jnp.minimum((i + 1) * r, nblk - 1), 0))
    ctx = pl.BlockSpec((1, 1, lc, HEAD_DIM), lambda b, h, i: (b, h, 0, 0))
    smem = pl.BlockSpec(memory_space=pltpu.SMEM)
    if band:
        nk = tq + 2 * WINDOW
        rel = np.arange(nk)[None, :] - WINDOW - np.arange(tq)[:, None]
        inside = np.abs(rel) <= WINDOW
        col = np.arange(nk)[None, :]
        kinds = [inside & ((col >= WINDOW) | (not first)) & ((col < tq + WINDOW) | (not lastt))
                 for lastt in (False, True) for first in (False, True)]
        kinds = np.concatenate([np.stack(kinds), np.ones((4, tq, lc), bool)], axis=2)
        bias = jnp.asarray(np.where(kinds, 0.0, -np.inf), F32)
        bias_spec = pl.BlockSpec((1, tq, nk + lc),
                                 lambda b, h, i: ((i == 0).astype(I32) + 2 * (i == nt - 1).astype(I32), 0, 0))
        in_specs = [smem, qs, bias_spec, prev, cur, nxt, prev, cur, nxt, ctx, ctx]
        args = (sink, q, bias, k, k, k, v, v, v, kx, vx)
    else:
        in_specs = [smem, qs, ctx, ctx]
        args = (sink, q, kx, vx)
    return pl.pallas_call(
        functools.partial(_attn_kernel, tq=tq, seq=seq, band=band),
        grid=(nb, ATT_KV_HEADS, nt), in_specs=in_specs,
        out_specs=pl.BlockSpec((1, tq, ATT_GROUP * HEAD_DIM), lambda b, h, i: (b, i, h)),
        out_shape=jax.ShapeDtypeStruct((nb, seq, ATT_HEADS * HEAD_DIM), BF16),
        compiler_params=_params("arbitrary", "arbitrary", "arbitrary"), name="attn_band" if band else "attn_ctx",
    )(*args)


def _ssd_kernel(xp_ref, xc_ref, xn_ref, dt_ref, cw_ref, cb_ref, dtb_ref, alog_ref, init_ref,
                *refs, reverse, with_y, finalize, nc):
    if finalize:
        yf_ref, z_ref, dsk_ref, nw_ref, y_ref, st_ref = refs
    elif with_y:
        y_ref, st_ref = refs
    else:
        (st_ref,) = refs
    q = SSM_CHUNK
    c = pl.program_id(1)
    ce = (nc - 1 - c) if reverse else c

    @pl.when(c == 0)
    def _():
        st_ref[...] = init_ref[...]

    xc = xc_ref[0].astype(F32)
    prow = jnp.where(ce == 0, 0.0, xp_ref[0].astype(F32)[BF16_ROWS - 1:BF16_ROWS])
    nrow = jnp.where(ce == nc - 1, 0.0, xn_ref[0].astype(F32)[0:1])
    rid = lax.broadcasted_iota(I32, xc.shape, 0)
    up = jnp.where(rid == 0, prow, pltpu.roll(xc, 1, 0))
    dn = jnp.where(rid == q - 1, nrow, pltpu.roll(xc, q - 1, 0))
    cw = cw_ref[...]
    act = _silu(cw[0:1] * up + cw[1:2] * xc + cw[2:3] * dn + cb_ref[...])
    xs = act[:, :SSM_INNER]
    bmat = act[:, SSM_INNER:SSM_INNER + SSM_BC_W]
    cmat = act[:, SSM_INNER + SSM_BC_W:]

    pre = dt_ref[0] + dtb_ref[...]
    dt = jnp.maximum(pre, 0.0) + jnp.log1p(jnp.exp(-jnp.abs(pre)))
    dta = dt * (-jnp.exp(alog_ref[...]))
    ri = lax.broadcasted_iota(I32, (q, q), 0)
    ci = lax.broadcasted_iota(I32, (q, q), 1)
    tri = (ri <= ci) if reverse else (ri >= ci)
    acs = _dot_split(tri.astype(BF16), dta)
    col0 = SSM_HEADS if reverse else 0
    lo_half = lax.broadcasted_iota(I32, (q, LANES), 1) < SSM_HEAD_DIM

    def bcast(mat, h):
        return jnp.broadcast_to(mat[:, col0 + h:col0 + h + 1], (q, LANES))

    def lane_expand(cols):
        return jnp.concatenate([jnp.where(lo_half, cols[2 * j], cols[2 * j + 1]) for j in range(SSM_HEADS // 2)],
                               axis=1)

    acs_b = [bcast(acs, h) for h in range(SSM_HEADS)]
    acs_e = lane_expand(acs_b)
    xd = xs * lane_expand([bcast(dt, h) for h in range(SSM_HEADS)])
    tot_e = acs_e[0:1] if reverse else acs_e[q - 1:q]
    hpg = SSM_HEADS // SSM_GROUPS
    gw = hpg * SSM_HEAD_DIM

    def grp(mat, g):
        return mat[:, g * SSM_STATE:(g + 1) * SSM_STATE]

    if with_y:
        acs_t = acs.T
        cb = [_dot_nt(grp(cmat, g), grp(bmat, g)) for g in range(SSM_GROUPS)]
        y_diag = []
        for j in range(SSM_HEADS // 2):
            sc = []
            for h in (2 * j, 2 * j + 1):
                seg = acs_b[h] - acs_t[col0 + h:col0 + h + 1, :]
                sc.append((cb[h // hpg] * jnp.exp(jnp.where(tri, seg, -jnp.inf))).astype(BF16))
            slab = xd[:, j * LANES:(j + 1) * LANES].astype(BF16)
            rhs = jnp.concatenate([jnp.where(lo_half, slab, 0), jnp.where(lo_half, 0, slab)], axis=0)
            y_diag.append(jnp.dot(jnp.concatenate(sc, axis=1), rhs, preferred_element_type=F32))
        y_off = [_dot(grp(cmat, g), st_ref[0, g]) for g in range(SSM_GROUPS)]
        y = jnp.concatenate(y_diag, axis=1) + jnp.concatenate(y_off, axis=1) * jnp.exp(acs_e)

    xdd = xd * jnp.exp(tot_e - acs_e)
    b_t = bmat.T
    for g in range(SSM_GROUPS):
        upd = _dot(b_t[g * SSM_STATE:(g + 1) * SSM_STATE, :], xdd[:, g * gw:(g + 1) * gw])
        st_ref[0, g] = st_ref[0, g] * jnp.exp(tot_e[:, g * gw:(g + 1) * gw]) + upd

    if finalize:
        yt = yf_ref[0] + y + dsk_ref[...] * xs
        yt = yt * _silu(z_ref[0].astype(F32))
        yt = yt * lax.rsqrt(jnp.mean(yt * yt, axis=-1, keepdims=True) + EPS) * nw_ref[...]
        y_ref[0] = yt.astype(BF16)
    elif with_y:
        y_ref[0] = y


def _ssd(xbc, dtr, cw, cb, dtb, alog, init, reverse, with_y, fin=None):
    nb, seq, _ = xbc.shape
    st_block = (1, SSM_GROUPS, SSM_STATE, SSM_INNER // SSM_GROUPS)
    nc = seq // SSM_CHUNK
    per = SSM_CHUNK // BF16_ROWS
    nhalo = seq // BF16_ROWS
    ce = (lambda c: nc - 1 - c) if reverse else (lambda c: c)
    xw = SSM_XBC_W
    in_specs = [pl.BlockSpec((1, BF16_ROWS, xw), lambda b, c: (b, jnp.maximum(ce(c) * per - 1, 0), 0)),
                pl.BlockSpec((1, SSM_CHUNK, xw), lambda b, c: (b, ce(c), 0)),
                pl.BlockSpec((1, BF16_ROWS, xw), lambda b, c: (b, jnp.minimum((ce(c) + 1) * per, nhalo - 1), 0)),
                pl.BlockSpec((1, SSM_CHUNK, LANES), lambda b, c: (b, ce(c), 0)),
                _const_spec((8, xw)), _const_spec((1, xw)), _const_spec((1, LANES)), _const_spec((1, LANES)),
                pl.BlockSpec(st_block, lambda b, c: (b, 0, 0, 0))]
    args = [xbc, xbc, xbc, dtr, cw, cb, dtb, alog, init]
    st_spec = pl.BlockSpec(st_block, lambda b, c: (b, 0, 0, 0))
    st_shape = jax.ShapeDtypeStruct((nb,) + st_block[1:], F32)
    y_spec = pl.BlockSpec((1, SSM_CHUNK, SSM_INNER), lambda b, c: (b, ce(c), 0))
    finalize = fin is not None
    if finalize:
        yf, z, dsk, nw = fin
        in_specs += [y_spec, y_spec, _const_spec((1, SSM_INNER)), _const_spec((1, SSM_INNER))]
        args += [yf, z, dsk, nw]
    if with_y:
        out_specs = (y_spec, st_spec)
        out_shape = (jax.ShapeDtypeStruct((nb, seq, SSM_INNER), BF16 if finalize else F32), st_shape)
    else:
        out_specs = (st_spec,)
        out_shape = (st_shape,)
    res = pl.pallas_call(
        functools.partial(_ssd_kernel, reverse=reverse, with_y=with_y, finalize=finalize, nc=nc),
        grid=(nb, nc), in_specs=in_specs, out_specs=out_specs, out_shape=out_shape,
        compiler_params=_params("arbitrary", "arbitrary"), name="ssd_bwd" if reverse else "ssd_fwd",
    )(*args)
    return res if with_y else (None, res[0])


def _fft_a_kernel(u_ref, wc_ref, m1_ref, y_ref, wc_bf):
    @pl.when((pl.program_id(0) == 0) & (pl.program_id(1) == 0))
    def _():
        wc_bf[...] = wc_ref[...].astype(BF16)

    for j in range(m1_ref.shape[0]):
        v = _dot(u_ref[0, :, j * FNET_WIDTH:(j + 1) * FNET_WIDTH], wc_bf[...])
        vs = jnp.concatenate([v[:, :FNET_WIDTH], v[:, FNET_WIDTH:]], axis=0)
        b = _dot(m1_ref[j], vs)
        y_ref[0, 0, j] = b[:FFT_N2].astype(BF16)
        y_ref[0, 1, j] = b[FFT_N2:].astype(BF16)


def _fft_c_kernel(y_ref, m2_ref, o_ref):
    o_ref[0] = _dot(m2_ref[...], y_ref[0]).astype(BF16)


def _fft_small_kernel(u_ref, wc_ref, m_ref, o_ref):
    v = _dot(u_ref[0], wc_ref[...])
    vs = jnp.concatenate([v[:, :FNET_WIDTH], v[:, FNET_WIDTH:]], axis=0)
    o_ref[0] = _dot(m_ref[...], vs).astype(BF16)


def _channel_dft():
    idx = np.arange(FNET_GROUP_DIM)
    ang = 2.0 * np.pi * np.outer(idx, idx) / FNET_GROUP_DIM
    eye = np.eye(FNET_GROUPS)
    return jnp.asarray(np.concatenate([np.kron(eye, np.cos(ang)), -np.kron(eye, np.sin(ang))], axis=1), F32)


def _fourier_long(u):
    nb, seq, w = u.shape
    n2 = FFT_N2
    n1 = seq // n2
    t2 = np.arange(n2)
    k2 = np.arange(n2)
    t1 = np.arange(n1)
    ang = 2.0 * np.pi * (np.outer(k2, t2)[None] / n2 + (t1[:, None, None] * k2[None, :, None]) / seq)
    co, si = np.cos(ang), np.sin(ang)
    m1 = jnp.asarray(np.concatenate([np.concatenate([co, si], axis=2),
                                     np.concatenate([-si, co], axis=2)], axis=1), F32)
    ang1 = 2.0 * np.pi * np.outer(t1, t1) / n1
    norm = 1.0 / math.sqrt(seq * FNET_GROUP_DIM)
    m2 = jnp.asarray(np.concatenate([np.cos(ang1), np.sin(ang1)], axis=1) * norm, F32)
    per = FFT_T1_PER_STEP
    y = pl.pallas_call(
        _fft_a_kernel, grid=(nb, n1 // per),
        in_specs=[pl.BlockSpec((1, n2, per * w), lambda b, j: (b, 0, j)),
                  _const_spec((w, 2 * w)),
                  pl.BlockSpec((per, 2 * n2, 2 * n2), lambda b, j: (j, 0, 0))],
        out_specs=pl.BlockSpec((1, 2, per, n2, w), lambda b, j: (b, 0, j, 0, 0)),
        out_shape=jax.ShapeDtypeStruct((nb, 2, n1, n2, w), BF16),
        scratch_shapes=[pltpu.VMEM((w, 2 * w), BF16)],
        compiler_params=_params("arbitrary", "arbitrary"), name="fft_a",
    )(u.reshape(nb, n2, n1 * w), _channel_dft(), m1)
    nl = 2048
    out = pl.pallas_call(
        _fft_c_kernel, grid=(nb, n2 * w // nl),
        in_specs=[pl.BlockSpec((1, 2 * n1, nl), lambda b, j: (b, 0, j)),
                  _const_spec((n1, 2 * n1))],
        out_specs=pl.BlockSpec((1, n1, nl), lambda b, j: (b, 0, j)),
        out_shape=jax.ShapeDtypeStruct((nb, n1, n2 * w), BF16),
        compiler_params=_params("arbitrary", "arbitrary"), name="fft_c",
    )(y.reshape(nb, 2 * n1, n2 * w), m2)
    return out.reshape(nb, seq, w)


def _fourier_short(u):
    nb, seq, w = u.shape
    t = np.arange(seq)
    ang = 2.0 * np.pi * np.outer(t, t) / seq
    norm = 1.0 / math.sqrt(seq * FNET_GROUP_DIM)
    m = jnp.asarray(np.concatenate([np.cos(ang), np.sin(ang)], axis=1) * norm, F32)
    return pl.pallas_call(
        _fft_small_kernel, grid=(nb,),
        in_specs=[pl.BlockSpec((1, seq, w), lambda b: (b, 0, 0)),
                  _const_spec((w, 2 * w)), _const_spec((seq, 2 * seq))],
        out_specs=pl.BlockSpec((1, seq, w), lambda b: (b, 0, 0)),
        out_shape=jax.ShapeDtypeStruct((nb, seq, w), BF16),
        compiler_params=_params("arbitrary"), name="fft_small",
    )(u, _channel_dft(), m)


def _merge_kernel(x_ref, mod_ref, g1_ref, g2_ref, att_ref, ssm_ref, fft_ref, sup_ref, su_ref, sun_ref,
                  sw_ref, wg_ref, bg_ref, wb_ref, wo_ref, wr_ref, rb_ref, cin_ref,
                  x1_ref, h2_ref, ri_ref, rw_ref, cnt_ref, *, nt):
    d = x_ref.shape[1]
    t = x_ref.shape[0]
    i = pl.program_id(1)

    @pl.when((pl.program_id(0) == 0) & (i == 0))
    def _():
        cnt_ref[...] = cin_ref[...]

    x = x_ref[...]
    m = mod_ref[0]
    hb = _norm_mod(x, g1_ref[...], m[0:1], m[1:2]).astype(BF16)

    def gated(su):
        return su[:, SCONV_WIDTH:2 * SCONV_WIDTH] * su[:, 2 * SCONV_WIDTH:]

    su = su_ref[0].astype(F32)
    p = gated(su)
    prow = jnp.where(i == 0, 0.0, gated(sup_ref[0].astype(F32)[BF16_ROWS - 1:BF16_ROWS]))
    nrow = jnp.where(i == nt - 1, 0.0, gated(sun_ref[0].astype(F32)[0:1]))
    rid = lax.broadcasted_iota(I32, p.shape, 0)
    up = jnp.where(rid == 0, prow, pltpu.roll(p, 1, 0))
    dn = jnp.where(rid == t - 1, nrow, pltpu.roll(p, t - 1, 0))
    sw = sw_ref[...]
    sconv = su[:, :SCONV_WIDTH] * (sw[0:1] * up + sw[1:2] * p + sw[2:3] * dn)

    branches = (att_ref[0], ssm_ref[0], fft_ref[0], sconv.astype(BF16))
    acc = jnp.zeros((t, d), F32)
    for n in range(N_BRANCHES):
        gate = _sigmoid(jnp.dot(hb, wg_ref[:, n * d:(n + 1) * d], preferred_element_type=F32)
                        + bg_ref[:, n * d:(n + 1) * d])
        acc = acc + gate * jnp.dot(branches[n], wb_ref[n], preferred_element_type=F32)
    y = jnp.dot(acc.astype(BF16), wo_ref[...], preferred_element_type=F32)
    x1 = x + m[2:3] * y
    x1_ref[...] = x1
    h2 = _norm_mod(x1, g2_ref[...], m[3:4], m[4:5])
    h2_ref[...] = h2
    _route_tile(h2, wr_ref, rb_ref, ri_ref, rw_ref, cnt_ref)


def _merge(xflat, nb, seq, t, mod, mod_row, g1, g2, att, ssm, fft, su, sw, wg, bg, wb, wo, wr, rb, cnt_in):
    ntok, d = xflat.shape
    nt = seq // t
    per = t // BF16_ROWS
    nhalo = seq // BF16_ROWS
    mod_map = (lambda b, i: (b, 0, 0)) if mod_row is None else (lambda b, i: (mod_row, 0, 0))
    row = pl.BlockSpec((t, d), lambda b, i: (b * nt + i, 0))
    br = pl.BlockSpec((1, t, BRANCH_W), lambda b, i: (b, i, 0))
    suw = 3 * SCONV_WIDTH
    in_specs = [row, pl.BlockSpec((1, 8, d), mod_map), _const_spec((1, d)), _const_spec((1, d)),
                br, br, br,
                pl.BlockSpec((1, BF16_ROWS, suw), lambda b, i: (b, jnp.maximum(i * per - 1, 0), 0)),
                pl.BlockSpec((1, t, suw), lambda b, i: (b, i, 0)),
                pl.BlockSpec((1, BF16_ROWS, suw), lambda b, i: (b, jnp.minimum((i + 1) * per, nhalo - 1), 0)),
                _const_spec((8, SCONV_WIDTH)), _const_spec((d, N_BRANCHES * d)), _const_spec((1, N_BRANCHES * d)),
                _const_spec((N_BRANCHES, BRANCH_W, d)), _const_spec((d, d)),
                _const_spec((d, EXPERTS_PER_GROUP * LANES)), _const_spec((1, EXPERTS_PER_GROUP * LANES)),
                _const_spec((8, LANES))]
    lane_row = pl.BlockSpec((t, LANES), lambda b, i: (b * nt + i, 0))
    return pl.pallas_call(
        functools.partial(_merge_kernel, nt=nt),
        grid=(nb, nt), in_specs=in_specs,
        out_specs=(row, row, lane_row, lane_row, pl.BlockSpec((8, LANES), lambda b, i: (0, 0))),
        out_shape=(jax.ShapeDtypeStruct((ntok, d), F32), jax.ShapeDtypeStruct((ntok, d), F32),
                   jax.ShapeDtypeStruct((ntok, LANES), I32), jax.ShapeDtypeStruct((ntok, LANES), F32),
                   jax.ShapeDtypeStruct((8, LANES), F32)),
        compiler_params=_params("arbitrary", "arbitrary"), name="merge",
    )(xflat, mod, g1, g2, att, ssm, fft, su, su, su, sw, wg, bg, wb, wo, wr, rb, cnt_in)


def _route_tile(h2, wr_ref, rb_ref, ri_ref, rw_ref, cnt_ref):
    t = h2.shape[0]
    sc = _sigmoid(_dot(h2, wr_ref[...]))
    sel = sc + rb_ref[...]
    s = [sel[:, j * LANES:(j + 1) * LANES] for j in range(EXPERTS_PER_GROUP)]
    u = [sc[:, j * LANES:(j + 1) * LANES] for j in range(EXPERTS_PER_GROUP)]
    gs = None
    for a in range(EXPERTS_PER_GROUP):
        for b in range(a + 1, EXPERTS_PER_GROUP):
            pair = s[a] + s[b]
            gs = pair if gs is None else jnp.maximum(gs, pair)
    lane = lax.broadcasted_iota(I32, (t, LANES), 1)
    gmax = jnp.max(gs, axis=1, keepdims=True)
    best = jnp.min(jnp.where(gs == gmax, lane, LANES), axis=1, keepdims=True)
    on = lane == best
    v = [jnp.sum(jnp.where(on, s[j], 0.0), axis=1, keepdims=True) for j in range(EXPERTS_PER_GROUP)]
    w = [jnp.sum(jnp.where(on, u[j], 0.0), axis=1, keepdims=True) for j in range(EXPERTS_PER_GROUP)]

    def first_argmax(vals):
        mx = vals[0]
        for x in vals[1:]:
            mx = jnp.maximum(mx, x)
        idx = jnp.full_like(best, len(vals) - 1)
        for j in range(len(vals) - 2, -1, -1):
            idx = jnp.where(vals[j] == mx, j, idx)
        return idx

    def pick(vals, idx):
        out = vals[-1]
        for j in range(len(vals) - 2, -1, -1):
            out = jnp.where(idx == j, vals[j], out)
        return out

    l1 = first_argmax(v)
    l2 = first_argmax([jnp.where(l1 == j, -jnp.inf, v[j]) for j in range(EXPERTS_PER_GROUP)])
    w1, w2 = pick(w, l1), pick(w, l2)
    tot = w1 + w2
    e1 = best * EXPERTS_PER_GROUP + l1
    e2 = best * EXPERTS_PER_GROUP + l2

    oh1 = (lane == e1).astype(F32)
    oh2 = (lane == e2).astype(F32)
    oh = oh1 + oh2
    ri_ = lax.broadcasted_iota(I32, (t, t), 0)
    ci_ = lax.broadcasted_iota(I32, (t, t), 1)
    before = _dot((ri_ > ci_).astype(F32), oh) + cnt_ref[0:1]
    r1 = jnp.sum(oh1 * before, axis=1, keepdims=True).astype(I32)
    r2 = jnp.sum(oh2 * before, axis=1, keepdims=True).astype(I32)
    cnt_ref[...] = cnt_ref[...] + jnp.sum(oh, axis=0, keepdims=True)

    ri_ref[...] = jnp.where(lane == 0, e1, jnp.where(lane == 1, e2, jnp.where(lane == 2, r1, jnp.where(lane == 3, r2, 0))))
    rw_ref[...] = jnp.where(lane == 0, w1 / tot, jnp.where(lane == 1, w2 / tot, 0.0))


def _dispatch_kernel(dest_ref, zt_ref, *refs, td, starts):
    n_streams = len(starts) - 1
    h_refs = refs[:n_streams]
    xs_hbm, zbuf, zsem, sem = refs[n_streams:]
    i = pl.program_id(0)

    @pl.when(i == 0)
    def _():
        zbuf[...] = jnp.zeros_like(zbuf)

        def zero_copy(j):
            start = pl.multiple_of(zt_ref[j] * MOE_TILE, MOE_TILE)
            return pltpu.make_async_copy(zbuf, xs_hbm.at[pl.ds(start, MOE_TILE)], zsem)

        def zissue(j, carry):
            @pl.when(zt_ref[j] >= 0)
            def _():
                zero_copy(j).start()
            return carry

        def zdrain(j, carry):
            @pl.when(zt_ref[j] >= 0)
            def _():
                zero_copy(j).wait()
            return carry

        lax.fori_loop(0, zt_ref.shape[0], zissue, 0)
        lax.fori_loop(0, zt_ref.shape[0], zdrain, 0)

    for s in range(n_streams):
        h_ref = h_refs[s]

        def row_copy(src_row, dst_row, h_ref=h_ref):
            return pltpu.make_async_copy(h_ref.at[pl.ds(src_row, 1)], xs_hbm.at[pl.ds(dst_row, 1)], sem)

        @pl.when((i >= starts[s]) & (i < starts[s + 1]))
        def _(row_copy=row_copy):
            def issue(j, carry):
                row_copy(j, dest_ref[0, 0, 2 * j]).start()
                row_copy(j, dest_ref[0, 0, 2 * j + 1]).start()
                return carry

            lax.fori_loop(0, td, issue, 0, unroll=ROW_DMA_UNROLL)
            all_rows = xs_hbm.at[pl.ds(0, 2 * td)]
            pltpu.make_async_copy(all_rows, all_rows, sem).wait()


def _dispatch(h2s, dest, ztiles, nrows):
    d = h2s[0].shape[1]
    td = 256
    starts = [0]
    for h in h2s:
        starts.append(starts[-1] + h.shape[0] // td)
    nt = starts[-1]

    def stream_spec(s):
        return pl.BlockSpec((td, d), lambda i: (jnp.clip(i - starts[s], 0, starts[s + 1] - starts[s] - 1), 0))

    return pl.pallas_call(
        functools.partial(_dispatch_kernel, td=td, starts=tuple(starts)), grid=(nt,),
        in_specs=[pl.BlockSpec((1, 1, 2 * td), lambda i: (i, 0, 0), memory_space=pltpu.SMEM),
                  pl.BlockSpec(memory_space=pltpu.SMEM)] + [stream_spec(s) for s in range(len(h2s))],
        out_specs=pl.BlockSpec(memory_space=pl.ANY),
        out_shape=jax.ShapeDtypeStruct((nrows, d), F32),
        scratch_shapes=[pltpu.VMEM((MOE_TILE, d), F32), pltpu.SemaphoreType.DMA(()), pltpu.SemaphoreType.DMA(())],
        compiler_params=_params("arbitrary"), name="dispatch",
    )(dest.reshape(nt, 1, 2 * td), ztiles, *h2s)


def _ffn_kernel(be_ref, nu_ref, x_ref, w1_ref, w3_ref, w2_ref, y_ref, w1_bf, w3_bf, w2_bf):
    i = pl.program_id(0)
    used = i < nu_ref[0]

    @pl.when(used & ((i == 0) | (be_ref[i] != be_ref[jnp.maximum(i - 1, 0)])))
    def _():
        w1_bf[...] = w1_ref[0, 0].astype(BF16)
        w3_bf[...] = w3_ref[0, 0].astype(BF16)
        w2_bf[...] = w2_ref[0, 0].astype(BF16)

    @pl.when(used)
    def _():
        xb = x_ref[...].astype(BF16)
        a = jnp.dot(xb, w1_bf[...], preferred_element_type=F32)
        b = jnp.dot(xb, w3_bf[...], preferred_element_type=F32)
        y_ref[...] = jnp.dot((_silu(a) * b).astype(BF16), w2_bf[...], preferred_element_type=F32)

    @pl.when(i >= nu_ref[0])
    def _():
        y_ref[...] = jnp.zeros_like(y_ref)


def _ffn(xs, block_expert, n_used, w1, w3, w2, layer):
    nrows, d = xs.shape
    de = w1.shape[3]
    tm = MOE_TILE
    grid_spec = pltpu.PrefetchScalarGridSpec(
        num_scalar_prefetch=2, grid=(nrows // tm,),
        in_specs=[pl.BlockSpec((tm, d), lambda i, be, nu: (jnp.minimum(i, nu[0] - 1), 0)),
                  pl.BlockSpec((1, 1, d, de), lambda i, be, nu: (layer, be[i], 0, 0)),
                  pl.BlockSpec((1, 1, d, de), lambda i, be, nu: (layer, be[i], 0, 0)),
                  pl.BlockSpec((1, 1, de, d), lambda i, be, nu: (layer, be[i], 0, 0))],
        out_specs=pl.BlockSpec((tm, d), lambda i, be, nu: (i, 0)),
        scratch_shapes=[pltpu.VMEM((d, de), BF16), pltpu.VMEM((d, de), BF16), pltpu.VMEM((de, d), BF16)])
    return pl.pallas_call(
        _ffn_kernel, grid_spec=grid_spec, out_shape=jax.ShapeDtypeStruct((nrows, d), F32),
        compiler_params=_params("arbitrary"), name="ffn",
    )(block_expert, n_used, xs, w1, w3, w2)


def _combine_kernel(dest_ref, dnext_ref, x_ref, mod_ref, rw_ref, fw_ref, y_hbm, o_ref, buf, sems, *, tc, final):
    i = pl.program_id(0)
    nt = pl.num_programs(0)
    slot = i % 2

    def gather(d_ref, s):
        def issue(j, carry):
            for k in range(2):
                pltpu.make_async_copy(y_hbm.at[pl.ds(d_ref[0, 0, 2 * j + k], 1)],
                                      buf.at[s, k, pl.ds(j, 1)], sems.at[s]).start()
            return carry
        lax.fori_loop(0, tc, issue, 0, unroll=ROW_DMA_UNROLL)

    @pl.when(i == 0)
    def _():
        gather(dest_ref, 0)

    @pl.when(i + 1 < nt)
    def _():
        gather(dnext_ref, 1 - slot)

    pltpu.make_async_copy(buf.at[slot], buf.at[slot], sems.at[slot]).wait()
    rw = rw_ref[...]
    f = rw[:, 0:1] * buf[slot, 0] + rw[:, 1:2] * buf[slot, 1]
    x = x_ref[...] + mod_ref[0][5:6] * f
    if final:
        x = x * lax.rsqrt(jnp.mean(x * x, axis=-1, keepdims=True) + EPS) * fw_ref[...]
    o_ref[...] = x


def _combine(x1, dest, rw, y, mod, mod_row, seq, fw, final):
    ntok, d = x1.shape
    tc = 256
    nt = ntok // tc
    per_seq = seq // tc
    mod_map = (lambda i: (i // per_seq, 0, 0)) if mod_row is None else (lambda i: (mod_row, 0, 0))
    return pl.pallas_call(
        functools.partial(_combine_kernel, tc=tc, final=final), grid=(nt,),
        in_specs=[pl.BlockSpec((1, 1, 2 * tc), lambda i: (i, 0, 0), memory_space=pltpu.SMEM),
                  pl.BlockSpec((1, 1, 2 * tc), lambda i: (jnp.minimum(i + 1, nt - 1), 0, 0), memory_space=pltpu.SMEM),
                  pl.BlockSpec((tc, d), lambda i: (i, 0)),
                  pl.BlockSpec((1, 8, d), mod_map),
                  pl.BlockSpec((tc, LANES), lambda i: (i, 0)),
                  _const_spec((1, d)),
                  pl.BlockSpec(memory_space=pl.ANY)],
        out_specs=pl.BlockSpec((tc, d), lambda i: (i, 0)),
        out_shape=jax.ShapeDtypeStruct((ntok, d), F32),
        scratch_shapes=[pltpu.VMEM((2, 2, tc, d), F32), pltpu.SemaphoreType.DMA((2,))],
        compiler_params=_params("arbitrary"), name="combine",
    )(dest.reshape(nt, 1, 2 * tc), dest.reshape(nt, 1, 2 * tc), x1, mod, rw, fw, y)


def _moe(streams, cnt, mod, w1, w3, w2, layer, fw, final):
    counts = cnt[0, :N_EXPERTS].astype(I32)
    padded = (counts + MOE_TILE - 1) // MOE_TILE * MOE_TILE
    pad_end = jnp.cumsum(padded)
    offs = pad_end - padded
    ntok = sum(s[0].shape[0] for s in streams)
    n_tiles = -(-2 * ntok // MOE_TILE) + N_EXPERTS
    tile_start = jnp.arange(n_tiles, dtype=I32) * MOE_TILE
    block_expert = jnp.minimum(jnp.sum((pad_end[None, :] <= tile_start[:, None]).astype(I32), axis=1), N_EXPERTS - 1)
    n_used = (pad_end[-1:] // MOE_TILE).astype(I32)
    dests = [(offs[s[2][:, 0:2]] + s[2][:, 2:4]).reshape(-1) for s in streams]
    tail = n_used + jnp.arange(N_EXPERTS, dtype=I32)
    ztiles = jnp.concatenate([jnp.where(padded > 0, pad_end // MOE_TILE - 1, -1),
                              jnp.where(tail < n_tiles, tail, -1)]).astype(I32)
    xs = _dispatch([s[1] for s in streams], jnp.concatenate(dests), ztiles, n_tiles * MOE_TILE)
    y = _ffn(xs, block_expert, n_used, w1, w3, w2, layer)
    return [_combine(x1, dest, rw, y, mod, mod_row, seq, fw, final)
            for (x1, _, _, rw, seq, mod_row), dest in zip(streams, dests)]


def _pad_rows(a, rows):
    return jnp.concatenate([a, jnp.zeros((rows - a.shape[0],) + a.shape[1:], a.dtype)], axis=0)


def _pad_cols(a, cols, value=0.0):
    return jnp.concatenate([a, jnp.full(a.shape[:-1] + (cols - a.shape[-1],), value, a.dtype)], axis=-1)


def _rope_tables(seq):
    rows = seq // GRID_W
    row = jnp.repeat(jnp.arange(rows, dtype=F32), GRID_W)
    col = jnp.tile(jnp.arange(GRID_W, dtype=F32), rows)
    pairs = HEAD_DIM // 4
    inv_freq = ROPE_THETA ** (-jnp.arange(pairs, dtype=F32) / pairs)
    ang = jnp.concatenate([row[:, None] * inv_freq, col[:, None] * inv_freq], axis=-1)
    cos, sin = jnp.cos(ang), jnp.sin(ang)
    cos_t = jnp.tile(cos, (1, LANES // (HEAD_DIM // 2)))
    sin_t = jnp.tile(jnp.concatenate([-sin, sin], axis=-1), (1, LANES // HEAD_DIM))
    return cos_t, sin_t


def kernel(x, c, ctx, c_ctx, norm1_w, norm2_w, w_ada, b_ada, w_in, attn_sink, ssm_conv_w, ssm_conv_b,
           ssm_dt_bias, ssm_a_log, ssm_d, ssm_norm_w, sconv_w, w_gate, b_gate, w_branch, w_o,
           w_router, router_bias, moe_w1, moe_w3, moe_w2, final_norm_w):
    nb, seq, d = x.shape
    lc = ctx.shape[1]
    depth = w_in.shape[0]
    n_lat = nb * seq
    n_ctx = nb * lc
    assert seq % 512 == 0 and seq % (FFT_N2 * 8) == 0 and lc % 256 == 0 and nb < 8

    xl = x.reshape(n_lat, d)
    xc = ctx.reshape(n_ctx, d)
    c8 = _pad_rows(jnp.concatenate([c, c_ctx[None]], axis=0), 8)
    mods = _ada(c8, w_ada, b_ada)
    mods = _pad_cols(mods.reshape(depth, 8, 6, d).swapaxes(-1, -2), 8).swapaxes(-1, -2)

    cos_l, sin_l = _rope_tables(seq)
    cos_c, sin_c = jnp.ones((lc, LANES), F32), jnp.zeros((lc, LANES), F32)

    wr = w_router.reshape(d, N_EXPERT_GROUPS, EXPERTS_PER_GROUP).transpose(0, 2, 1)
    wr = _pad_cols(wr, LANES).reshape(d, EXPERTS_PER_GROUP * LANES).astype(BF16)
    rb = _pad_cols(router_bias.reshape(N_EXPERT_GROUPS, EXPERTS_PER_GROUP).T, LANES, -1e30).reshape(1, -1)

    for layer in range(depth):
        last = layer + 1 == depth
        mod = mods[layer]
        wi = w_in[layer]
        w_pad = jnp.concatenate([wi[:, :2048], _pad_cols(wi[:, 2048:2048 + DT_W], LANES), wi[:, 2048 + DT_W:]],
                                axis=1).astype(BF16)
        g1 = norm1_w[layer][None]
        g2 = norm2_w[layer][None]
        cw = _pad_rows(ssm_conv_w[layer], 8)
        cb = ssm_conv_b[layer][None]
        dtb = _pad_cols(ssm_dt_bias[layer].reshape(1, DT_W), LANES)
        alog = _pad_cols(ssm_a_log[layer].reshape(1, DT_W), LANES)
        dsk = jnp.repeat(ssm_d[layer], SSM_HEAD_DIM)[None]
        nw = ssm_norm_w[layer][None]
        sw = _pad_rows(sconv_w[layer], 8)
        wg = w_gate[layer].astype(BF16)
        bg = b_gate[layer][None]
        wb = w_branch[layer].astype(BF16)
        wo = w_o[layer].astype(BF16)
        sink = attn_sink[layer]

        q, k, v, z, xbc, dtr, fu, su = _inproj(xl, nb, seq, 512, mod, None, g1, w_pad, cos_l, sin_l)
        qc, kc, vc, zc, xbcc, dtrc, fuc, suc = _inproj(xc, nb, lc, 256, mod, nb, g1, w_pad, cos_c, sin_c)

        att = _attention(sink, q, k, v, kc, vc, 256, True)

        zero_state = jnp.zeros((nb, SSM_GROUPS, SSM_STATE, SSM_INNER // SSM_GROUPS), F32)
        ssd_c = functools.partial(_ssd, xbcc, dtrc, cw, cb, dtb, alog)
        ssd_l = functools.partial(_ssd, xbc, dtr, cw, cb, dtb, alog)
        if last:
            _, fin_f = ssd_c(zero_state, False, False)
            _, fin_b = ssd_c(zero_state, True, False)
        else:
            yc_f, fin_f = ssd_c(zero_state, False, True)
            ssm_c, fin_b = ssd_c(zero_state, True, True, fin=(yc_f, zc, dsk, nw))
        y_f, _ = ssd_l(fin_f, False, True)
        ssm, _ = ssd_l(fin_b, True, True, fin=(y_f, z, dsk, nw))

        fft = _fourier_long(fu)

        x1, h2, ri, rw, cnt = _merge(xl, nb, seq, 256, mod, None, g1, g2, att, ssm, fft, su, sw, wg, bg, wb, wo,
                                     wr, rb, jnp.zeros((8, LANES), F32))
        streams = [(x1, h2, ri, rw, seq, None)]
        if not last:
            att_c = _attention(sink, qc, None, None, kc, vc, lc, False)
            fft_c = _fourier_short(fuc)
            x1c, h2c, ric, rwc, cnt = _merge(xc, nb, lc, 256, mod, nb, g1, g2, att_c, ssm_c, fft_c, suc, sw, wg,
                                             bg, wb, wo, wr, rb, cnt)
            streams.append((x1c, h2c, ric, rwc, lc, nb))
        new = _moe(streams, cnt, mod, moe_w1, moe_w3, moe_w2, layer, final_norm_w[None], last)
        xl = new[0]
        if not last:
            xc = new[1]
    return xl.reshape(nb, seq, d)
```

```python
import functools
import math

import numpy as np
import jax
import jax.numpy as jnp
from jax import lax
from jax.experimental import pallas as pl
from jax.experimental.pallas import tpu as pltpu

F32 = jnp.float32
BF16 = jnp.bfloat16
I32 = jnp.int32

EPS = 1e-6
GRID_W = 64
ROPE_THETA = 10000.0
HEAD_DIM = 64
ATT_HEADS = 8
ATT_KV_HEADS = 2
ATT_GROUP = ATT_HEADS // ATT_KV_HEADS
WINDOW = 128
SSM_HEADS = 8
SSM_HEAD_DIM = 64
SSM_INNER = SSM_HEADS * SSM_HEAD_DIM
SSM_GROUPS = 2
SSM_STATE = 64
SSM_BC_W = SSM_GROUPS * SSM_STATE
SSM_XBC_W = SSM_INNER + 2 * SSM_BC_W
SSM_CHUNK = 128
FNET_GROUPS = 8
FNET_GROUP_DIM = 64
FNET_WIDTH = FNET_GROUPS * FNET_GROUP_DIM
SCONV_WIDTH = 512
BRANCH_W = 512
N_BRANCHES = 4
N_EXPERTS = 32
N_EXPERT_GROUPS = 8
EXPERTS_PER_GROUP = N_EXPERTS // N_EXPERT_GROUPS
D_EXPERT = 512

LANES = 128
BF16_ROWS = 16
VMEM_LIMIT = 56 * 1024 * 1024
FFT_N2 = 128
F32_ROWS = 8
SSD_CHUNKS_PER_STEP = 2
MOE_TILE = 256
ROW_DMA_UNROLL = 8

Q_OFF, K_OFF, V_OFF, Z_OFF, XBC_OFF, DT_OFF, FU_OFF, SU_OFF = 0, 512, 640, 768, 1280, 2048, 2176, 2688
IN_W_PAD = SU_OFF + 3 * SCONV_WIDTH
DT_W = 2 * SSM_HEADS


def _dot(a, b):
    return jnp.dot(a.astype(BF16), b.astype(BF16), preferred_element_type=F32)


def _dot_nt(a, b):
    return lax.dot_general(a.astype(BF16), b.astype(BF16), (((1,), (1,)), ((), ())),
                           preferred_element_type=F32)


def _dot_split(a_bf16, x):
    hi = x.astype(BF16)
    rest = x - hi.astype(F32)
    mid = rest.astype(BF16)
    lo = (rest - mid.astype(F32)).astype(BF16)
    return sum(jnp.dot(a_bf16, t, preferred_element_type=F32) for t in (hi, mid, lo))


def _sigmoid(x):
    return 1.0 / (1.0 + jnp.exp(-x))


def _silu(x):
    return x * _sigmoid(x)


def _norm_mod(x, g, shift, scale):
    y = x * lax.rsqrt(jnp.mean(x * x, axis=-1, keepdims=True) + EPS) * g
    return y * (1.0 + scale) + shift


def _params(*sem):
    return pltpu.CompilerParams(dimension_semantics=sem, vmem_limit_bytes=VMEM_LIMIT)


def _const_spec(shape):
    n = len(shape)
    return pl.BlockSpec(shape, lambda *_: (0,) * n, pipeline_mode=pl.Buffered(1))


def _ada_kernel(c_ref, w_ref, b_ref, o_ref):
    o_ref[0] = _dot(_silu(c_ref[...]), w_ref[0]) + b_ref[0]


def _ada(c8, w_ada, b_ada):
    nl, d, w = w_ada.shape
    tn = 512
    return pl.pallas_call(
        _ada_kernel, grid=(nl, w // tn),
        in_specs=[pl.BlockSpec((8, d), lambda l, j: (0, 0)),
                  pl.BlockSpec((1, d, tn), lambda l, j: (l, 0, j)),
                  pl.BlockSpec((1, 1, tn), lambda l, j: (l, 0, j))],
        out_specs=pl.BlockSpec((1, 8, tn), lambda l, j: (l, 0, j)),
        out_shape=jax.ShapeDtypeStruct((nl, 8, w), F32),
        compiler_params=_params("arbitrary", "arbitrary"), name="ada",
    )(c8, w_ada, b_ada.reshape(nl, 1, w))


def _inproj_kernel(x_ref, mod_ref, g_ref, w_ref, cos_ref, sin_ref,
                   q_ref, k_ref, v_ref, z_ref, xbc_ref, dt_ref, fu_ref, su_ref):
    m = mod_ref[0]
    hb = _norm_mod(x_ref[...], g_ref[...], m[0:1], m[1:2]).astype(BF16)
    cos = cos_ref[...]
    sin = sin_ref[...]
    lane = lax.broadcasted_iota(I32, cos.shape, 1)
    first_half = (lane % HEAD_DIM) < HEAD_DIM // 2

    def proj(start, width):
        return jnp.dot(hb, w_ref[:, start:start + width], preferred_element_type=F32)

    def rope(r):
        rot = jnp.where(first_half, pltpu.roll(r, LANES - HEAD_DIM // 2, 1), pltpu.roll(r, HEAD_DIM // 2, 1))
        return r * cos + rot * sin

    scale = HEAD_DIM ** -0.5
    for j in range(ATT_HEADS // 2):
        r = rope(proj(Q_OFF + j * LANES, LANES)) * scale
        q_ref[0, 2 * j] = r[:, :HEAD_DIM].astype(BF16)
        q_ref[0, 2 * j + 1] = r[:, HEAD_DIM:].astype(BF16)
    r = rope(proj(K_OFF, LANES))
    k_ref[0, 0] = r[:, :HEAD_DIM].astype(BF16)
    k_ref[0, 1] = r[:, HEAD_DIM:].astype(BF16)
    r = proj(V_OFF, LANES)
    v_ref[0, 0] = r[:, :HEAD_DIM].astype(BF16)
    v_ref[0, 1] = r[:, HEAD_DIM:].astype(BF16)
    z_ref[0] = proj(Z_OFF, SSM_INNER).astype(BF16)
    xbc_ref[0] = proj(XBC_OFF, SSM_XBC_W).astype(BF16)
    dt_ref[0] = proj(DT_OFF, LANES)
    fu_ref[0] = proj(FU_OFF, FNET_WIDTH)
    su_ref[0] = proj(SU_OFF, 3 * SCONV_WIDTH).astype(BF16)


def _inproj(xflat, nb, seq, t, mod, mod_row, g, w, cos, sin):
    d = xflat.shape[1]
    nt = seq // t
    if mod_row is None:
        mod_map = lambda b, i: (b, 0, 0)
    else:
        mod_map = lambda b, i: (mod_row, 0, 0)
    sds = jax.ShapeDtypeStruct
    outs = (sds((nb, ATT_HEADS, seq, HEAD_DIM), BF16), sds((nb, ATT_KV_HEADS, seq, HEAD_DIM), BF16),
            sds((nb, ATT_KV_HEADS, seq, HEAD_DIM), BF16), sds((nb, seq, SSM_INNER), BF16),
            sds((nb, seq, SSM_XBC_W), BF16), sds((nb, seq, LANES), F32),
            sds((nb, seq, FNET_WIDTH), F32), sds((nb, seq, 3 * SCONV_WIDTH), BF16))
    head_spec = lambda nh: pl.BlockSpec((1, nh, t, HEAD_DIM), lambda b, i: (b, 0, i, 0))
    row_spec = lambda wd: pl.BlockSpec((1, t, wd), lambda b, i: (b, i, 0))
    return pl.pallas_call(
        _inproj_kernel, grid=(nb, nt),
        in_specs=[pl.BlockSpec((t, d), lambda b, i: (b * nt + i, 0)),
                  pl.BlockSpec((1, 8, d), mod_map),
                  _const_spec((1, d)), _const_spec((d, IN_W_PAD)),
                  pl.BlockSpec((t, LANES), lambda b, i: (i, 0)),
                  pl.BlockSpec((t, LANES), lambda b, i: (i, 0))],
        out_specs=(head_spec(ATT_HEADS), head_spec(ATT_KV_HEADS), head_spec(ATT_KV_HEADS),
                   row_spec(SSM_INNER), row_spec(SSM_XBC_W), row_spec(LANES),
                   row_spec(FNET_WIDTH), row_spec(3 * SCONV_WIDTH)),
        out_shape=outs, compiler_params=_params("arbitrary", "arbitrary"), name="inproj",
    )(xflat, mod, g, w, cos, sin)


def _attn_kernel(sink_ref, q_ref, *refs, tq, seq, band):
    if band:
        bias_ref, kp_ref, kc_ref, kn_ref, vp_ref, vc_ref, vn_ref, kx_ref, vx_ref, o_ref = refs
    else:
        kx_ref, vx_ref, o_ref = refs
    kh = pl.program_id(1)
    if band:
        keys = jnp.concatenate([kp_ref[0, 0], kc_ref[0, 0], kn_ref[0, 0], kx_ref[0, 0]], axis=0)
        vals = jnp.concatenate([vp_ref[0, 0], vc_ref[0, 0], vn_ref[0, 0], vx_ref[0, 0]], axis=0)
        bias = bias_ref[0]
    else:
        keys, vals = kx_ref[0, 0], vx_ref[0, 0]
    ones_col = (lax.broadcasted_iota(I32, vals.shape, 1) == 0).astype(BF16)
    v_ext = jnp.concatenate([vals, ones_col], axis=1)
    outs = []
    for g in range(ATT_GROUP):
        s = _dot_nt(q_ref[0, g], keys)
        if band:
            s = s + bias
        sink = sink_ref[kh * ATT_GROUP + g]
        m = jnp.maximum(jnp.max(s, axis=1, keepdims=True), sink)
        acc = jnp.dot(jnp.exp(s - m).astype(BF16), v_ext, preferred_element_type=F32)
        den = acc[:, HEAD_DIM:HEAD_DIM + 1] + jnp.exp(sink - m)
        outs.append(acc[:, :HEAD_DIM] / den)
    o_ref[0] = jnp.concatenate(outs, axis=1).astype(BF16)


def _attention(sink, q, k, v, kx, vx, tq, band):
    nb, _, seq, _ = q.shape
    lc = kx.shape[2]
    nt = seq // tq
    r = tq // WINDOW
    nblk = seq // WINDOW
    qs = pl.BlockSpec((1, ATT_GROUP, tq, HEAD_DIM), lambda b, h, i: (b, h, i, 0))
    cur = pl.BlockSpec((1, 1, tq, HEAD_DIM), lambda b, h, i: (b, h, i, 0))
    prev = pl.BlockSpec((1, 1, WINDOW, HEAD_DIM), lambda b, h, i: (b, h, jnp.maximum(i * r - 1, 0), 0))
    nxt = pl.BlockSpec((1, 1, WINDOW, HEAD_DIM), lambda b, h, i: (b, h, jnp.minimum((i + 1) * r, nblk - 1), 0))
    ctx = pl.BlockSpec((1, 1, lc, HEAD_DIM), lambda b, h, i: (b, h, 0, 0))
    smem = pl.BlockSpec(memory_space=pltpu.SMEM)
    if band:
        nk = tq + 2 * WINDOW
        rel = np.arange(nk)[None, :] - WINDOW - np.arange(tq)[:, None]
        inside = np.abs(rel) <= WINDOW
        col = np.arange(nk)[None, :]
        kinds = [inside & ((col >= WINDOW) | (not first)) & ((col < tq + WINDOW) | (not lastt))
                 for lastt in (False, True) for first in (False, True)]
        kinds = np.concatenate([np.stack(kinds), np.ones((4, tq, lc), bool)], axis=2)
        bias = jnp.asarray(np.where(kinds, 0.0, -np.inf), F32)
        bias_spec = pl.BlockSpec((1, tq, nk + lc),
                                 lambda b, h, i: ((i == 0).astype(I32) + 2 * (i == nt - 1).astype(I32), 0, 0))
        in_specs = [smem, qs, bias_spec, prev, cur, nxt, prev, cur, nxt, ctx, ctx]
        args = (sink, q, bias, k, k, k, v, v, v, kx, vx)
    else:
        in_specs = [smem, qs, ctx, ctx]
        args = (sink, q, kx, vx)
    return pl.pallas_call(
        functools.partial(_attn_kernel, tq=tq, seq=seq, band=band),
        grid=(nb, ATT_KV_HEADS, nt), in_specs=in_specs,
        out_specs=pl.BlockSpec((1, tq, ATT_GROUP * HEAD_DIM), lambda b, h, i: (b, i, h)),
        out_shape=jax.ShapeDtypeStruct((nb, seq, ATT_HEADS * HEAD_DIM), BF16),
        compiler_params=_params("arbitrary", "arbitrary", "arbitrary"), name="attn_band" if band else "attn_ctx",
    )(*args)


def _ssd_kernel(xp_ref, xc_ref, xn_ref, dt_ref, cw_ref, cb_ref, dtb_ref, alog_ref, init_ref,
                *refs, reverse, with_y, finalize, nc):
    if finalize:
        yf_ref, z_ref, dsk_ref, nw_ref, y_ref, st_ref = refs
    elif with_y:
        y_ref, st_ref = refs
    else:
        (st_ref,) = refs
    q = SSM_CHUNK
    rows = xc_ref.shape[1]
    c = pl.program_id(1)
    ce = (nc - 1 - c) if reverse else c

    @pl.when(c == 0)
    def _():
        st_ref[...] = init_ref[...]

    xc = xc_ref[0].astype(F32)
    prow = jnp.where(ce == 0, 0.0, xp_ref[0].astype(F32)[BF16_ROWS - 1:BF16_ROWS])
    nrow = jnp.where(ce == nc - 1, 0.0, xn_ref[0].astype(F32)[0:1])
    rid = lax.broadcasted_iota(I32, xc.shape, 0)
    up = jnp.where(rid == 0, prow, pltpu.roll(xc, 1, 0))
    dn = jnp.where(rid == rows - 1, nrow, pltpu.roll(xc, rows - 1, 0))
    cw = cw_ref[...]
    act_all = _silu(cw[0:1] * up + cw[1:2] * xc + cw[2:3] * dn + cb_ref[...])
    pre = dt_ref[0] + dtb_ref[...]
    dt_all = jnp.maximum(pre, 0.0) + jnp.log1p(jnp.exp(-jnp.abs(pre)))
    neg_a = -jnp.exp(alog_ref[...])

    subs = range(rows // q)
    for sub in (reversed(subs) if reverse else subs):
        r0 = sub * q
        _ssd_chunk(act_all[r0:r0 + q], dt_all[r0:r0 + q], neg_a, st_ref,
                   (yf_ref[0, r0:r0 + q], z_ref[0, r0:r0 + q], dsk_ref[...], nw_ref[...]) if finalize else None,
                   y_ref.at[0, r0:r0 + q] if with_y else None, reverse)


def _ssd_chunk(act, dt, neg_a, st_ref, fin, y_out, reverse):
    q = SSM_CHUNK
    with_y = y_out is not None
    xs = act[:, :SSM_INNER]
    bmat = act[:, SSM_INNER:SSM_INNER + SSM_BC_W]
    cmat = act[:, SSM_INNER + SSM_BC_W:]
    dta = dt * neg_a
    ri = lax.broadcasted_iota(I32, (q, q), 0)
    ci = lax.broadcasted_iota(I32, (q, q), 1)
    tri = (ri <= ci) if reverse else (ri >= ci)
    acs = _dot_split(tri.astype(BF16), dta)
    col0 = SSM_HEADS if reverse else 0
    lo_half = lax.broadcasted_iota(I32, (q, LANES), 1) < SSM_HEAD_DIM

    def bcast(mat, h):
        return jnp.broadcast_to(mat[:, col0 + h:col0 + h + 1], (q, LANES))

    def lane_expand(cols):
        return jnp.concatenate([jnp.where(lo_half, cols[2 * j], cols[2 * j + 1]) for j in range(SSM_HEADS // 2)],
                               axis=1)

    acs_b = [bcast(acs, h) for h in range(SSM_HEADS)]
    acs_e = lane_expand(acs_b)
    xd = xs * lane_expand([bcast(dt, h) for h in range(SSM_HEADS)])
    tot_e = acs_e[0:1] if reverse else acs_e[q - 1:q]
    hpg = SSM_HEADS // SSM_GROUPS
    gw = hpg * SSM_HEAD_DIM

    def grp(mat, g):
        return mat[:, g * SSM_STATE:(g + 1) * SSM_STATE]

    if with_y:
        acs_t = acs.T
        cb = [_dot_nt(grp(cmat, g), grp(bmat, g)) for g in range(SSM_GROUPS)]
        y_diag = []
        for j in range(SSM_HEADS // 2):
            sc = []
            for h in (2 * j, 2 * j + 1):
                seg = acs_b[h] - acs_t[col0 + h:col0 + h + 1, :]
                sc.append((cb[h // hpg] * jnp.exp(jnp.where(tri, seg, -jnp.inf))).astype(BF16))
            slab = xd[:, j * LANES:(j + 1) * LANES].astype(BF16)
            rhs = jnp.concatenate([jnp.where(lo_half, slab, 0), jnp.where(lo_half, 0, slab)], axis=0)
            y_diag.append(jnp.dot(jnp.concatenate(sc, axis=1), rhs, preferred_element_type=F32))
        y_off = [_dot(grp(cmat, g), st_ref[0, g]) for g in range(SSM_GROUPS)]
        y = jnp.concatenate(y_diag, axis=1) + jnp.concatenate(y_off, axis=1) * jnp.exp(acs_e)

    xdd = xd * jnp.exp(tot_e - acs_e)
    b_t = bmat.T
    for g in range(SSM_GROUPS):
        upd = _dot(b_t[g * SSM_STATE:(g + 1) * SSM_STATE, :], xdd[:, g * gw:(g + 1) * gw])
        st_ref[0, g] = st_ref[0, g] * jnp.exp(tot_e[:, g * gw:(g + 1) * gw]) + upd

    if fin is not None:
        y_fwd, z, dsk, nw = fin
        yt = y_fwd + y + dsk * xs
        yt = yt * _silu(z.astype(F32))
        yt = yt * lax.rsqrt(jnp.mean(yt * yt, axis=-1, keepdims=True) + EPS) * nw
        y_out[...] = yt.astype(BF16)
    elif with_y:
        y_out[...] = y


def _ssd(xbc, dtr, cw, cb, dtb, alog, init, reverse, with_y, fin=None):
    nb, seq, _ = xbc.shape
    st_block = (1, SSM_GROUPS, SSM_STATE, SSM_INNER // SSM_GROUPS)
    blk = SSM_CHUNK * SSD_CHUNKS_PER_STEP
    nc = seq // blk
    per = blk // BF16_ROWS
    nhalo = seq // BF16_ROWS
    ce = (lambda c: nc - 1 - c) if reverse else (lambda c: c)
    xw = SSM_XBC_W
    in_specs = [pl.BlockSpec((1, BF16_ROWS, xw), lambda b, c: (b, jnp.maximum(ce(c) * per - 1, 0), 0)),
                pl.BlockSpec((1, blk, xw), lambda b, c: (b, ce(c), 0)),
                pl.BlockSpec((1, BF16_ROWS, xw), lambda b, c: (b, jnp.minimum((ce(c) + 1) * per, nhalo - 1), 0)),
                pl.BlockSpec((1, blk, LANES), lambda b, c: (b, ce(c), 0)),
                _const_spec((8, xw)), _const_spec((1, xw)), _const_spec((1, LANES)), _const_spec((1, LANES)),
                pl.BlockSpec(st_block, lambda b, c: (b, 0, 0, 0))]
    args = [xbc, xbc, xbc, dtr, cw, cb, dtb, alog, init]
    st_spec = pl.BlockSpec(st_block, lambda b, c: (b, 0, 0, 0))
    st_shape = jax.ShapeDtypeStruct((nb,) + st_block[1:], F32)
    y_spec = pl.BlockSpec((1, blk, SSM_INNER), lambda b, c: (b, ce(c), 0))
    finalize = fin is not None
    if finalize:
        yf, z, dsk, nw = fin
        in_specs += [y_spec, y_spec, _const_spec((1, SSM_INNER)), _const_spec((1, SSM_INNER))]
        args += [yf, z, dsk, nw]
    if with_y:
        out_specs = (y_spec, st_spec)
        out_shape = (jax.ShapeDtypeStruct((nb, seq, SSM_INNER), BF16 if finalize else F32), st_shape)
    else:
        out_specs = (st_spec,)
        out_shape = (st_shape,)
    res = pl.pallas_call(
        functools.partial(_ssd_kernel, reverse=reverse, with_y=with_y, finalize=finalize, nc=nc),
        grid=(nb, nc), in_specs=in_specs, out_specs=out_specs, out_shape=out_shape,
        compiler_params=_params("arbitrary", "arbitrary"), name="ssd_bwd" if reverse else "ssd_fwd",
    )(*args)
    return res if with_y else (None, res[0])


def _fft_a_kernel(u_ref, wc_ref, m1_ref, y_ref, wc_bf):
    @pl.when((pl.program_id(0) == 0) & (pl.program_id(1) == 0))
    def _():
        wc_bf[...] = wc_ref[...].astype(BF16)

    for j in range(m1_ref.shape[0]):
        v = _dot(u_ref[0, :, j, :], wc_bf[...])
        vs = jnp.concatenate([v[:, :FNET_WIDTH], v[:, FNET_WIDTH:]], axis=0)
        b = _dot(m1_ref[j], vs)
        y_ref[0, 0, j] = b[:FFT_N2]
        y_ref[0, 1, j] = b[FFT_N2:]


def _fft_c_kernel(y_ref, m2_ref, o_ref, m2_bf):
    @pl.when((pl.program_id(0) == 0) & (pl.program_id(1) == 0))
    def _():
        m2_bf[...] = m2_ref[...].astype(BF16)

    rows = y_ref.shape[1] * y_ref.shape[2] * y_ref.shape[4]
    y2d = y_ref[0].reshape(rows, FNET_WIDTH).astype(BF16)
    o_ref[0] = jnp.dot(m2_bf[...], y2d, preferred_element_type=F32).reshape(o_ref.shape[1:])


def _fft_small_kernel(u_ref, wc_ref, m_ref, o_ref):
    v = _dot(u_ref[0], wc_ref[...])
    vs = jnp.concatenate([v[:, :FNET_WIDTH], v[:, FNET_WIDTH:]], axis=0)
    o_ref[0] = _dot(m_ref[...], vs).astype(BF16)


def _channel_dft():
    idx = np.arange(FNET_GROUP_DIM)
    ang = 2.0 * np.pi * np.outer(idx, idx) / FNET_GROUP_DIM
    eye = np.eye(FNET_GROUPS)
    return jnp.asarray(np.concatenate([np.kron(eye, np.cos(ang)), -np.kron(eye, np.sin(ang))], axis=1), F32)


def _fourier_long(u):
    nb, seq, w = u.shape
    n2 = FFT_N2
    n1 = seq // n2
    t2 = np.arange(n2)
    k2 = np.arange(n2)
    t1 = np.arange(n1)
    ang = 2.0 * np.pi * (np.outer(k2, t2)[None] / n2 + (t1[:, None, None] * k2[None, :, None]) / seq)
    co, si = np.cos(ang), np.sin(ang)
    m1 = jnp.asarray(np.concatenate([np.concatenate([co, si], axis=2),
                                     np.concatenate([-si, co], axis=2)], axis=1), F32)
    ang1 = 2.0 * np.pi * np.outer(t1, t1) / n1
    norm = 1.0 / math.sqrt(seq * FNET_GROUP_DIM)
    m2 = np.concatenate([np.cos(ang1), np.sin(ang1)], axis=1) * norm
    m2k = jnp.asarray(np.kron(m2, np.eye(F32_ROWS)), F32)
    per = F32_ROWS
    y = pl.pallas_call(
        _fft_a_kernel, grid=(nb, n1 // per),
        in_specs=[pl.BlockSpec((1, n2, per, w), lambda b, j: (b, 0, j, 0)),
                  _const_spec((w, 2 * w)),
                  pl.BlockSpec((per, 2 * n2, 2 * n2), lambda b, j: (j, 0, 0))],
        out_specs=pl.BlockSpec((1, 2, per, n2, w), lambda b, j: (b, 0, j, 0, 0)),
        out_shape=jax.ShapeDtypeStruct((nb, 2, n1, n2, w), F32),
        scratch_shapes=[pltpu.VMEM((w, 2 * w), BF16)],
        compiler_params=_params("arbitrary", "arbitrary"), name="fft_a",
    )(u.reshape(nb, n2, n1, w), _channel_dft(), m1)
    out = pl.pallas_call(
        _fft_c_kernel, grid=(nb, n2 // per),
        in_specs=[pl.BlockSpec((1, 2, n1, 1, per, w), lambda b, j: (b, 0, 0, j, 0, 0)),
                  _const_spec((n1 * per, 2 * n1 * per))],
        out_specs=pl.BlockSpec((1, n1, 1, per, w), lambda b, j: (b, 0, j, 0, 0)),
        out_shape=jax.ShapeDtypeStruct((nb, n1, n2 // per, per, w), F32),
        scratch_shapes=[pltpu.VMEM((n1 * per, 2 * n1 * per), BF16)],
        compiler_params=_params("arbitrary", "arbitrary"), name="fft_c",
    )(y.reshape(nb, 2, n1, n2 // per, per, w), m2k)
    return out.reshape(nb, seq, w)


def _fourier_short(u):
    nb, seq, w = u.shape
    t = np.arange(seq)
    ang = 2.0 * np.pi * np.outer(t, t) / seq
    norm = 1.0 / math.sqrt(seq * FNET_GROUP_DIM)
    m = jnp.asarray(np.concatenate([np.cos(ang), np.sin(ang)], axis=1) * norm, F32)
    return pl.pallas_call(
        _fft_small_kernel, grid=(nb,),
        in_specs=[pl.BlockSpec((1, seq, w), lambda b: (b, 0, 0)),
                  _const_spec((w, 2 * w)), _const_spec((seq, 2 * seq))],
        out_specs=pl.BlockSpec((1, seq, w), lambda b: (b, 0, 0)),
        out_shape=jax.ShapeDtypeStruct((nb, seq, w), BF16),
        compiler_params=_params("arbitrary"), name="fft_small",
    )(u, _channel_dft(), m)


def _merge_kernel(x_ref, mod_ref, g1_ref, g2_ref, att_ref, ssm_ref, fft_ref, sup_ref, su_ref, sun_ref,
                  sw_ref, wg_ref, bg_ref, wb_ref, wo_ref, wr_ref, rb_ref, cin_ref,
                  x1_ref, h2_ref, ri_ref, rw_ref, cnt_ref, *, nt):
    d = x_ref.shape[1]
    t = x_ref.shape[0]
    i = pl.program_id(1)

    @pl.when((pl.program_id(0) == 0) & (i == 0))
    def _():
        cnt_ref[...] = cin_ref[...]

    x = x_ref[...]
    m = mod_ref[0]
    hb = _norm_mod(x, g1_ref[...], m[0:1], m[1:2]).astype(BF16)

    def gated(su):
        return su[:, SCONV_WIDTH:2 * SCONV_WIDTH] * su[:, 2 * SCONV_WIDTH:]

    su = su_ref[0].astype(F32)
    p = gated(su)
    prow = jnp.where(i == 0, 0.0, gated(sup_ref[0].astype(F32)[BF16_ROWS - 1:BF16_ROWS]))
    nrow = jnp.where(i == nt - 1, 0.0, gated(sun_ref[0].astype(F32)[0:1]))
    rid = lax.broadcasted_iota(I32, p.shape, 0)
    up = jnp.where(rid == 0, prow, pltpu.roll(p, 1, 0))
    dn = jnp.where(rid == t - 1, nrow, pltpu.roll(p, t - 1, 0))
    sw = sw_ref[...]
    sconv = su[:, :SCONV_WIDTH] * (sw[0:1] * up + sw[1:2] * p + sw[2:3] * dn)

    branches = (att_ref[0], ssm_ref[0], fft_ref[0].astype(BF16), sconv.astype(BF16))
    acc = jnp.zeros((t, d), F32)
    for n in range(N_BRANCHES):
        gate = _sigmoid(jnp.dot(hb, wg_ref[:, n * d:(n + 1) * d], preferred_element_type=F32)
                        + bg_ref[:, n * d:(n + 1) * d])
        acc = acc + gate * jnp.dot(branches[n], wb_ref[n], preferred_element_type=F32)
    y = jnp.dot(acc.astype(BF16), wo_ref[...], preferred_element_type=F32)
    x1 = x + m[2:3] * y
    x1_ref[...] = x1
    h2 = _norm_mod(x1, g2_ref[...], m[3:4], m[4:5])
    h2_ref[...] = h2
    _route_tile(h2, wr_ref, rb_ref, ri_ref, rw_ref, cnt_ref)


def _merge(xflat, nb, seq, t, mod, mod_row, g1, g2, att, ssm, fft, su, sw, wg, bg, wb, wo, wr, rb, cnt_in):
    ntok, d = xflat.shape
    nt = seq // t
    per = t // BF16_ROWS
    nhalo = seq // BF16_ROWS
    mod_map = (lambda b, i: (b, 0, 0)) if mod_row is None else (lambda b, i: (mod_row, 0, 0))
    row = pl.BlockSpec((t, d), lambda b, i: (b * nt + i, 0))
    br = pl.BlockSpec((1, t, BRANCH_W), lambda b, i: (b, i, 0))
    suw = 3 * SCONV_WIDTH
    in_specs = [row, pl.BlockSpec((1, 8, d), mod_map), _const_spec((1, d)), _const_spec((1, d)),
                br, br, br,
                pl.BlockSpec((1, BF16_ROWS, suw), lambda b, i: (b, jnp.maximum(i * per - 1, 0), 0)),
                pl.BlockSpec((1, t, suw), lambda b, i: (b, i, 0)),
                pl.BlockSpec((1, BF16_ROWS, suw), lambda b, i: (b, jnp.minimum((i + 1) * per, nhalo - 1), 0)),
                _const_spec((8, SCONV_WIDTH)), _const_spec((d, N_BRANCHES * d)), _const_spec((1, N_BRANCHES * d)),
                _const_spec((N_BRANCHES, BRANCH_W, d)), _const_spec((d, d)),
                _const_spec((d, EXPERTS_PER_GROUP * LANES)), _const_spec((1, EXPERTS_PER_GROUP * LANES)),
                _const_spec((8, LANES))]
    lane_row = pl.BlockSpec((t, LANES), lambda b, i: (b * nt + i, 0))
    return pl.pallas_call(
        functools.partial(_merge_kernel, nt=nt),
        grid=(nb, nt), in_specs=in_specs,
        out_specs=(row, row, lane_row, lane_row, pl.BlockSpec((8, LANES), lambda b, i: (0, 0))),
        out_shape=(jax.ShapeDtypeStruct((ntok, d), F32), jax.ShapeDtypeStruct((ntok, d), F32),
                   jax.ShapeDtypeStruct((ntok, LANES), I32), jax.ShapeDtypeStruct((ntok, LANES), F32),
                   jax.ShapeDtypeStruct((8, LANES), F32)),
        compiler_params=_params("arbitrary", "arbitrary"), name="merge",
    )(xflat, mod, g1, g2, att, ssm, fft, su, su, su, sw, wg, bg, wb, wo, wr, rb, cnt_in)


def _route_tile(h2, wr_ref, rb_ref, ri_ref, rw_ref, cnt_ref):
    t = h2.shape[0]
    sc = _sigmoid(_dot(h2, wr_ref[...]))
    sel = sc + rb_ref[...]
    s = [sel[:, j * LANES:(j + 1) * LANES] for j in range(EXPERTS_PER_GROUP)]
    u = [sc[:, j * LANES:(j + 1) * LANES] for j in range(EXPERTS_PER_GROUP)]
    gs = None
    for a in range(EXPERTS_PER_GROUP):
        for b in range(a + 1, EXPERTS_PER_GROUP):
            pair = s[a] + s[b]
            gs = pair if gs is None else jnp.maximum(gs, pair)
    lane = lax.broadcasted_iota(I32, (t, LANES), 1)
    gmax = jnp.max(gs, axis=1, keepdims=True)
    best = jnp.min(jnp.where(gs == gmax, lane, LANES), axis=1, keepdims=True)
    on = lane == best
    v = [jnp.sum(jnp.where(on, s[j], 0.0), axis=1, keepdims=True) for j in range(EXPERTS_PER_GROUP)]
    w = [jnp.sum(jnp.where(on, u[j], 0.0), axis=1, keepdims=True) for j in range(EXPERTS_PER_GROUP)]

    def first_argmax(vals):
        mx = vals[0]
        for x in vals[1:]:
            mx = jnp.maximum(mx, x)
        idx = jnp.full_like(best, len(vals) - 1)
        for j in range(len(vals) - 2, -1, -1):
            idx = jnp.where(vals[j] == mx, j, idx)
        return idx

    def pick(vals, idx):
        out = vals[-1]
        for j in range(len(vals) - 2, -1, -1):
            out = jnp.where(idx == j, vals[j], out)
        return out

    l1 = first_argmax(v)
    l2 = first_argmax([jnp.where(l1 == j, -jnp.inf, v[j]) for j in range(EXPERTS_PER_GROUP)])
    w1, w2 = pick(w, l1), pick(w, l2)
    tot = w1 + w2
    e1 = best * EXPERTS_PER_GROUP + l1
    e2 = best * EXPERTS_PER_GROUP + l2

    oh1 = (lane == e1).astype(F32)
    oh2 = (lane == e2).astype(F32)
    oh = oh1 + oh2
    ri_ = lax.broadcasted_iota(I32, (t, t), 0)
    ci_ = lax.broadcasted_iota(I32, (t, t), 1)
    before = _dot((ri_ > ci_).astype(F32), oh) + cnt_ref[0:1]
    r1 = jnp.sum(oh1 * before, axis=1, keepdims=True).astype(I32)
    r2 = jnp.sum(oh2 * before, axis=1, keepdims=True).astype(I32)
    cnt_ref[...] = cnt_ref[...] + jnp.sum(oh, axis=0, keepdims=True)

    ri_ref[...] = jnp.where(lane == 0, e1, jnp.where(lane == 1, e2, jnp.where(lane == 2, r1, jnp.where(lane == 3, r2, 0))))
    rw_ref[...] = jnp.where(lane == 0, w1 / tot, jnp.where(lane == 1, w2 / tot, 0.0))


def _dispatch_kernel(dest_ref, zt_ref, *refs, td, starts):
    n_streams = len(starts) - 1
    h_refs = refs[:n_streams]
    xs_hbm, zbuf, zsem, sem = refs[n_streams:]
    i = pl.program_id(0)

    @pl.when(i == 0)
    def _():
        zbuf[...] = jnp.zeros_like(zbuf)

        def zero_copy(j):
            start = pl.multiple_of(zt_ref[j] * MOE_TILE, MOE_TILE)
            return pltpu.make_async_copy(zbuf, xs_hbm.at[pl.ds(start, MOE_TILE)], zsem)

        def zissue(j, carry):
            @pl.when(zt_ref[j] >= 0)
            def _():
                zero_copy(j).start()
            return carry

        def zdrain(j, carry):
            @pl.when(zt_ref[j] >= 0)
            def _():
                zero_copy(j).wait()
            return carry

        lax.fori_loop(0, zt_ref.shape[0], zissue, 0)
        lax.fori_loop(0, zt_ref.shape[0], zdrain, 0)

    for s in range(n_streams):
        h_ref = h_refs[s]

        def row_copy(src_row, dst_row, h_ref=h_ref):
            return pltpu.make_async_copy(h_ref.at[pl.ds(src_row, 1)], xs_hbm.at[pl.ds(dst_row, 1)], sem)

        @pl.when((i >= starts[s]) & (i < starts[s + 1]))
        def _(row_copy=row_copy):
            def issue(j, carry):
                row_copy(j, dest_ref[0, 0, 2 * j]).start()
                row_copy(j, dest_ref[0, 0, 2 * j + 1]).start()
                return carry

            lax.fori_loop(0, td, issue, 0, unroll=ROW_DMA_UNROLL)
            all_rows = xs_hbm.at[pl.ds(0, 2 * td)]
            pltpu.make_async_copy(all_rows, all_rows, sem).wait()


def _dispatch(h2s, dest, ztiles, nrows):
    d = h2s[0].shape[1]
    td = 256
    starts = [0]
    for h in h2s:
        starts.append(starts[-1] + h.shape[0] // td)
    nt = starts[-1]

    def stream_spec(s):
        return pl.BlockSpec((td, d), lambda i: (jnp.clip(i - starts[s], 0, starts[s + 1] - starts[s] - 1), 0))

    return pl.pallas_call(
        functools.partial(_dispatch_kernel, td=td, starts=tuple(starts)), grid=(nt,),
        in_specs=[pl.BlockSpec((1, 1, 2 * td), lambda i: (i, 0, 0), memory_space=pltpu.SMEM),
                  pl.BlockSpec(memory_space=pltpu.SMEM)] + [stream_spec(s) for s in range(len(h2s))],
        out_specs=pl.BlockSpec(memory_space=pl.ANY),
        out_shape=jax.ShapeDtypeStruct((nrows, d), F32),
        scratch_shapes=[pltpu.VMEM((MOE_TILE, d), F32), pltpu.SemaphoreType.DMA(()), pltpu.SemaphoreType.DMA(())],
        compiler_params=_params("arbitrary"), name="dispatch",
    )(dest.reshape(nt, 1, 2 * td), ztiles, *h2s)


def _ffn_kernel(be_ref, nu_ref, x_ref, w1_ref, w3_ref, w2_ref, y_ref, w1_bf, w3_bf, w2_bf):
    i = pl.program_id(0)
    used = i < nu_ref[0]

    @pl.when(used & ((i == 0) | (be_ref[i] != be_ref[jnp.maximum(i - 1, 0)])))
    def _():
        w1_bf[...] = w1_ref[0, 0].astype(BF16)
        w3_bf[...] = w3_ref[0, 0].astype(BF16)
        w2_bf[...] = w2_ref[0, 0].astype(BF16)

    @pl.when(used)
    def _():
        xb = x_ref[...].astype(BF16)
        a = jnp.dot(xb, w1_bf[...], preferred_element_type=F32)
        b = jnp.dot(xb, w3_bf[...], preferred_element_type=F32)
        y_ref[...] = jnp.dot((_silu(a) * b).astype(BF16), w2_bf[...], preferred_element_type=F32)

    @pl.when(i >= nu_ref[0])
    def _():
        y_ref[...] = jnp.zeros_like(y_ref)


def _ffn(xs, block_expert, n_used, w1, w3, w2, layer):
    nrows, d = xs.shape
    de = w1.shape[3]
    tm = MOE_TILE
    grid_spec = pltpu.PrefetchScalarGridSpec(
        num_scalar_prefetch=2, grid=(nrows // tm,),
        in_specs=[pl.BlockSpec((tm, d), lambda i, be, nu: (jnp.minimum(i, nu[0] - 1), 0)),
                  pl.BlockSpec((1, 1, d, de), lambda i, be, nu: (layer, be[i], 0, 0)),
                  pl.BlockSpec((1, 1, d, de), lambda i, be, nu: (layer, be[i], 0, 0)),
                  pl.BlockSpec((1, 1, de, d), lambda i, be, nu: (layer, be[i], 0, 0))],
        out_specs=pl.BlockSpec((tm, d), lambda i, be, nu: (i, 0)),
        scratch_shapes=[pltpu.VMEM((d, de), BF16), pltpu.VMEM((d, de), BF16), pltpu.VMEM((de, d), BF16)])
    return pl.pallas_call(
        _ffn_kernel, grid_spec=grid_spec, out_shape=jax.ShapeDtypeStruct((nrows, d), F32),
        compiler_params=_params("arbitrary"), name="ffn",
    )(block_expert, n_used, xs, w1, w3, w2)


def _combine_kernel(dest_ref, dnext_ref, x_ref, mod_ref, rw_ref, fw_ref, y_hbm, o_ref, buf, sems, *, tc, final):
    i = pl.program_id(0)
    nt = pl.num_programs(0)
    slot = i % 2

    def gather(d_ref, s):
        def issue(j, carry):
            for k in range(2):
                pltpu.make_async_copy(y_hbm.at[pl.ds(d_ref[0, 0, 2 * j + k], 1)],
                                      buf.at[s, k, pl.ds(j, 1)], sems.at[s]).start()
            return carry
        lax.fori_loop(0, tc, issue, 0, unroll=ROW_DMA_UNROLL)

    @pl.when(i == 0)
    def _():
        gather(dest_ref, 0)

    @pl.when(i + 1 < nt)
    def _():
        gather(dnext_ref, 1 - slot)

    pltpu.make_async_copy(buf.at[slot], buf.at[slot], sems.at[slot]).wait()
    rw = rw_ref[...]
    f = rw[:, 0:1] * buf[slot, 0] + rw[:, 1:2] * buf[slot, 1]
    x = x_ref[...] + mod_ref[0][5:6] * f
    if final:
        x = x * lax.rsqrt(jnp.mean(x * x, axis=-1, keepdims=True) + EPS) * fw_ref[...]
    o_ref[...] = x


def _combine(x1, dest, rw, y, mod, mod_row, seq, fw, final):
    ntok, d = x1.shape
    tc = 256
    nt = ntok // tc
    per_seq = seq // tc
    mod_map = (lambda i: (i // per_seq, 0, 0)) if mod_row is None else (lambda i: (mod_row, 0, 0))
    return pl.pallas_call(
        functools.partial(_combine_kernel, tc=tc, final=final), grid=(nt,),
        in_specs=[pl.BlockSpec((1, 1, 2 * tc), lambda i: (i, 0, 0), memory_space=pltpu.SMEM),
                  pl.BlockSpec((1, 1, 2 * tc), lambda i: (jnp.minimum(i + 1, nt - 1), 0, 0), memory_space=pltpu.SMEM),
                  pl.BlockSpec((tc, d), lambda i: (i, 0)),
                  pl.BlockSpec((1, 8, d), mod_map),
                  pl.BlockSpec((tc, LANES), lambda i: (i, 0)),
                  _const_spec((1, d)),
                  pl.BlockSpec(memory_space=pl.ANY)],
        out_specs=pl.BlockSpec((tc, d), lambda i: (i, 0)),
        out_shape=jax.ShapeDtypeStruct((ntok, d), F32),
        scratch_shapes=[pltpu.VMEM((2, 2, tc, d), F32), pltpu.SemaphoreType.DMA((2,))],
        compiler_params=_params("arbitrary"), name="combine",
    )(dest.reshape(nt, 1, 2 * tc), dest.reshape(nt, 1, 2 * tc), x1, mod, rw, fw, y)


def _moe(streams, cnt, mod, w1, w3, w2, layer, fw, final):
    counts = cnt[0, :N_EXPERTS].astype(I32)
    padded = (counts + MOE_TILE - 1) // MOE_TILE * MOE_TILE
    pad_end = jnp.cumsum(padded)
    offs = pad_end - padded
    ntok = sum(s[0].shape[0] for s in streams)
    n_tiles = -(-2 * ntok // MOE_TILE) + N_EXPERTS
    tile_start = jnp.arange(n_tiles, dtype=I32) * MOE_TILE
    block_expert = jnp.minimum(jnp.sum((pad_end[None, :] <= tile_start[:, None]).astype(I32), axis=1), N_EXPERTS - 1)
    n_used = (pad_end[-1:] // MOE_TILE).astype(I32)
    dests = [(offs[s[2][:, 0:2]] + s[2][:, 2:4]).reshape(-1) for s in streams]
    tail = n_used + jnp.arange(N_EXPERTS, dtype=I32)
    ztiles = jnp.concatenate([jnp.where(padded > 0, pad_end // MOE_TILE - 1, -1),
                              jnp.where(tail < n_tiles, tail, -1)]).astype(I32)
    xs = _dispatch([s[1] for s in streams], jnp.concatenate(dests), ztiles, n_tiles * MOE_TILE)
    y = _ffn(xs, block_expert, n_used, w1, w3, w2, layer)
    return [_combine(x1, dest, rw, y, mod, mod_row, seq, fw, final)
            for (x1, _, _, rw, seq, mod_row), dest in zip(streams, dests)]


def _pad_rows(a, rows):
    return jnp.concatenate([a, jnp.zeros((rows - a.shape[0],) + a.shape[1:], a.dtype)], axis=0)


def _pad_cols(a, cols, value=0.0):
    return jnp.concatenate([a, jnp.full(a.shape[:-1] + (cols - a.shape[-1],), value, a.dtype)], axis=-1)


def _rope_tables(seq):
    rows = seq // GRID_W
    row = jnp.repeat(jnp.arange(rows, dtype=F32), GRID_W)
    col = jnp.tile(jnp.arange(GRID_W, dtype=F32), rows)
    pairs = HEAD_DIM // 4
    inv_freq = ROPE_THETA ** (-jnp.arange(pairs, dtype=F32) / pairs)
    ang = jnp.concatenate([row[:, None] * inv_freq, col[:, None] * inv_freq], axis=-1)
    cos, sin = jnp.cos(ang), jnp.sin(ang)
    cos_t = jnp.tile(cos, (1, LANES // (HEAD_DIM // 2)))
    sin_t = jnp.tile(jnp.concatenate([-sin, sin], axis=-1), (1, LANES // HEAD_DIM))
    return cos_t, sin_t


def kernel(x, c, ctx, c_ctx, norm1_w, norm2_w, w_ada, b_ada, w_in, attn_sink, ssm_conv_w, ssm_conv_b,
           ssm_dt_bias, ssm_a_log, ssm_d, ssm_norm_w, sconv_w, w_gate, b_gate, w_branch, w_o,
           w_router, router_bias, moe_w1, moe_w3, moe_w2, final_norm_w):
    nb, seq, d = x.shape
    lc = ctx.shape[1]
    depth = w_in.shape[0]
    n_lat = nb * seq
    n_ctx = nb * lc
    assert seq % 512 == 0 and seq % (FFT_N2 * 8) == 0 and lc % 256 == 0 and nb < 8

    xl = x.reshape(n_lat, d)
    xc = ctx.reshape(n_ctx, d)
    c8 = _pad_rows(jnp.concatenate([c, c_ctx[None]], axis=0), 8)
    mods = _ada(c8, w_ada, b_ada)
    mods = _pad_cols(mods.reshape(depth, 8, 6, d).swapaxes(-1, -2), 8).swapaxes(-1, -2)

    cos_l, sin_l = _rope_tables(seq)
    cos_c, sin_c = jnp.ones((lc, LANES), F32), jnp.zeros((lc, LANES), F32)

    wr = w_router.reshape(d, N_EXPERT_GROUPS, EXPERTS_PER_GROUP).transpose(0, 2, 1)
    wr = _pad_cols(wr, LANES).reshape(d, EXPERTS_PER_GROUP * LANES).astype(BF16)
    rb = _pad_cols(router_bias.reshape(N_EXPERT_GROUPS, EXPERTS_PER_GROUP).T, LANES, -1e30).reshape(1, -1)

    for layer in range(depth):
        last = layer + 1 == depth
        mod = mods[layer]
        wi = w_in[layer]
        w_pad = jnp.concatenate([wi[:, :2048], _pad_cols(wi[:, 2048:2048 + DT_W], LANES), wi[:, 2048 + DT_W:]],
                                axis=1).astype(BF16)
        g1 = norm1_w[layer][None]
        g2 = norm2_w[layer][None]
        cw = _pad_rows(ssm_conv_w[layer], 8)
        cb = ssm_conv_b[layer][None]
        dtb = _pad_cols(ssm_dt_bias[layer].reshape(1, DT_W), LANES)
        alog = _pad_cols(ssm_a_log[layer].reshape(1, DT_W), LANES)
        dsk = jnp.repeat(ssm_d[layer], SSM_HEAD_DIM)[None]
        nw = ssm_norm_w[layer][None]
        sw = _pad_rows(sconv_w[layer], 8)
        wg = w_gate[layer].astype(BF16)
        bg = b_gate[layer][None]
        wb = w_branch[layer].astype(BF16)
        wo = w_o[layer].astype(BF16)
        sink = attn_sink[layer]

        q, k, v, z, xbc, dtr, fu, su = _inproj(xl, nb, seq, 512, mod, None, g1, w_pad, cos_l, sin_l)
        qc, kc, vc, zc, xbcc, dtrc, fuc, suc = _inproj(xc, nb, lc, 256, mod, nb, g1, w_pad, cos_c, sin_c)

        att = _attention(sink, q, k, v, kc, vc, 256, True)

        zero_state = jnp.zeros((nb, SSM_GROUPS, SSM_STATE, SSM_INNER // SSM_GROUPS), F32)
        ssd_c = functools.partial(_ssd, xbcc, dtrc, cw, cb, dtb, alog)
        ssd_l = functools.partial(_ssd, xbc, dtr, cw, cb, dtb, alog)
        if last:
            _, fin_f = ssd_c(zero_state, False, False)
            _, fin_b = ssd_c(zero_state, True, False)
        else:
            yc_f, fin_f = ssd_c(zero_state, False, True)
            ssm_c, fin_b = ssd_c(zero_state, True, True, fin=(yc_f, zc, dsk, nw))
        y_f, _ = ssd_l(fin_f, False, True)
        ssm, _ = ssd_l(fin_b, True, True, fin=(y_f, z, dsk, nw))

        fft = _fourier_long(fu)

        x1, h2, ri, rw, cnt = _merge(xl, nb, seq, 256, mod, None, g1, g2, att, ssm, fft, su, sw, wg, bg, wb, wo,
                                     wr, rb, jnp.zeros((8, LANES), F32))
        streams = [(x1, h2, ri, rw, seq, None)]
        if not last:
            att_c = _attention(sink, qc, None, None, kc, vc, lc, False)
            fft_c = _fourier_short(fuc)
            x1c, h2c, ric, rwc, cnt = _merge(xc, nb, lc, 256, mod, nb, g1, g2, att_c, ssm_c, fft_c, suc, sw, wg,
                                             bg, wb, wo, wr, rb, cnt)
            streams.append((x1c, h2c, ric, rwc, lc, nb))
        new = _moe(streams, cnt, mod, moe_w1, moe_w3, moe_w2, layer, final_norm_w[None], last)
        xl = new[0]
        if not last:
            xc = new[1]
    return xl.reshape(nb, seq, d)
```

```python
import functools
import math

import numpy as np
import jax
import jax.numpy as jnp
from jax import lax
from jax.experimental import pallas as pl
from jax.experimental.pallas import tpu as pltpu

F32 = jnp.float32
BF16 = jnp.bfloat16
I32 = jnp.int32

EPS = 1e-6
GRID_W = 64
ROPE_THETA = 10000.0
HEAD_DIM = 64
ATT_HEADS = 8
ATT_KV_HEADS = 2
ATT_GROUP = ATT_HEADS // ATT_KV_HEADS
WINDOW = 128
SSM_HEADS = 8
SSM_HEAD_DIM = 64
SSM_INNER = SSM_HEADS * SSM_HEAD_DIM
SSM_GROUPS = 2
SSM_STATE = 64
SSM_BC_W = SSM_GROUPS * SSM_STATE
SSM_XBC_W = SSM_INNER + 2 * SSM_BC_W
SSM_CHUNK = 128
FNET_GROUPS = 8
FNET_GROUP_DIM = 64
FNET_WIDTH = FNET_GROUPS * FNET_GROUP_DIM
SCONV_WIDTH = 512
BRANCH_W = 512
N_BRANCHES = 4
N_EXPERTS = 32
N_EXPERT_GROUPS = 8
EXPERTS_PER_GROUP = N_EXPERTS // N_EXPERT_GROUPS
D_EXPERT = 512

LANES = 128
BF16_ROWS = 16
VMEM_LIMIT = 56 * 1024 * 1024
FFT_N2 = 128
F32_ROWS = 8
SSD_CHUNKS_PER_STEP = 2
MERGE_ROW_GROUP = 256
MOE_TILE = 256
ROW_DMA_UNROLL = 8

Q_OFF, K_OFF, V_OFF, Z_OFF, XBC_OFF, DT_OFF, FU_OFF, SU_OFF = 0, 512, 640, 768, 1280, 2048, 2176, 2688
IN_W_PAD = SU_OFF + 3 * SCONV_WIDTH
DT_W = 2 * SSM_HEADS


def _dot(a, b):
    return jnp.dot(a.astype(BF16), b.astype(BF16), preferred_element_type=F32)


def _dot_nt(a, b):
    return lax.dot_general(a.astype(BF16), b.astype(BF16), (((1,), (1,)), ((), ())),
                           preferred_element_type=F32)


def _dot_split(a_bf16, x):
    hi = x.astype(BF16)
    rest = x - hi.astype(F32)
    mid = rest.astype(BF16)
    lo = (rest - mid.astype(F32)).astype(BF16)
    return sum(jnp.dot(a_bf16, t, preferred_element_type=F32) for t in (hi, mid, lo))


def _sigmoid(x):
    return 1.0 / (1.0 + jnp.exp(-x))


def _silu(x):
    return x * _sigmoid(x)


def _norm_mod(x, g, shift, scale):
    y = x * lax.rsqrt(jnp.mean(x * x, axis=-1, keepdims=True) + EPS) * g
    return y * (1.0 + scale) + shift


def _params(*sem):
    return pltpu.CompilerParams(dimension_semantics=sem, vmem_limit_bytes=VMEM_LIMIT)


def _const_spec(shape):
    n = len(shape)
    return pl.BlockSpec(shape, lambda *_: (0,) * n, pipeline_mode=pl.Buffered(1))


def _ada_kernel(c_ref, w_ref, b_ref, o_ref):
    o_ref[0] = _dot(_silu(c_ref[...]), w_ref[0]) + b_ref[0]


def _ada(c8, w_ada, b_ada):
    nl, d, w = w_ada.shape
    tn = 512
    return pl.pallas_call(
        _ada_kernel, grid=(nl, w // tn),
        in_specs=[pl.BlockSpec((8, d), lambda l, j: (0, 0)),
                  pl.BlockSpec((1, d, tn), lambda l, j: (l, 0, j)),
                  pl.BlockSpec((1, 1, tn), lambda l, j: (l, 0, j))],
        out_specs=pl.BlockSpec((1, 8, tn), lambda l, j: (l, 0, j)),
        out_shape=jax.ShapeDtypeStruct((nl, 8, w), F32),
        compiler_params=_params("arbitrary", "arbitrary"), name="ada",
    )(c8, w_ada, b_ada.reshape(nl, 1, w))


def _inproj_kernel(x_ref, mod_ref, g_ref, w_ref, cos_ref, sin_ref,
                   q_ref, k_ref, v_ref, z_ref, xbc_ref, dt_ref, fu_ref, su_ref):
    m = mod_ref[0]
    hb = _norm_mod(x_ref[...], g_ref[...], m[0:1], m[1:2]).astype(BF16)
    cos = cos_ref[...]
    sin = sin_ref[...]
    lane = lax.broadcasted_iota(I32, cos.shape, 1)
    first_half = (lane % HEAD_DIM) < HEAD_DIM // 2

    def proj(start, width):
        return jnp.dot(hb, w_ref[:, start:start + width], preferred_element_type=F32)

    def rope(r):
        rot = jnp.where(first_half, pltpu.roll(r, LANES - HEAD_DIM // 2, 1), pltpu.roll(r, HEAD_DIM // 2, 1))
        return r * cos + rot * sin

    scale = HEAD_DIM ** -0.5
    for j in range(ATT_HEADS // 2):
        r = rope(proj(Q_OFF + j * LANES, LANES)) * scale
        q_ref[0, 2 * j] = r[:, :HEAD_DIM].astype(BF16)
        q_ref[0, 2 * j + 1] = r[:, HEAD_DIM:].astype(BF16)
    r = rope(proj(K_OFF, LANES))
    k_ref[0, 0] = r[:, :HEAD_DIM].astype(BF16)
    k_ref[0, 1] = r[:, HEAD_DIM:].astype(BF16)
    r = proj(V_OFF, LANES)
    v_ref[0, 0] = r[:, :HEAD_DIM].astype(BF16)
    v_ref[0, 1] = r[:, HEAD_DIM:].astype(BF16)
    z_ref[0] = proj(Z_OFF, SSM_INNER).astype(BF16)
    xbc_ref[0] = proj(XBC_OFF, SSM_XBC_W).astype(BF16)
    dt_ref[0] = proj(DT_OFF, LANES)
    fu_ref[0] = proj(FU_OFF, FNET_WIDTH)
    su_ref[0] = proj(SU_OFF, 3 * SCONV_WIDTH).astype(BF16)


def _inproj(xflat, nb, seq, t, mod, mod_row, g, w, cos, sin):
    d = xflat.shape[1]
    nt = seq // t
    if mod_row is None:
        mod_map = lambda b, i: (b, 0, 0)
    else:
        mod_map = lambda b, i: (mod_row, 0, 0)
    sds = jax.ShapeDtypeStruct
    outs = (sds((nb, ATT_HEADS, seq, HEAD_DIM), BF16), sds((nb, ATT_KV_HEADS, seq, HEAD_DIM), BF16),
            sds((nb, ATT_KV_HEADS, seq, HEAD_DIM), BF16), sds((nb, seq, SSM_INNER), BF16),
            sds((nb, seq, SSM_XBC_W), BF16), sds((nb, seq, LANES), F32),
            sds((nb, seq, FNET_WIDTH), F32), sds((nb, seq, 3 * SCONV_WIDTH), BF16))
    head_spec = lambda nh: pl.BlockSpec((1, nh, t, HEAD_DIM), lambda b, i: (b, 0, i, 0))
    row_spec = lambda wd: pl.BlockSpec((1, t, wd), lambda b, i: (b, i, 0))
    return pl.pallas_call(
        _inproj_kernel, grid=(nb, nt),
        in_specs=[pl.BlockSpec((t, d), lambda b, i: (b * nt + i, 0)),
                  pl.BlockSpec((1, 8, d), mod_map),
                  _const_spec((1, d)), _const_spec((d, IN_W_PAD)),
                  pl.BlockSpec((t, LANES), lambda b, i: (i, 0)),
                  pl.BlockSpec((t, LANES), lambda b, i: (i, 0))],
        out_specs=(head_spec(ATT_HEADS), head_spec(ATT_KV_HEADS), head_spec(ATT_KV_HEADS),
                   row_spec(SSM_INNER), row_spec(SSM_XBC_W), row_spec(LANES),
                   row_spec(FNET_WIDTH), row_spec(3 * SCONV_WIDTH)),
        out_shape=outs, compiler_params=_params("arbitrary", "arbitrary"), name="inproj",
    )(xflat, mod, g, w, cos, sin)


def _attn_kernel(sink_ref, q_ref, *refs, tq, seq, band):
    if band:
        bias_ref, kp_ref, kc_ref, kn_ref, vp_ref, vc_ref, vn_ref, kx_ref, vx_ref, o_ref = refs
    else:
        kx_ref, vx_ref, o_ref = refs
    kh = pl.program_id(1)
    if band:
        keys = jnp.concatenate([kp_ref[0, 0], kc_ref[0, 0], kn_ref[0, 0], kx_ref[0, 0]], axis=0)
        vals = jnp.concatenate([vp_ref[0, 0], vc_ref[0, 0], vn_ref[0, 0], vx_ref[0, 0]], axis=0)
        bias = bias_ref[0]
    else:
        keys, vals = kx_ref[0, 0], vx_ref[0, 0]
    ones_col = (lax.broadcasted_iota(I32, vals.shape, 1) == 0).astype(BF16)
    v_ext = jnp.concatenate([vals, ones_col], axis=1)
    outs = []
    for g in range(ATT_GROUP):
        s = _dot_nt(q_ref[0, g], keys)
        if band:
            s = s + bias
        sink = sink_ref[kh * ATT_GROUP + g]
        m = jnp.maximum(jnp.max(s, axis=1, keepdims=True), sink)
        acc = jnp.dot(jnp.exp(s - m).astype(BF16), v_ext, preferred_element_type=F32)
        den = acc[:, HEAD_DIM:HEAD_DIM + 1] + jnp.exp(sink - m)
        outs.append(acc[:, :HEAD_DIM] / den)
    o_ref[0] = jnp.concatenate(outs, axis=1).astype(BF16)


def _attention(sink, q, k, v, kx, vx, tq, band):
    nb, _, seq, _ = q.shape
    lc = kx.shape[2]
    nt = seq // tq
    r = tq // WINDOW
    nblk = seq // WINDOW
    qs = pl.BlockSpec((1, ATT_GROUP, tq, HEAD_DIM), lambda b, h, i: (b, h, i, 0))
    cur = pl.BlockSpec((1, 1, tq, HEAD_DIM), lambda b, h, i: (b, h, i, 0))
    prev = pl.BlockSpec((1, 1, WINDOW, HEAD_DIM), lambda b, h, i: (b, h, jnp.maximum(i * r - 1, 0), 0))
    nxt = pl.BlockSpec((1, 1, WINDOW, HEAD_DIM), lambda b, h, i: (b, h, jnp.minimum((i + 1) * r, nblk - 1), 0))
    ctx = pl.BlockSpec((1, 1, lc, HEAD_DIM), lambda b, h, i: (b, h, 0, 0))
    smem = pl.BlockSpec(memory_space=pltpu.SMEM)
    if band:
        nk = tq + 2 * WINDOW
        rel = np.arange(nk)[None, :] - WINDOW - np.arange(tq)[:, None]
        inside = np.abs(rel) <= WINDOW
        col = np.arange(nk)[None, :]
        kinds = [inside & ((col >= WINDOW) | (not first)) & ((col < tq + WINDOW) | (not lastt))
                 for lastt in (False, True) for first in (False, True)]
        kinds = np.concatenate([np.stack(kinds), np.ones((4, tq, lc), bool)], axis=2)
        bias = jnp.asarray(np.where(kinds, 0.0, -np.inf), F32)
        bias_spec = pl.BlockSpec((1, tq, nk + lc),
                                 lambda b, h, i: ((i == 0).astype(I32) + 2 * (i == nt - 1).astype(I32), 0, 0))
        in_specs = [smem, qs, bias_spec, prev, cur, nxt, prev, cur, nxt, ctx, ctx]
        args = (sink, q, bias, k, k, k, v, v, v, kx, vx)
    else:
        in_specs = [smem, qs, ctx, ctx]
        args = (sink, q, kx, vx)
    return pl.pallas_call(
        functools.partial(_attn_kernel, tq=tq, seq=seq, band=band),
        grid=(nb, ATT_KV_HEADS, nt), in_specs=in_specs,
        out_specs=pl.BlockSpec((1, tq, ATT_GROUP * HEAD_DIM), lambda b, h, i: (b, i, h)),
        out_shape=jax.ShapeDtypeStruct((nb, seq, ATT_HEADS * HEAD_DIM), BF16),
        compiler_params=_params("arbitrary", "arbitrary", "arbitrary"), name="attn_band" if band else "attn_ctx",
    )(*args)


def _ssd_kernel(xp_ref, xc_ref, xn_ref, dt_ref, cw_ref, cb_ref, dtb_ref, alog_ref, init_ref,
                *refs, reverse, with_y, finalize, nc):
    if finalize:
        yf_ref, z_ref, dsk_ref, nw_ref, y_ref, st_ref = refs
    elif with_y:
        y_ref, st_ref = refs
    else:
        (st_ref,) = refs
    q = SSM_CHUNK
    rows = xc_ref.shape[1]
    c = pl.program_id(1)
    ce = (nc - 1 - c) if reverse else c

    @pl.when(c == 0)
    def _():
        st_ref[...] = init_ref[...]

    xc = xc_ref[0].astype(F32)
    prow = jnp.where(ce == 0, 0.0, xp_ref[0].astype(F32)[BF16_ROWS - 1:BF16_ROWS])
    nrow = jnp.where(ce == nc - 1, 0.0, xn_ref[0].astype(F32)[0:1])
    rid = lax.broadcasted_iota(I32, xc.shape, 0)
    up = jnp.where(rid == 0, prow, pltpu.roll(xc, 1, 0))
    dn = jnp.where(rid == rows - 1, nrow, pltpu.roll(xc, rows - 1, 0))
    cw = cw_ref[...]
    act_all = _silu(cw[0:1] * up + cw[1:2] * xc + cw[2:3] * dn + cb_ref[...])
    pre = dt_ref[0] + dtb_ref[...]
    dt_all = jnp.maximum(pre, 0.0) + jnp.log1p(jnp.exp(-jnp.abs(pre)))
    neg_a = -jnp.exp(alog_ref[...])

    subs = range(rows // q)
    for sub in (reversed(subs) if reverse else subs):
        r0 = sub * q
        _ssd_chunk(act_all[r0:r0 + q], dt_all[r0:r0 + q], neg_a, st_ref,
                   (yf_ref[0, r0:r0 + q], z_ref[0, r0:r0 + q], dsk_ref[...], nw_ref[...]) if finalize else None,
                   y_ref.at[0, r0:r0 + q] if with_y else None, reverse)


def _ssd_chunk(act, dt, neg_a, st_ref, fin, y_out, reverse):
    q = SSM_CHUNK
    with_y = y_out is not None
    xs = act[:, :SSM_INNER]
    bmat = act[:, SSM_INNER:SSM_INNER + SSM_BC_W]
    cmat = act[:, SSM_INNER + SSM_BC_W:]
    dta = dt * neg_a
    ri = lax.broadcasted_iota(I32, (q, q), 0)
    ci = lax.broadcasted_iota(I32, (q, q), 1)
    tri = (ri <= ci) if reverse else (ri >= ci)
    acs = _dot_split(tri.astype(BF16), dta)
    col0 = SSM_HEADS if reverse else 0
    lo_half = lax.broadcasted_iota(I32, (q, LANES), 1) < SSM_HEAD_DIM

    def bcast(mat, h):
        return jnp.broadcast_to(mat[:, col0 + h:col0 + h + 1], (q, LANES))

    def lane_expand(cols):
        return jnp.concatenate([jnp.where(lo_half, cols[2 * j], cols[2 * j + 1]) for j in range(SSM_HEADS // 2)],
                               axis=1)

    acs_b = [bcast(acs, h) for h in range(SSM_HEADS)]
    acs_e = lane_expand(acs_b)
    xd = xs * lane_expand([bcast(dt, h) for h in range(SSM_HEADS)])
    tot_e = acs_e[0:1] if reverse else acs_e[q - 1:q]
    hpg = SSM_HEADS // SSM_GROUPS
    gw = hpg * SSM_HEAD_DIM

    def grp(mat, g):
        return mat[:, g * SSM_STATE:(g + 1) * SSM_STATE]

    if with_y:
        acs_t = acs.T
        cb = [_dot_nt(grp(cmat, g), grp(bmat, g)) for g in range(SSM_GROUPS)]
        y_diag = []
        for j in range(SSM_HEADS // 2):
            sc = []
            for h in (2 * j, 2 * j + 1):
                seg = acs_b[h] - acs_t[col0 + h:col0 + h + 1, :]
                sc.append((cb[h // hpg] * jnp.exp(jnp.where(tri, seg, -jnp.inf))).astype(BF16))
            slab = xd[:, j * LANES:(j + 1) * LANES].astype(BF16)
            rhs = jnp.concatenate([jnp.where(lo_half, slab, 0), jnp.where(lo_half, 0, slab)], axis=0)
            y_diag.append(jnp.dot(jnp.concatenate(sc, axis=1), rhs, preferred_element_type=F32))
        y_off = [_dot(grp(cmat, g), st_ref[0, g]) for g in range(SSM_GROUPS)]
        y = jnp.concatenate(y_diag, axis=1) + jnp.concatenate(y_off, axis=1) * jnp.exp(acs_e)

    xdd = xd * jnp.exp(tot_e - acs_e)
    b_t = bmat.T
    for g in range(SSM_GROUPS):
        upd = _dot(b_t[g * SSM_STATE:(g + 1) * SSM_STATE, :], xdd[:, g * gw:(g + 1) * gw])
        st_ref[0, g] = st_ref[0, g] * jnp.exp(tot_e[:, g * gw:(g + 1) * gw]) + upd

    if fin is not None:
        y_fwd, z, dsk, nw = fin
        yt = y_fwd + y + dsk * xs
        yt = yt * _silu(z.astype(F32))
        yt = yt * lax.rsqrt(jnp.mean(yt * yt, axis=-1, keepdims=True) + EPS) * nw
        y_out[...] = yt.astype(BF16)
    elif with_y:
        y_out[...] = y


def _ssd(xbc, dtr, cw, cb, dtb, alog, init, reverse, with_y, fin=None):
    nb, seq, _ = xbc.shape
    st_block = (1, SSM_GROUPS, SSM_STATE, SSM_INNER // SSM_GROUPS)
    blk = SSM_CHUNK * SSD_CHUNKS_PER_STEP
    nc = seq // blk
    per = blk // BF16_ROWS
    nhalo = seq // BF16_ROWS
    ce = (lambda c: nc - 1 - c) if reverse else (lambda c: c)
    xw = SSM_XBC_W
    in_specs = [pl.BlockSpec((1, BF16_ROWS, xw), lambda b, c: (b, jnp.maximum(ce(c) * per - 1, 0), 0)),
                pl.BlockSpec((1, blk, xw), lambda b, c: (b, ce(c), 0)),
                pl.BlockSpec((1, BF16_ROWS, xw), lambda b, c: (b, jnp.minimum((ce(c) + 1) * per, nhalo - 1), 0)),
                pl.BlockSpec((1, blk, LANES), lambda b, c: (b, ce(c), 0)),
                _const_spec((8, xw)), _const_spec((1, xw)), _const_spec((1, LANES)), _const_spec((1, LANES)),
                pl.BlockSpec(st_block, lambda b, c: (b, 0, 0, 0))]
    args = [xbc, xbc, xbc, dtr, cw, cb, dtb, alog, init]
    st_spec = pl.BlockSpec(st_block, lambda b, c: (b, 0, 0, 0))
    st_shape = jax.ShapeDtypeStruct((nb,) + st_block[1:], F32)
    y_spec = pl.BlockSpec((1, blk, SSM_INNER), lambda b, c: (b, ce(c), 0))
    finalize = fin is not None
    if finalize:
        yf, z, dsk, nw = fin
        in_specs += [y_spec, y_spec, _const_spec((1, SSM_INNER)), _const_spec((1, SSM_INNER))]
        args += [yf, z, dsk, nw]
    if with_y:
        out_specs = (y_spec, st_spec)
        out_shape = (jax.ShapeDtypeStruct((nb, seq, SSM_INNER), BF16 if finalize else F32), st_shape)
    else:
        out_specs = (st_spec,)
        out_shape = (st_shape,)
    res = pl.pallas_call(
        functools.partial(_ssd_kernel, reverse=reverse, with_y=with_y, finalize=finalize, nc=nc),
        grid=(nb, nc), in_specs=in_specs, out_specs=out_specs, out_shape=out_shape,
        compiler_params=_params("arbitrary", "arbitrary"), name="ssd_bwd" if reverse else "ssd_fwd",
    )(*args)
    return res if with_y else (None, res[0])


def _fft_a_kernel(u_ref, wc_ref, m1_ref, y_ref, wc_bf):
    @pl.when((pl.program_id(0) == 0) & (pl.program_id(1) == 0))
    def _():
        wc_bf[...] = wc_ref[...].astype(BF16)

    for j in range(m1_ref.shape[0]):
        v = _dot(u_ref[0, :, j, :], wc_bf[...])
        vs = jnp.concatenate([v[:, :FNET_WIDTH], v[:, FNET_WIDTH:]], axis=0)
        b = _dot(m1_ref[j], vs)
        y_ref[0, 0, j] = b[:FFT_N2]
        y_ref[0, 1, j] = b[FFT_N2:]


def _fft_c_kernel(y_ref, m2_ref, o_ref, m2_bf):
    @pl.when((pl.program_id(0) == 0) & (pl.program_id(1) == 0))
    def _():
        m2_bf[...] = m2_ref[...].astype(BF16)

    rows = y_ref.shape[1] * y_ref.shape[2] * y_ref.shape[4]
    y2d = y_ref[0].reshape(rows, FNET_WIDTH).astype(BF16)
    o_ref[0] = jnp.dot(m2_bf[...], y2d, preferred_element_type=F32).reshape(o_ref.shape[1:])


def _fft_small_kernel(u_ref, wc_ref, m_ref, o_ref):
    v = _dot(u_ref[0], wc_ref[...])
    vs = jnp.concatenate([v[:, :FNET_WIDTH], v[:, FNET_WIDTH:]], axis=0)
    o_ref[0] = _dot(m_ref[...], vs).astype(BF16)


def _channel_dft():
    idx = np.arange(FNET_GROUP_DIM)
    ang = 2.0 * np.pi * np.outer(idx, idx) / FNET_GROUP_DIM
    eye = np.eye(FNET_GROUPS)
    return jnp.asarray(np.concatenate([np.kron(eye, np.cos(ang)), -np.kron(eye, np.sin(ang))], axis=1), F32)


def _fourier_long(u):
    nb, seq, w = u.shape
    n2 = FFT_N2
    n1 = seq // n2
    t2 = np.arange(n2)
    k2 = np.arange(n2)
    t1 = np.arange(n1)
    ang = 2.0 * np.pi * (np.outer(k2, t2)[None] / n2 + (t1[:, None, None] * k2[None, :, None]) / seq)
    co, si = np.cos(ang), np.sin(ang)
    m1 = jnp.asarray(np.concatenate([np.concatenate([co, si], axis=2),
                                     np.concatenate([-si, co], axis=2)], axis=1), F32)
    ang1 = 2.0 * np.pi * np.outer(t1, t1) / n1
    norm = 1.0 / math.sqrt(seq * FNET_GROUP_DIM)
    m2 = np.concatenate([np.cos(ang1), np.sin(ang1)], axis=1) * norm
    m2k = jnp.asarray(np.kron(m2, np.eye(F32_ROWS)), F32)
    per = F32_ROWS
    y = pl.pallas_call(
        _fft_a_kernel, grid=(nb, n1 // per),
        in_specs=[pl.BlockSpec((1, n2, per, w), lambda b, j: (b, 0, j, 0)),
                  _const_spec((w, 2 * w)),
                  pl.BlockSpec((per, 2 * n2, 2 * n2), lambda b, j: (j, 0, 0))],
        out_specs=pl.BlockSpec((1, 2, per, n2, w), lambda b, j: (b, 0, j, 0, 0)),
        out_shape=jax.ShapeDtypeStruct((nb, 2, n1, n2, w), F32),
        scratch_shapes=[pltpu.VMEM((w, 2 * w), BF16)],
        compiler_params=_params("arbitrary", "arbitrary"), name="fft_a",
    )(u.reshape(nb, n2, n1, w), _channel_dft(), m1)
    out = pl.pallas_call(
        _fft_c_kernel, grid=(nb, n2 // per),
        in_specs=[pl.BlockSpec((1, 2, n1, 1, per, w), lambda b, j: (b, 0, 0, j, 0, 0)),
                  _const_spec((n1 * per, 2 * n1 * per))],
        out_specs=pl.BlockSpec((1, n1, 1, per, w), lambda b, j: (b, 0, j, 0, 0)),
        out_shape=jax.ShapeDtypeStruct((nb, n1, n2 // per, per, w), F32),
        scratch_shapes=[pltpu.VMEM((n1 * per, 2 * n1 * per), BF16)],
        compiler_params=_params("arbitrary", "arbitrary"), name="fft_c",
    )(y.reshape(nb, 2, n1, n2 // per, per, w), m2k)
    return out.reshape(nb, seq, w)


def _fourier_short(u):
    nb, seq, w = u.shape
    t = np.arange(seq)
    ang = 2.0 * np.pi * np.outer(t, t) / seq
    norm = 1.0 / math.sqrt(seq * FNET_GROUP_DIM)
    m = jnp.asarray(np.concatenate([np.cos(ang), np.sin(ang)], axis=1) * norm, F32)
    return pl.pallas_call(
        _fft_small_kernel, grid=(nb,),
        in_specs=[pl.BlockSpec((1, seq, w), lambda b: (b, 0, 0)),
                  _const_spec((w, 2 * w)), _const_spec((seq, 2 * seq))],
        out_specs=pl.BlockSpec((1, seq, w), lambda b: (b, 0, 0)),
        out_shape=jax.ShapeDtypeStruct((nb, seq, w), BF16),
        compiler_params=_params("arbitrary"), name="fft_small",
    )(u, _channel_dft(), m)


def _merge_kernel(x_ref, mod_ref, g1_ref, g2_ref, att_ref, ssm_ref, fft_ref, sup_ref, su_ref, sun_ref,
                  sw_ref, wg_ref, bg_ref, wb_ref, wo_ref, wr_ref, rb_ref, cin_ref,
                  x1_ref, h2_ref, ri_ref, rw_ref, cnt_ref, *, nt):
    d = x_ref.shape[1]
    t = x_ref.shape[0]
    i = pl.program_id(1)

    @pl.when((pl.program_id(0) == 0) & (i == 0))
    def _():
        cnt_ref[...] = cin_ref[...]

    m = mod_ref[0]

    def gated(su):
        return su[:, SCONV_WIDTH:2 * SCONV_WIDTH] * su[:, 2 * SCONV_WIDTH:]

    su = su_ref[0].astype(F32)
    p = gated(su)
    prow = jnp.where(i == 0, 0.0, gated(sup_ref[0].astype(F32)[BF16_ROWS - 1:BF16_ROWS]))
    nrow = jnp.where(i == nt - 1, 0.0, gated(sun_ref[0].astype(F32)[0:1]))
    rid = lax.broadcasted_iota(I32, p.shape, 0)
    up = jnp.where(rid == 0, prow, pltpu.roll(p, 1, 0))
    dn = jnp.where(rid == t - 1, nrow, pltpu.roll(p, t - 1, 0))
    sw = sw_ref[...]
    sconv = su[:, :SCONV_WIDTH] * (sw[0:1] * up + sw[1:2] * p + sw[2:3] * dn)

    rg = min(t, MERGE_ROW_GROUP)
    h2_parts = []
    for a in range(0, t, rg):
        x = x_ref[a:a + rg]
        hb = _norm_mod(x, g1_ref[...], m[0:1], m[1:2]).astype(BF16)
        branches = (att_ref[0, a:a + rg], ssm_ref[0, a:a + rg], fft_ref[0, a:a + rg].astype(BF16),
                    sconv[a:a + rg].astype(BF16))
        acc = jnp.zeros((rg, d), F32)
        for n in range(N_BRANCHES):
            gate = _sigmoid(jnp.dot(hb, wg_ref[:, n * d:(n + 1) * d], preferred_element_type=F32)
                            + bg_ref[:, n * d:(n + 1) * d])
            acc = acc + gate * jnp.dot(branches[n], wb_ref[n], preferred_element_type=F32)
        y = jnp.dot(acc.astype(BF16), wo_ref[...], preferred_element_type=F32)
        x1 = x + m[2:3] * y
        x1_ref[a:a + rg] = x1
        h2 = _norm_mod(x1, g2_ref[...], m[3:4], m[4:5])
        h2_ref[a:a + rg] = h2
        h2_parts.append(h2)
    _route_tile(jnp.concatenate(h2_parts, axis=0), wr_ref, rb_ref, ri_ref, rw_ref, cnt_ref)


def _merge(xflat, nb, seq, t, mod, mod_row, g1, g2, att, ssm, fft, su, sw, wg, bg, wb, wo, wr, rb, cnt_in):
    ntok, d = xflat.shape
    nt = seq // t
    per = t // BF16_ROWS
    nhalo = seq // BF16_ROWS
    mod_map = (lambda b, i: (b, 0, 0)) if mod_row is None else (lambda b, i: (mod_row, 0, 0))
    row = pl.BlockSpec((t, d), lambda b, i: (b * nt + i, 0))
    br = pl.BlockSpec((1, t, BRANCH_W), lambda b, i: (b, i, 0))
    suw = 3 * SCONV_WIDTH
    in_specs = [row, pl.BlockSpec((1, 8, d), mod_map), _const_spec((1, d)), _const_spec((1, d)),
                br, br, br,
                pl.BlockSpec((1, BF16_ROWS, suw), lambda b, i: (b, jnp.maximum(i * per - 1, 0), 0)),
                pl.BlockSpec((1, t, suw), lambda b, i: (b, i, 0)),
                pl.BlockSpec((1, BF16_ROWS, suw), lambda b, i: (b, jnp.minimum((i + 1) * per, nhalo - 1), 0)),
                _const_spec((8, SCONV_WIDTH)), _const_spec((d, N_BRANCHES * d)), _const_spec((1, N_BRANCHES * d)),
                _const_spec((N_BRANCHES, BRANCH_W, d)), _const_spec((d, d)),
                _const_spec((N_EXPERTS, d)), _const_spec((N_EXPERTS, t)), _const_spec((N_EXPERTS, LANES))]
    tile_rows = pl.BlockSpec((1, 8, t), lambda b, i: (b * nt + i, 0, 0))
    x1, h2, ri, rw, cnt = pl.pallas_call(
        functools.partial(_merge_kernel, nt=nt),
        grid=(nb, nt), in_specs=in_specs,
        out_specs=(row, row, tile_rows, tile_rows, pl.BlockSpec((N_EXPERTS, LANES), lambda b, i: (0, 0))),
        out_shape=(jax.ShapeDtypeStruct((ntok, d), F32), jax.ShapeDtypeStruct((ntok, d), F32),
                   jax.ShapeDtypeStruct((ntok // t, 8, t), I32), jax.ShapeDtypeStruct((ntok // t, 8, t), F32),
                   jax.ShapeDtypeStruct((N_EXPERTS, LANES), F32)),
        compiler_params=_params("arbitrary", "arbitrary"), name="merge",
    )(xflat, mod, g1, g2, att, ssm, fft, su, su, su, sw, wg, bg, wb, wo, wr, jnp.tile(rb, (1, t // LANES)), cnt_in)
    return x1, h2, ri.transpose(0, 2, 1).reshape(ntok, 8), rw.transpose(0, 2, 1).reshape(ntok, 8), cnt


def _route_tile(h2, wr_ref, rb_ref, ri_ref, rw_ref, cnt_ref):
    t = h2.shape[0]
    ng = N_EXPERT_GROUPS
    sc = _sigmoid(_dot_nt(wr_ref[...], h2))
    sel = sc + rb_ref[...]
    s = [sel[j * ng:(j + 1) * ng] for j in range(EXPERTS_PER_GROUP)]
    u = [sc[j * ng:(j + 1) * ng] for j in range(EXPERTS_PER_GROUP)]
    gs = None
    for a in range(EXPERTS_PER_GROUP):
        for b in range(a + 1, EXPERTS_PER_GROUP):
            pair = s[a] + s[b]
            gs = pair if gs is None else jnp.maximum(gs, pair)
    grp = lax.broadcasted_iota(I32, (ng, t), 0)
    gmax = jnp.max(gs, axis=0, keepdims=True)
    best = jnp.min(jnp.where(gs == gmax, grp, ng), axis=0, keepdims=True)
    on = grp == best
    v = [jnp.sum(jnp.where(on, s[j], 0.0), axis=0, keepdims=True) for j in range(EXPERTS_PER_GROUP)]
    w = [jnp.sum(jnp.where(on, u[j], 0.0), axis=0, keepdims=True) for j in range(EXPERTS_PER_GROUP)]

    def first_argmax(vals):
        mx = vals[0]
        for x in vals[1:]:
            mx = jnp.maximum(mx, x)
        idx = jnp.full_like(best, len(vals) - 1)
        for j in range(len(vals) - 2, -1, -1):
            idx = jnp.where(vals[j] == mx, j, idx)
        return idx

    def pick(vals, idx):
        out = vals[-1]
        for j in range(len(vals) - 2, -1, -1):
            out = jnp.where(idx == j, vals[j], out)
        return out

    l1 = first_argmax(v)
    l2 = first_argmax([jnp.where(l1 == j, -jnp.inf, v[j]) for j in range(EXPERTS_PER_GROUP)])
    w1, w2 = pick(w, l1), pick(w, l2)
    tot = w1 + w2
    e1 = best * EXPERTS_PER_GROUP + l1
    e2 = best * EXPERTS_PER_GROUP + l2

    eid = lax.broadcasted_iota(I32, (N_EXPERTS, t), 0)
    oh1 = (eid == e1).astype(F32)
    oh2 = (eid == e2).astype(F32)
    oh = oh1 + oh2
    ri_ = lax.broadcasted_iota(I32, (t, t), 0)
    ci_ = lax.broadcasted_iota(I32, (t, t), 1)
    cnt = cnt_ref[...]
    before = _dot(oh, (ri_ < ci_).astype(F32)) + jnp.tile(cnt, (1, t // LANES))
    r1 = jnp.sum(oh1 * before, axis=0, keepdims=True).astype(I32)
    r2 = jnp.sum(oh2 * before, axis=0, keepdims=True).astype(I32)
    cnt_ref[...] = cnt + jnp.sum(oh, axis=1, keepdims=True)

    row = lax.broadcasted_iota(I32, (8, t), 0)
    ri_ref[0] = jnp.where(row == 0, e1, jnp.where(row == 1, e2, jnp.where(row == 2, r1, jnp.where(row == 3, r2, 0))))
    rw_ref[0] = jnp.where(row == 0, w1 / tot, jnp.where(row == 1, w2 / tot, 0.0))


def _dispatch_kernel(dest_ref, zt_ref, *refs, td, starts):
    n_streams = len(starts) - 1
    h_refs = refs[:n_streams]
    xs_hbm, zbuf, zsem, sem = refs[n_streams:]
    i = pl.program_id(0)

    @pl.when(i == 0)
    def _():
        zbuf[...] = jnp.zeros_like(zbuf)

        def zero_copy(j):
            start = pl.multiple_of(zt_ref[j] * MOE_TILE, MOE_TILE)
            return pltpu.make_async_copy(zbuf, xs_hbm.at[pl.ds(start, MOE_TILE)], zsem)

        def zissue(j, carry):
            @pl.when(zt_ref[j] >= 0)
            def _():
                zero_copy(j).start()
            return carry

        def zdrain(j, carry):
            @pl.when(zt_ref[j] >= 0)
            def _():
                zero_copy(j).wait()
            return carry

        lax.fori_loop(0, zt_ref.shape[0], zissue, 0)
        lax.fori_loop(0, zt_ref.shape[0], zdrain, 0)

    for s in range(n_streams):
        h_ref = h_refs[s]

        def row_copy(src_row, dst_row, h_ref=h_ref):
            return pltpu.make_async_copy(h_ref.at[pl.ds(src_row, 1)], xs_hbm.at[pl.ds(dst_row, 1)], sem)

        @pl.when((i >= starts[s]) & (i < starts[s + 1]))
        def _(row_copy=row_copy):
            def issue(j, carry):
                row_copy(j, dest_ref[0, 0, 2 * j]).start()
                row_copy(j, dest_ref[0, 0, 2 * j + 1]).start()
                return carry

            lax.fori_loop(0, td, issue, 0, unroll=ROW_DMA_UNROLL)
            all_rows = xs_hbm.at[pl.ds(0, 2 * td)]
            pltpu.make_async_copy(all_rows, all_rows, sem).wait()


def _dispatch(h2s, dest, ztiles, nrows):
    d = h2s[0].shape[1]
    td = 256
    starts = [0]
    for h in h2s:
        starts.append(starts[-1] + h.shape[0] // td)
    nt = starts[-1]

    def stream_spec(s):
        return pl.BlockSpec((td, d), lambda i: (jnp.clip(i - starts[s], 0, starts[s + 1] - starts[s] - 1), 0))

    return pl.pallas_call(
        functools.partial(_dispatch_kernel, td=td, starts=tuple(starts)), grid=(nt,),
        in_specs=[pl.BlockSpec((1, 1, 2 * td), lambda i: (i, 0, 0), memory_space=pltpu.SMEM),
                  pl.BlockSpec(memory_space=pltpu.SMEM)] + [stream_spec(s) for s in range(len(h2s))],
        out_specs=pl.BlockSpec(memory_space=pl.ANY),
        out_shape=jax.ShapeDtypeStruct((nrows, d), F32),
        scratch_shapes=[pltpu.VMEM((MOE_TILE, d), F32), pltpu.SemaphoreType.DMA(()), pltpu.SemaphoreType.DMA(())],
        compiler_params=_params("arbitrary"), name="dispatch",
    )(dest.reshape(nt, 1, 2 * td), ztiles, *h2s)


def _ffn_kernel(be_ref, nu_ref, x_ref, w1_ref, w3_ref, w2_ref, y_ref, w1_bf, w3_bf, w2_bf):
    i = pl.program_id(0)
    used = i < nu_ref[0]

    @pl.when(used & ((i == 0) | (be_ref[i] != be_ref[jnp.maximum(i - 1, 0)])))
    def _():
        w1_bf[...] = w1_ref[0, 0].astype(BF16)
        w3_bf[...] = w3_ref[0, 0].astype(BF16)
        w2_bf[...] = w2_ref[0, 0].astype(BF16)

    @pl.when(used)
    def _():
        xb = x_ref[...].astype(BF16)
        a = jnp.dot(xb, w1_bf[...], preferred_element_type=F32)
        b = jnp.dot(xb, w3_bf[...], preferred_element_type=F32)
        y_ref[...] = jnp.dot((_silu(a) * b).astype(BF16), w2_bf[...], preferred_element_type=F32)

    @pl.when(i >= nu_ref[0])
    def _():
        y_ref[...] = jnp.zeros_like(y_ref)


def _ffn(xs, block_expert, n_used, w1, w3, w2, layer):
    nrows, d = xs.shape
    de = w1.shape[3]
    tm = MOE_TILE
    grid_spec = pltpu.PrefetchScalarGridSpec(
        num_scalar_prefetch=2, grid=(nrows // tm,),
        in_specs=[pl.BlockSpec((tm, d), lambda i, be, nu: (jnp.minimum(i, nu[0] - 1), 0)),
                  pl.BlockSpec((1, 1, d, de), lambda i, be, nu: (layer, be[i], 0, 0)),
                  pl.BlockSpec((1, 1, d, de), lambda i, be, nu: (layer, be[i], 0, 0)),
                  pl.BlockSpec((1, 1, de, d), lambda i, be, nu: (layer, be[i], 0, 0))],
        out_specs=pl.BlockSpec((tm, d), lambda i, be, nu: (i, 0)),
        scratch_shapes=[pltpu.VMEM((d, de), BF16), pltpu.VMEM((d, de), BF16), pltpu.VMEM((de, d), BF16)])
    return pl.pallas_call(
        _ffn_kernel, grid_spec=grid_spec, out_shape=jax.ShapeDtypeStruct((nrows, d), F32),
        compiler_params=_params("arbitrary"), name="ffn",
    )(block_expert, n_used, xs, w1, w3, w2)


def _combine_kernel(dest_ref, dnext_ref, x_ref, mod_ref, rw_ref, fw_ref, y_hbm, o_ref, buf, sems, *, tc, final):
    i = pl.program_id(0)
    nt = pl.num_programs(0)
    slot = i % 2

    def gather(d_ref, s):
        def issue(j, carry):
            for k in range(2):
                pltpu.make_async_copy(y_hbm.at[pl.ds(d_ref[0, 0, 2 * j + k], 1)],
                                      buf.at[s, k, pl.ds(j, 1)], sems.at[s]).start()
            return carry
        lax.fori_loop(0, tc, issue, 0, unroll=ROW_DMA_UNROLL)

    @pl.when(i == 0)
    def _():
        gather(dest_ref, 0)

    @pl.when(i + 1 < nt)
    def _():
        gather(dnext_ref, 1 - slot)

    pltpu.make_async_copy(buf.at[slot], buf.at[slot], sems.at[slot]).wait()
    rw = rw_ref[...]
    f = rw[:, 0:1] * buf[slot, 0] + rw[:, 1:2] * buf[slot, 1]
    x = x_ref[...] + mod_ref[0][5:6] * f
    if final:
        x = x * lax.rsqrt(jnp.mean(x * x, axis=-1, keepdims=True) + EPS) * fw_ref[...]
    o_ref[...] = x


def _combine(x1, dest, rw, y, mod, mod_row, seq, fw, final):
    ntok, d = x1.shape
    tc = 256
    nt = ntok // tc
    per_seq = seq // tc
    mod_map = (lambda i: (i // per_seq, 0, 0)) if mod_row is None else (lambda i: (mod_row, 0, 0))
    return pl.pallas_call(
        functools.partial(_combine_kernel, tc=tc, final=final), grid=(nt,),
        in_specs=[pl.BlockSpec((1, 1, 2 * tc), lambda i: (i, 0, 0), memory_space=pltpu.SMEM),
                  pl.BlockSpec((1, 1, 2 * tc), lambda i: (jnp.minimum(i + 1, nt - 1), 0, 0), memory_space=pltpu.SMEM),
                  pl.BlockSpec((tc, d), lambda i: (i, 0)),
                  pl.BlockSpec((1, 8, d), mod_map),
                  pl.BlockSpec((tc, 8), lambda i: (i, 0)),
                  _const_spec((1, d)),
                  pl.BlockSpec(memory_space=pl.ANY)],
        out_specs=pl.BlockSpec((tc, d), lambda i: (i, 0)),
        out_shape=jax.ShapeDtypeStruct((ntok, d), F32),
        scratch_shapes=[pltpu.VMEM((2, 2, tc, d), F32), pltpu.SemaphoreType.DMA((2,))],
        compiler_params=_params("arbitrary"), name="combine",
    )(dest.reshape(nt, 1, 2 * tc), dest.reshape(nt, 1, 2 * tc), x1, mod, rw, fw, y)


def _moe(streams, cnt, mod, w1, w3, w2, layer, fw, final):
    counts = cnt[:, 0].astype(I32)
    padded = (counts + MOE_TILE - 1) // MOE_TILE * MOE_TILE
    pad_end = jnp.cumsum(padded)
    offs = pad_end - padded
    ntok = sum(s[0].shape[0] for s in streams)
    n_tiles = -(-2 * ntok // MOE_TILE) + N_EXPERTS
    tile_start = jnp.arange(n_tiles, dtype=I32) * MOE_TILE
    block_expert = jnp.minimum(jnp.sum((pad_end[None, :] <= tile_start[:, None]).astype(I32), axis=1), N_EXPERTS - 1)
    n_used = (pad_end[-1:] // MOE_TILE).astype(I32)
    dests = [(offs[s[2][:, 0:2]] + s[2][:, 2:4]).reshape(-1) for s in streams]
    tail = n_used + jnp.arange(N_EXPERTS, dtype=I32)
    ztiles = jnp.concatenate([jnp.where(padded > 0, pad_end // MOE_TILE - 1, -1),
                              jnp.where(tail < n_tiles, tail, -1)]).astype(I32)
    xs = _dispatch([s[1] for s in streams], jnp.concatenate(dests), ztiles, n_tiles * MOE_TILE)
    y = _ffn(xs, block_expert, n_used, w1, w3, w2, layer)
    return [_combine(x1, dest, rw, y, mod, mod_row, seq, fw, final)
            for (x1, _, _, rw, seq, mod_row), dest in zip(streams, dests)]


def _pad_rows(a, rows):
    return jnp.concatenate([a, jnp.zeros((rows - a.shape[0],) + a.shape[1:], a.dtype)], axis=0)


def _pad_cols(a, cols, value=0.0):
    return jnp.concatenate([a, jnp.full(a.shape[:-1] + (cols - a.shape[-1],), value, a.dtype)], axis=-1)


def _rope_tables(seq):
    rows = seq // GRID_W
    row = jnp.repeat(jnp.arange(rows, dtype=F32), GRID_W)
    col = jnp.tile(jnp.arange(GRID_W, dtype=F32), rows)
    pairs = HEAD_DIM // 4
    inv_freq = ROPE_THETA ** (-jnp.arange(pairs, dtype=F32) / pairs)
    ang = jnp.concatenate([row[:, None] * inv_freq, col[:, None] * inv_freq], axis=-1)
    cos, sin = jnp.cos(ang), jnp.sin(ang)
    cos_t = jnp.tile(cos, (1, LANES // (HEAD_DIM // 2)))
    sin_t = jnp.tile(jnp.concatenate([-sin, sin], axis=-1), (1, LANES // HEAD_DIM))
    return cos_t, sin_t


def kernel(x, c, ctx, c_ctx, norm1_w, norm2_w, w_ada, b_ada, w_in, attn_sink, ssm_conv_w, ssm_conv_b,
           ssm_dt_bias, ssm_a_log, ssm_d, ssm_norm_w, sconv_w, w_gate, b_gate, w_branch, w_o,
           w_router, router_bias, moe_w1, moe_w3, moe_w2, final_norm_w):
    nb, seq, d = x.shape
    lc = ctx.shape[1]
    depth = w_in.shape[0]
    n_lat = nb * seq
    n_ctx = nb * lc
    assert seq % 512 == 0 and seq % (FFT_N2 * 8) == 0 and lc % 256 == 0 and nb < 8

    xl = x.reshape(n_lat, d)
    xc = ctx.reshape(n_ctx, d)
    c8 = _pad_rows(jnp.concatenate([c, c_ctx[None]], axis=0), 8)
    mods = _ada(c8, w_ada, b_ada)
    mods = _pad_cols(mods.reshape(depth, 8, 6, d).swapaxes(-1, -2), 8).swapaxes(-1, -2)

    cos_l, sin_l = _rope_tables(seq)
    cos_c, sin_c = jnp.ones((lc, LANES), F32), jnp.zeros((lc, LANES), F32)

    wr = w_router.T.reshape(N_EXPERT_GROUPS, EXPERTS_PER_GROUP, d).transpose(1, 0, 2).reshape(N_EXPERTS, d)
    wr = wr.astype(BF16)
    rb = jnp.broadcast_to(router_bias.reshape(N_EXPERT_GROUPS, EXPERTS_PER_GROUP).T.reshape(N_EXPERTS, 1),
                          (N_EXPERTS, LANES))

    for layer in range(depth):
        last = layer + 1 == depth
        mod = mods[layer]
        wi = w_in[layer]
        w_pad = jnp.concatenate([wi[:, :2048], _pad_cols(wi[:, 2048:2048 + DT_W], LANES), wi[:, 2048 + DT_W:]],
                                axis=1).astype(BF16)
        g1 = norm1_w[layer][None]
        g2 = norm2_w[layer][None]
        cw = _pad_rows(ssm_conv_w[layer], 8)
        cb = ssm_conv_b[layer][None]
        dtb = _pad_cols(ssm_dt_bias[layer].reshape(1, DT_W), LANES)
        alog = _pad_cols(ssm_a_log[layer].reshape(1, DT_W), LANES)
        dsk = jnp.repeat(ssm_d[layer], SSM_HEAD_DIM)[None]
        nw = ssm_norm_w[layer][None]
        sw = _pad_rows(sconv_w[layer], 8)
        wg = w_gate[layer].astype(BF16)
        bg = b_gate[layer][None]
        wb = w_branch[layer].astype(BF16)
        wo = w_o[layer].astype(BF16)
        sink = attn_sink[layer]

        q, k, v, z, xbc, dtr, fu, su = _inproj(xl, nb, seq, 512, mod, None, g1, w_pad, cos_l, sin_l)
        qc, kc, vc, zc, xbcc, dtrc, fuc, suc = _inproj(xc, nb, lc, 256, mod, nb, g1, w_pad, cos_c, sin_c)

        att = _attention(sink, q, k, v, kc, vc, 256, True)

        zero_state = jnp.zeros((nb, SSM_GROUPS, SSM_STATE, SSM_INNER // SSM_GROUPS), F32)
        ssd_c = functools.partial(_ssd, xbcc, dtrc, cw, cb, dtb, alog)
        ssd_l = functools.partial(_ssd, xbc, dtr, cw, cb, dtb, alog)
        if last:
            _, fin_f = ssd_c(zero_state, False, False)
            _, fin_b = ssd_c(zero_state, True, False)
        else:
            yc_f, fin_f = ssd_c(zero_state, False, True)
            ssm_c, fin_b = ssd_c(zero_state, True, True, fin=(yc_f, zc, dsk, nw))
        y_f, _ = ssd_l(fin_f, False, True)
        ssm, _ = ssd_l(fin_b, True, True, fin=(y_f, z, dsk, nw))

        fft = _fourier_long(fu)

        x1, h2, ri, rw, cnt = _merge(xl, nb, seq, 512, mod, None, g1, g2, att, ssm, fft, su, sw, wg, bg, wb, wo,
                                     wr, rb, jnp.zeros((N_EXPERTS, LANES), F32))
        streams = [(x1, h2, ri, rw, seq, None)]
        if not last:
            att_c = _attention(sink, qc, None, None, kc, vc, lc, False)
            fft_c = _fourier_short(fuc)
            x1c, h2c, ric, rwc, cnt = _merge(xc, nb, lc, 256, mod, nb, g1, g2, att_c, ssm_c, fft_c, suc, sw, wg,
                                             bg, wb, wo, wr, rb, cnt)
            streams.append((x1c, h2c, ric, rwc, lc, nb))
        new = _moe(streams, cnt, mod, moe_w1, moe_w3, moe_w2, layer, final_norm_w[None], last)
        xl = new[0]
        if not last:
            xc = new[1]
    return xl.reshape(nb, seq, d)
```

```python
import functools
import math

import numpy as np
import jax
import jax.numpy as jnp
from jax import lax
from jax.experimental import pallas as pl
from jax.experimental.pallas import tpu as pltpu

F32 = jnp.float32
BF16 = jnp.bfloat16
I32 = jnp.int32

EPS = 1e-6
GRID_W = 64
ROPE_THETA = 10000.0
HEAD_DIM = 64
ATT_HEADS = 8
ATT_KV_HEADS = 2
ATT_GROUP = ATT_HEADS // ATT_KV_HEADS
WINDOW = 128
SSM_HEADS = 8
SSM_HEAD_DIM = 64
SSM_INNER = SSM_HEADS * SSM_HEAD_DIM
SSM_GROUPS = 2
SSM_STATE = 64
SSM_BC_W = SSM_GROUPS * SSM_STATE
SSM_XBC_W = SSM_INNER + 2 * SSM_BC_W
SSM_CHUNK = 128
FNET_GROUPS = 8
FNET_GROUP_DIM = 64
FNET_WIDTH = FNET_GROUPS * FNET_GROUP_DIM
SCONV_WIDTH = 512
BRANCH_W = 512
N_BRANCHES = 4
N_EXPERTS = 32
N_EXPERT_GROUPS = 8
EXPERTS_PER_GROUP = N_EXPERTS // N_EXPERT_GROUPS
D_EXPERT = 512

LANES = 128
BF16_ROWS = 16
VMEM_LIMIT = 56 * 1024 * 1024
FFT_N2 = 128
F32_ROWS = 8
SSD_CHUNKS_PER_STEP = 2
MERGE_ROW_GROUP = 256
MOE_TILE = 256
ROW_DMA_UNROLL = 8

Q_OFF, K_OFF, Z_OFF, XBC_OFF, FU_OFF, SU_OFF = 0, 512, 1024, 1536, 2304, 2816
IN_W_PAD = SU_OFF + 3 * SCONV_WIDTH
DT_W = 2 * SSM_HEADS


def _dot(a, b):
    return jnp.dot(a.astype(BF16), b.astype(BF16), preferred_element_type=F32)


def _dot_nt(a, b):
    return lax.dot_general(a.astype(BF16), b.astype(BF16), (((1,), (1,)), ((), ())),
                           preferred_element_type=F32)


def _dot_split(a_bf16, x):
    hi = x.astype(BF16)
    rest = x - hi.astype(F32)
    mid = rest.astype(BF16)
    lo = (rest - mid.astype(F32)).astype(BF16)
    return sum(jnp.dot(a_bf16, t, preferred_element_type=F32) for t in (hi, mid, lo))


def _sigmoid(x):
    return 1.0 / (1.0 + jnp.exp(-x))


def _silu(x):
    return x * _sigmoid(x)


def _norm_mod(x, g, shift, scale):
    y = x * lax.rsqrt(jnp.mean(x * x, axis=-1, keepdims=True) + EPS) * g
    return y * (1.0 + scale) + shift


def _params(*sem):
    return pltpu.CompilerParams(dimension_semantics=sem, vmem_limit_bytes=VMEM_LIMIT)


def _const_spec(shape):
    n = len(shape)
    return pl.BlockSpec(shape, lambda *_: (0,) * n, pipeline_mode=pl.Buffered(1))


def _ada_kernel(c_ref, w_ref, b_ref, o_ref):
    o_ref[0] = _dot(_silu(c_ref[...]), w_ref[0]) + b_ref[0]


def _ada(c8, w_ada, b_ada):
    nl, d, w = w_ada.shape
    tn = 512
    return pl.pallas_call(
        _ada_kernel, grid=(nl, w // tn),
        in_specs=[pl.BlockSpec((8, d), lambda l, j: (0, 0)),
                  pl.BlockSpec((1, d, tn), lambda l, j: (l, 0, j)),
                  pl.BlockSpec((1, 1, tn), lambda l, j: (l, 0, j))],
        out_specs=pl.BlockSpec((1, 8, tn), lambda l, j: (l, 0, j)),
        out_shape=jax.ShapeDtypeStruct((nl, 8, w), F32),
        compiler_params=_params("arbitrary", "arbitrary"), name="ada",
    )(c8, w_ada, b_ada.reshape(nl, 1, w))


def _inproj_kernel(x_ref, mod_ref, g_ref, w_ref, cos_ref, sin_ref,
                   q_ref, k_ref, v_ref, z_ref, xbc_ref, dt_ref, fu_ref, su_ref):
    m = mod_ref[0]
    hb = _norm_mod(x_ref[...], g_ref[...], m[0:1], m[1:2]).astype(BF16)
    cos = cos_ref[...]
    sin = sin_ref[...]
    lane = lax.broadcasted_iota(I32, cos.shape, 1)
    first_half = (lane % HEAD_DIM) < HEAD_DIM // 2

    def proj(start, width):
        return jnp.dot(hb, w_ref[:, start:start + width], preferred_element_type=F32)

    def rope(r):
        rot = jnp.where(first_half, pltpu.roll(r, LANES - HEAD_DIM // 2, 1), pltpu.roll(r, HEAD_DIM // 2, 1))
        return r * cos + rot * sin

    scale = HEAD_DIM ** -0.5
    qq = proj(Q_OFF, ATT_HEADS * HEAD_DIM)
    for j in range(ATT_HEADS // 2):
        r = rope(qq[:, j * LANES:(j + 1) * LANES]) * scale
        q_ref[0, 2 * j] = r[:, :HEAD_DIM].astype(BF16)
        q_ref[0, 2 * j + 1] = r[:, HEAD_DIM:].astype(BF16)
    kvd = proj(K_OFF, 4 * LANES)
    r = rope(kvd[:, :LANES])
    k_ref[0, 0] = r[:, :HEAD_DIM].astype(BF16)
    k_ref[0, 1] = r[:, HEAD_DIM:].astype(BF16)
    r = kvd[:, LANES:2 * LANES]
    v_ref[0, 0] = r[:, :HEAD_DIM].astype(BF16)
    v_ref[0, 1] = r[:, HEAD_DIM:].astype(BF16)
    dt_ref[0] = kvd[:, 2 * LANES:3 * LANES]
    z_ref[0] = proj(Z_OFF, SSM_INNER).astype(BF16)
    xbc_ref[0] = proj(XBC_OFF, SSM_XBC_W).astype(BF16)
    fu_ref[0] = proj(FU_OFF, FNET_WIDTH)
    su_ref[0] = proj(SU_OFF, 3 * SCONV_WIDTH).astype(BF16)


def _inproj(xflat, nb, seq, t, mod, mod_row, g, w, cos, sin):
    d = xflat.shape[1]
    nt = seq // t
    if mod_row is None:
        mod_map = lambda b, i: (b, 0, 0)
    else:
        mod_map = lambda b, i: (mod_row, 0, 0)
    sds = jax.ShapeDtypeStruct
    outs = (sds((nb, ATT_HEADS, seq, HEAD_DIM), BF16), sds((nb, ATT_KV_HEADS, seq, HEAD_DIM), BF16),
            sds((nb, ATT_KV_HEADS, seq, HEAD_DIM), BF16), sds((nb, seq, SSM_INNER), BF16),
            sds((nb, seq, SSM_XBC_W), BF16), sds((nb, seq, LANES), F32),
            sds((nb, seq, FNET_WIDTH), F32), sds((nb, seq, 3 * SCONV_WIDTH), BF16))
    head_spec = lambda nh: pl.BlockSpec((1, nh, t, HEAD_DIM), lambda b, i: (b, 0, i, 0))
    row_spec = lambda wd: pl.BlockSpec((1, t, wd), lambda b, i: (b, i, 0))
    return pl.pallas_call(
        _inproj_kernel, grid=(nb, nt),
        in_specs=[pl.BlockSpec((t, d), lambda b, i: (b * nt + i, 0)),
                  pl.BlockSpec((1, 8, d), mod_map),
                  _const_spec((1, d)), _const_spec((d, IN_W_PAD)),
                  pl.BlockSpec((t, LANES), lambda b, i: (i, 0)),
                  pl.BlockSpec((t, LANES), lambda b, i: (i, 0))],
        out_specs=(head_spec(ATT_HEADS), head_spec(ATT_KV_HEADS), head_spec(ATT_KV_HEADS),
                   row_spec(SSM_INNER), row_spec(SSM_XBC_W), row_spec(LANES),
                   row_spec(FNET_WIDTH), row_spec(3 * SCONV_WIDTH)),
        out_shape=outs, compiler_params=_params("arbitrary", "arbitrary"), name="inproj",
    )(xflat, mod, g, w, cos, sin)


def _attn_kernel(sink_ref, q_ref, *refs, tq, seq, band):
    if band:
        bias_ref, kp_ref, kc_ref, kn_ref, vp_ref, vc_ref, vn_ref, kx_ref, vx_ref, o_ref = refs
    else:
        kx_ref, vx_ref, o_ref = refs
    kh = pl.program_id(1)
    if band:
        keys = jnp.concatenate([kp_ref[0, 0], kc_ref[0, 0], kn_ref[0, 0], kx_ref[0, 0]], axis=0)
        vals = jnp.concatenate([vp_ref[0, 0], vc_ref[0, 0], vn_ref[0, 0], vx_ref[0, 0]], axis=0)
        bias = bias_ref[0]
    else:
        keys, vals = kx_ref[0, 0], vx_ref[0, 0]
    ones_col = (lax.broadcasted_iota(I32, vals.shape, 1) == 0).astype(BF16)
    v_ext = jnp.concatenate([vals, ones_col], axis=1)
    outs = []
    for g in range(ATT_GROUP):
        s = _dot_nt(q_ref[0, g], keys)
        if band:
            s = s + bias
        sink = sink_ref[kh * ATT_GROUP + g]
        m = jnp.maximum(jnp.max(s, axis=1, keepdims=True), sink)
        acc = jnp.dot(jnp.exp(s - m).astype(BF16), v_ext, preferred_element_type=F32)
        den = acc[:, HEAD_DIM:HEAD_DIM + 1] + jnp.exp(sink - m)
        outs.append(acc[:, :HEAD_DIM] / den)
    o_ref[0] = jnp.concatenate(outs, axis=1).astype(BF16)


def _attention(sink, q, k, v, kx, vx, tq, band):
    nb, _, seq, _ = q.shape
    lc = kx.shape[2]
    nt = seq // tq
    r = tq // WINDOW
    nblk = seq // WINDOW
    qs = pl.BlockSpec((1, ATT_GROUP, tq, HEAD_DIM), lambda b, h, i: (b, h, i, 0))
    cur = pl.BlockSpec((1, 1, tq, HEAD_DIM), lambda b, h, i: (b, h, i, 0))
    prev = pl.BlockSpec((1, 1, WINDOW, HEAD_DIM), lambda b, h, i: (b, h, jnp.maximum(i * r - 1, 0), 0))
    nxt = pl.BlockSpec((1, 1, WINDOW, HEAD_DIM), lambda b, h, i: (b, h, jnp.minimum((i + 1) * r, nblk - 1), 0))
    ctx = pl.BlockSpec((1, 1, lc, HEAD_DIM), lambda b, h, i: (b, h, 0, 0))
    smem = pl.BlockSpec(memory_space=pltpu.SMEM)
    if band:
        nk = tq + 2 * WINDOW
        rel = np.arange(nk)[None, :] - WINDOW - np.arange(tq)[:, None]
        inside = np.abs(rel) <= WINDOW
        col = np.arange(nk)[None, :]
        kinds = [inside & ((col >= WINDOW) | (not first)) & ((col < tq + WINDOW) | (not lastt))
                 for lastt in (False, True) for first in (False, True)]
        kinds = np.concatenate([np.stack(kinds), np.ones((4, tq, lc), bool)], axis=2)
        bias = jnp.asarray(np.where(kinds, 0.0, -np.inf), F32)
        bias_spec = pl.BlockSpec((1, tq, nk + lc),
                                 lambda b, h, i: ((i == 0).astype(I32) + 2 * (i == nt - 1).astype(I32), 0, 0))
        in_specs = [smem, qs, bias_spec, prev, cur, nxt, prev, cur, nxt, ctx, ctx]
        args = (sink, q, bias, k, k, k, v, v, v, kx, vx)
    else:
        in_specs = [smem, qs, ctx, ctx]
        args = (sink, q, kx, vx)
    return pl.pallas_call(
        functools.partial(_attn_kernel, tq=tq, seq=seq, band=band),
        grid=(nb, ATT_KV_HEADS, nt), in_specs=in_specs,
        out_specs=pl.BlockSpec((1, tq, ATT_GROUP * HEAD_DIM), lambda b, h, i: (b, i, h)),
        out_shape=jax.ShapeDtypeStruct((nb, seq, ATT_HEADS * HEAD_DIM), BF16),
        compiler_params=_params("arbitrary", "arbitrary", "arbitrary"), name="attn_band" if band else "attn_ctx",
    )(*args)


def _ssd_kernel(xp_ref, xc_ref, xn_ref, dt_ref, cw_ref, cb_ref, dtb_ref, alog_ref, init_ref,
                *refs, reverse, with_y, finalize, nc):
    if finalize:
        yf_ref, z_ref, dsk_ref, nw_ref, y_ref, st_ref = refs
    elif with_y:
        y_ref, st_ref = refs
    else:
        (st_ref,) = refs
    q = SSM_CHUNK
    rows = xc_ref.shape[1]
    c = pl.program_id(1)
    ce = (nc - 1 - c) if reverse else c

    @pl.when(c == 0)
    def _():
        st_ref[...] = init_ref[...]

    xc = xc_ref[0].astype(F32)
    prow = jnp.where(ce == 0, 0.0, xp_ref[0].astype(F32)[BF16_ROWS - 1:BF16_ROWS])
    nrow = jnp.where(ce == nc - 1, 0.0, xn_ref[0].astype(F32)[0:1])
    rid = lax.broadcasted_iota(I32, xc.shape, 0)
    up = jnp.where(rid == 0, prow, pltpu.roll(xc, 1, 0))
    dn = jnp.where(rid == rows - 1, nrow, pltpu.roll(xc, rows - 1, 0))
    cw = cw_ref[...]
    act_all = _silu(cw[0:1] * up + cw[1:2] * xc + cw[2:3] * dn + cb_ref[...])
    pre = dt_ref[0] + dtb_ref[...]
    dt_all = jnp.maximum(pre, 0.0) + jnp.log1p(jnp.exp(-jnp.abs(pre)))
    neg_a = -jnp.exp(alog_ref[...])

    subs = range(rows // q)
    for sub in (reversed(subs) if reverse else subs):
        r0 = sub * q
        _ssd_chunk(act_all[r0:r0 + q], dt_all[r0:r0 + q], neg_a, st_ref,
                   (yf_ref[0, r0:r0 + q], z_ref[0, r0:r0 + q], dsk_ref[...], nw_ref[...]) if finalize else None,
                   y_ref.at[0, r0:r0 + q] if with_y else None, reverse)


def _ssd_chunk(act, dt, neg_a, st_ref, fin, y_out, reverse):
    q = SSM_CHUNK
    with_y = y_out is not None
    xs = act[:, :SSM_INNER]
    bmat = act[:, SSM_INNER:SSM_INNER + SSM_BC_W]
    cmat = act[:, SSM_INNER + SSM_BC_W:]
    dta = dt * neg_a
    ri = lax.broadcasted_iota(I32, (q, q), 0)
    ci = lax.broadcasted_iota(I32, (q, q), 1)
    tri = (ri <= ci) if reverse else (ri >= ci)
    acs = _dot_split(tri.astype(BF16), dta)
    col0 = SSM_HEADS if reverse else 0
    lo_half = lax.broadcasted_iota(I32, (q, LANES), 1) < SSM_HEAD_DIM

    def bcast(mat, h):
        return jnp.broadcast_to(mat[:, col0 + h:col0 + h + 1], (q, LANES))

    def lane_expand(cols):
        return jnp.concatenate([jnp.where(lo_half, cols[2 * j], cols[2 * j + 1]) for j in range(SSM_HEADS // 2)],
                               axis=1)

    acs_b = [bcast(acs, h) for h in range(SSM_HEADS)]
    acs_e = lane_expand(acs_b)
    xd = xs * lane_expand([bcast(dt, h) for h in range(SSM_HEADS)])
    tot_e = acs_e[0:1] if reverse else acs_e[q - 1:q]
    hpg = SSM_HEADS // SSM_GROUPS
    gw = hpg * SSM_HEAD_DIM

    def grp(mat, g):
        return mat[:, g * SSM_STATE:(g + 1) * SSM_STATE]

    if with_y:
        acs_t = acs.T
        cb = [_dot_nt(grp(cmat, g), grp(bmat, g)) for g in range(SSM_GROUPS)]
        y_diag = []
        for j in range(SSM_HEADS // 2):
            sc = []
            for h in (2 * j, 2 * j + 1):
                seg = acs_b[h] - acs_t[col0 + h:col0 + h + 1, :]
                sc.append((cb[h // hpg] * jnp.exp(jnp.where(tri, seg, -jnp.inf))).astype(BF16))
            slab = xd[:, j * LANES:(j + 1) * LANES].astype(BF16)
            rhs = jnp.concatenate([jnp.where(lo_half, slab, 0), jnp.where(lo_half, 0, slab)], axis=0)
            y_diag.append(jnp.dot(jnp.concatenate(sc, axis=1), rhs, preferred_element_type=F32))
        y_off = [_dot(grp(cmat, g), st_ref[0, g]) for g in range(SSM_GROUPS)]
        y = jnp.concatenate(y_diag, axis=1) + jnp.concatenate(y_off, axis=1) * jnp.exp(acs_e)

    xdd = xd * jnp.exp(tot_e - acs_e)
    b_t = bmat.T
    for g in range(SSM_GROUPS):
        upd = _dot(b_t[g * SSM_STATE:(g + 1) * SSM_STATE, :], xdd[:, g * gw:(g + 1) * gw])
        st_ref[0, g] = st_ref[0, g] * jnp.exp(tot_e[:, g * gw:(g + 1) * gw]) + upd

    if fin is not None:
        y_fwd, z, dsk, nw = fin
        yt = y_fwd + y + dsk * xs
        yt = yt * _silu(z.astype(F32))
        yt = yt * lax.rsqrt(jnp.mean(yt * yt, axis=-1, keepdims=True) + EPS) * nw
        y_out[...] = yt.astype(BF16)
    elif with_y:
        y_out[...] = y


def _ssd(xbc, dtr, cw, cb, dtb, alog, init, reverse, with_y, fin=None):
    nb, seq, _ = xbc.shape
    st_block = (1, SSM_GROUPS, SSM_STATE, SSM_INNER // SSM_GROUPS)
    blk = SSM_CHUNK * SSD_CHUNKS_PER_STEP
    nc = seq // blk
    per = blk // BF16_ROWS
    nhalo = seq // BF16_ROWS
    ce = (lambda c: nc - 1 - c) if reverse else (lambda c: c)
    xw = SSM_XBC_W
    in_specs = [pl.BlockSpec((1, BF16_ROWS, xw), lambda b, c: (b, jnp.maximum(ce(c) * per - 1, 0), 0)),
                pl.BlockSpec((1, blk, xw), lambda b, c: (b, ce(c), 0)),
                pl.BlockSpec((1, BF16_ROWS, xw), lambda b, c: (b, jnp.minimum((ce(c) + 1) * per, nhalo - 1), 0)),
                pl.BlockSpec((1, blk, LANES), lambda b, c: (b, ce(c), 0)),
                _const_spec((8, xw)), _const_spec((1, xw)), _const_spec((1, LANES)), _const_spec((1, LANES)),
                pl.BlockSpec(st_block, lambda b, c: (b, 0, 0, 0))]
    args = [xbc, xbc, xbc, dtr, cw, cb, dtb, alog, init]
    st_spec = pl.BlockSpec(st_block, lambda b, c: (b, 0, 0, 0))
    st_shape = jax.ShapeDtypeStruct((nb,) + st_block[1:], F32)
    y_spec = pl.BlockSpec((1, blk, SSM_INNER), lambda b, c: (b, ce(c), 0))
    finalize = fin is not None
    if finalize:
        yf, z, dsk, nw = fin
        in_specs += [y_spec, y_spec, _const_spec((1, SSM_INNER)), _const_spec((1, SSM_INNER))]
        args += [yf, z, dsk, nw]
    if with_y:
        out_specs = (y_spec, st_spec)
        out_shape = (jax.ShapeDtypeStruct((nb, seq, SSM_INNER), BF16 if finalize else F32), st_shape)
    else:
        out_specs = (st_spec,)
        out_shape = (st_shape,)
    res = pl.pallas_call(
        functools.partial(_ssd_kernel, reverse=reverse, with_y=with_y, finalize=finalize, nc=nc),
        grid=(nb, nc), in_specs=in_specs, out_specs=out_specs, out_shape=out_shape,
        compiler_params=_params("arbitrary", "arbitrary"), name="ssd_bwd" if reverse else "ssd_fwd",
    )(*args)
    return res if with_y else (None, res[0])


def _fft_a_kernel(u_ref, wc_ref, m1_ref, y_ref, wc_bf):
    @pl.when((pl.program_id(0) == 0) & (pl.program_id(1) == 0))
    def _():
        wc_bf[...] = wc_ref[...].astype(BF16)

    per = m1_ref.shape[0]
    u_all = jnp.concatenate([u_ref[0, :, j, :] for j in range(per)], axis=0)
    v_all = _dot(u_all, wc_bf[...])
    for j in range(per):
        v = v_all[j * FFT_N2:(j + 1) * FFT_N2]
        vs = jnp.concatenate([v[:, :FNET_WIDTH], v[:, FNET_WIDTH:]], axis=0)
        b = _dot(m1_ref[j], vs)
        y_ref[0, 0, j] = b[:FFT_N2]
        y_ref[0, 1, j] = b[FFT_N2:]


def _fft_c_kernel(y_ref, m2_ref, o_ref, m2_bf):
    @pl.when((pl.program_id(0) == 0) & (pl.program_id(1) == 0))
    def _():
        m2_bf[...] = m2_ref[...].astype(BF16)

    rows = y_ref.shape[1] * y_ref.shape[2] * y_ref.shape[4]
    y2d = y_ref[0].reshape(rows, FNET_WIDTH).astype(BF16)
    o_ref[0] = jnp.dot(m2_bf[...], y2d, preferred_element_type=F32).reshape(o_ref.shape[1:])


def _fft_small_kernel(u_ref, wc_ref, m_ref, o_ref):
    v = _dot(u_ref[0], wc_ref[...])
    vs = jnp.concatenate([v[:, :FNET_WIDTH], v[:, FNET_WIDTH:]], axis=0)
    o_ref[0] = _dot(m_ref[...], vs).astype(BF16)


def _channel_dft():
    idx = np.arange(FNET_GROUP_DIM)
    ang = 2.0 * np.pi * np.outer(idx, idx) / FNET_GROUP_DIM
    eye = np.eye(FNET_GROUPS)
    return jnp.asarray(np.concatenate([np.kron(eye, np.cos(ang)), -np.kron(eye, np.sin(ang))], axis=1), F32)


def _fourier_long(u):
    nb, seq, w = u.shape
    n2 = FFT_N2
    n1 = seq // n2
    t2 = np.arange(n2)
    k2 = np.arange(n2)
    t1 = np.arange(n1)
    ang = 2.0 * np.pi * (np.outer(k2, t2)[None] / n2 + (t1[:, None, None] * k2[None, :, None]) / seq)
    co, si = np.cos(ang), np.sin(ang)
    m1 = jnp.asarray(np.concatenate([np.concatenate([co, si], axis=2),
                                     np.concatenate([-si, co], axis=2)], axis=1), F32)
    ang1 = 2.0 * np.pi * np.outer(t1, t1) / n1
    norm = 1.0 / math.sqrt(seq * FNET_GROUP_DIM)
    m2 = np.concatenate([np.cos(ang1), np.sin(ang1)], axis=1) * norm
    m2k = jnp.asarray(np.kron(m2, np.eye(F32_ROWS)), F32)
    per = F32_ROWS
    y = pl.pallas_call(
        _fft_a_kernel, grid=(n1 // per, nb),
        in_specs=[pl.BlockSpec((1, n2, per, w), lambda j, b: (b, 0, j, 0)),
                  _const_spec((w, 2 * w)),
                  pl.BlockSpec((per, 2 * n2, 2 * n2), lambda j, b: (j, 0, 0))],
        out_specs=pl.BlockSpec((1, 2, per, n2, w), lambda j, b: (b, 0, j, 0, 0)),
        out_shape=jax.ShapeDtypeStruct((nb, 2, n1, n2, w), F32),
        scratch_shapes=[pltpu.VMEM((w, 2 * w), BF16)],
        compiler_params=_params("arbitrary", "arbitrary"), name="fft_a",
    )(u.reshape(nb, n2, n1, w), _channel_dft(), m1)
    out = pl.pallas_call(
        _fft_c_kernel, grid=(nb, n2 // per),
        in_specs=[pl.BlockSpec((1, 2, n1, 1, per, w), lambda b, j: (b, 0, 0, j, 0, 0)),
                  _const_spec((n1 * per, 2 * n1 * per))],
        out_specs=pl.BlockSpec((1, n1, 1, per, w), lambda b, j: (b, 0, j, 0, 0)),
        out_shape=jax.ShapeDtypeStruct((nb, n1, n2 // per, per, w), F32),
        scratch_shapes=[pltpu.VMEM((n1 * per, 2 * n1 * per), BF16)],
        compiler_params=_params("arbitrary", "arbitrary"), name="fft_c",
    )(y.reshape(nb, 2, n1, n2 // per, per, w), m2k)
    return out.reshape(nb, seq, w)


def _fourier_short(u):
    nb, seq, w = u.shape
    t = np.arange(seq)
    ang = 2.0 * np.pi * np.outer(t, t) / seq
    norm = 1.0 / math.sqrt(seq * FNET_GROUP_DIM)
    m = jnp.asarray(np.concatenate([np.cos(ang), np.sin(ang)], axis=1) * norm, F32)
    return pl.pallas_call(
        _fft_small_kernel, grid=(nb,),
        in_specs=[pl.BlockSpec((1, seq, w), lambda b: (b, 0, 0)),
                  _const_spec((w, 2 * w)), _const_spec((seq, 2 * seq))],
        out_specs=pl.BlockSpec((1, seq, w), lambda b: (b, 0, 0)),
        out_shape=jax.ShapeDtypeStruct((nb, seq, w), BF16),
        compiler_params=_params("arbitrary"), name="fft_small",
    )(u, _channel_dft(), m)


def _merge_kernel(x_ref, mod_ref, g1_ref, g2_ref, att_ref, ssm_ref, fft_ref, sup_ref, su_ref, sun_ref,
                  sw_ref, wg_ref, bg_ref, wb_ref, wo_ref, wr_ref, rb_ref, cin_ref,
                  x1_ref, h2_ref, ri_ref, rw_ref, cnt_ref, *, nt):
    d = x_ref.shape[1]
    t = x_ref.shape[0]
    i = pl.program_id(1)

    @pl.when((pl.program_id(0) == 0) & (i == 0))
    def _():
        cnt_ref[...] = cin_ref[...]

    m = mod_ref[0]

    def gated(su):
        return su[:, SCONV_WIDTH:2 * SCONV_WIDTH] * su[:, 2 * SCONV_WIDTH:]

    su = su_ref[0].astype(F32)
    p = gated(su)
    prow = jnp.where(i == 0, 0.0, gated(sup_ref[0].astype(F32)[BF16_ROWS - 1:BF16_ROWS]))
    nrow = jnp.where(i == nt - 1, 0.0, gated(sun_ref[0].astype(F32)[0:1]))
    rid = lax.broadcasted_iota(I32, p.shape, 0)
    up = jnp.where(rid == 0, prow, pltpu.roll(p, 1, 0))
    dn = jnp.where(rid == t - 1, nrow, pltpu.roll(p, t - 1, 0))
    sw = sw_ref[...]
    sconv = su[:, :SCONV_WIDTH] * (sw[0:1] * up + sw[1:2] * p + sw[2:3] * dn)

    rg = min(t, MERGE_ROW_GROUP)
    h2_parts = []
    for a in range(0, t, rg):
        x = x_ref[a:a + rg]
        hb = _norm_mod(x, g1_ref[...], m[0:1], m[1:2]).astype(BF16)
        branches = (att_ref[0, a:a + rg], ssm_ref[0, a:a + rg], fft_ref[0, a:a + rg].astype(BF16),
                    sconv[a:a + rg].astype(BF16))
        acc = jnp.zeros((rg, d), F32)
        for n in range(N_BRANCHES):
            gate = _sigmoid(jnp.dot(hb, wg_ref[:, n * d:(n + 1) * d], preferred_element_type=F32)
                            + bg_ref[:, n * d:(n + 1) * d])
            acc = acc + gate * jnp.dot(branches[n], wb_ref[n], preferred_element_type=F32)
        y = jnp.dot(acc.astype(BF16), wo_ref[...], preferred_element_type=F32)
        x1 = x + m[2:3] * y
        x1_ref[a:a + rg] = x1
        h2 = _norm_mod(x1, g2_ref[...], m[3:4], m[4:5])
        h2_ref[a:a + rg] = h2
        h2_parts.append(h2)
    _route_tile(jnp.concatenate(h2_parts, axis=0), wr_ref, rb_ref, ri_ref, rw_ref, cnt_ref)


def _merge(xflat, nb, seq, t, mod, mod_row, g1, g2, att, ssm, fft, su, sw, wg, bg, wb, wo, wr, rb, cnt_in):
    ntok, d = xflat.shape
    nt = seq // t
    per = t // BF16_ROWS
    nhalo = seq // BF16_ROWS
    mod_map = (lambda b, i: (b, 0, 0)) if mod_row is None else (lambda b, i: (mod_row, 0, 0))
    row = pl.BlockSpec((t, d), lambda b, i: (b * nt + i, 0))
    br = pl.BlockSpec((1, t, BRANCH_W), lambda b, i: (b, i, 0))
    suw = 3 * SCONV_WIDTH
    in_specs = [row, pl.BlockSpec((1, 8, d), mod_map), _const_spec((1, d)), _const_spec((1, d)),
                br, br, br,
                pl.BlockSpec((1, BF16_ROWS, suw), lambda b, i: (b, jnp.maximum(i * per - 1, 0), 0)),
                pl.BlockSpec((1, t, suw), lambda b, i: (b, i, 0)),
                pl.BlockSpec((1, BF16_ROWS, suw), lambda b, i: (b, jnp.minimum((i + 1) * per, nhalo - 1), 0)),
                _const_spec((8, SCONV_WIDTH)), _const_spec((d, N_BRANCHES * d)), _const_spec((1, N_BRANCHES * d)),
                _const_spec((N_BRANCHES, BRANCH_W, d)), _const_spec((d, d)),
                _const_spec((N_EXPERTS, d)), _const_spec((N_EXPERTS, t)), _const_spec((N_EXPERTS, LANES))]
    tile_rows = pl.BlockSpec((1, 8, t), lambda b, i: (b * nt + i, 0, 0))
    x1, h2, ri, rw, cnt = pl.pallas_call(
        functools.partial(_merge_kernel, nt=nt),
        grid=(nb, nt), in_specs=in_specs,
        out_specs=(row, row, tile_rows, tile_rows, pl.BlockSpec((N_EXPERTS, LANES), lambda b, i: (0, 0))),
        out_shape=(jax.ShapeDtypeStruct((ntok, d), F32), jax.ShapeDtypeStruct((ntok, d), F32),
                   jax.ShapeDtypeStruct((ntok // t, 8, t), I32), jax.ShapeDtypeStruct((ntok // t, 8, t), F32),
                   jax.ShapeDtypeStruct((N_EXPERTS, LANES), F32)),
        compiler_params=_params("arbitrary", "arbitrary"), name="merge",
    )(xflat, mod, g1, g2, att, ssm, fft, su, su, su, sw, wg, bg, wb, wo, wr, jnp.tile(rb, (1, t // LANES)), cnt_in)
    return x1, h2, ri.transpose(0, 2, 1).reshape(ntok, 8), rw.transpose(0, 2, 1).reshape(ntok, 8), cnt


def _route_tile(h2, wr_ref, rb_ref, ri_ref, rw_ref, cnt_ref):
    t = h2.shape[0]
    ng = N_EXPERT_GROUPS
    sc = _sigmoid(_dot_nt(wr_ref[...], h2))
    sel = sc + rb_ref[...]
    s = [sel[j * ng:(j + 1) * ng] for j in range(EXPERTS_PER_GROUP)]
    u = [sc[j * ng:(j + 1) * ng] for j in range(EXPERTS_PER_GROUP)]
    gs = None
    for a in range(EXPERTS_PER_GROUP):
        for b in range(a + 1, EXPERTS_PER_GROUP):
            pair = s[a] + s[b]
            gs = pair if gs is None else jnp.maximum(gs, pair)
    grp = lax.broadcasted_iota(I32, (ng, t), 0)
    gmax = jnp.max(gs, axis=0, keepdims=True)
    best = jnp.min(jnp.where(gs == gmax, grp, ng), axis=0, keepdims=True)
    on = grp == best
    v = [jnp.sum(jnp.where(on, s[j], 0.0), axis=0, keepdims=True) for j in range(EXPERTS_PER_GROUP)]
    w = [jnp.sum(jnp.where(on, u[j], 0.0), axis=0, keepdims=True) for j in range(EXPERTS_PER_GROUP)]

    def first_argmax(vals):
        mx = vals[0]
        for x in vals[1:]:
            mx = jnp.maximum(mx, x)
        idx = jnp.full_like(best, len(vals) - 1)
        for j in range(len(vals) - 2, -1, -1):
            idx = jnp.where(vals[j] == mx, j, idx)
        return idx

    def pick(vals, idx):
        out = vals[-1]
        for j in range(len(vals) - 2, -1, -1):
            out = jnp.where(idx == j, vals[j], out)
        return out

    l1 = first_argmax(v)
    l2 = first_argmax([jnp.where(l1 == j, -jnp.inf, v[j]) for j in range(EXPERTS_PER_GROUP)])
    w1, w2 = pick(w, l1), pick(w, l2)
    tot = w1 + w2
    e1 = best * EXPERTS_PER_GROUP + l1
    e2 = best * EXPERTS_PER_GROUP + l2

    eid = lax.broadcasted_iota(I32, (N_EXPERTS, t), 0)
    oh1 = (eid == e1).astype(F32)
    oh2 = (eid == e2).astype(F32)
    oh = oh1 + oh2
    ri_ = lax.broadcasted_iota(I32, (t, t), 0)
    ci_ = lax.broadcasted_iota(I32, (t, t), 1)
    cnt = cnt_ref[...]
    before = _dot(oh, (ri_ < ci_).astype(F32)) + jnp.tile(cnt, (1, t // LANES))
    r1 = jnp.sum(oh1 * before, axis=0, keepdims=True).astype(I32)
    r2 = jnp.sum(oh2 * before, axis=0, keepdims=True).astype(I32)
    cnt_ref[...] = cnt + jnp.sum(oh, axis=1, keepdims=True)

    row = lax.broadcasted_iota(I32, (8, t), 0)
    ri_ref[0] = jnp.where(row == 0, e1, jnp.where(row == 1, e2, jnp.where(row == 2, r1, jnp.where(row == 3, r2, 0))))
    rw_ref[0] = jnp.where(row == 0, w1 / tot, jnp.where(row == 1, w2 / tot, 0.0))


def _dispatch_kernel(dest_ref, zt_ref, *refs, td, starts):
    n_streams = len(starts) - 1
    h_refs = refs[:n_streams]
    xs_hbm, zbuf, zsem, sem = refs[n_streams:]
    i = pl.program_id(0)

    @pl.when(i == 0)
    def _():
        zbuf[...] = jnp.zeros_like(zbuf)

        def zero_copy(j):
            start = pl.multiple_of(zt_ref[j] * MOE_TILE, MOE_TILE)
            return pltpu.make_async_copy(zbuf, xs_hbm.at[pl.ds(start, MOE_TILE)], zsem)

        def zissue(j, carry):
            @pl.when(zt_ref[j] >= 0)
            def _():
                zero_copy(j).start()
            return carry

        def zdrain(j, carry):
            @pl.when(zt_ref[j] >= 0)
            def _():
                zero_copy(j).wait()
            return carry

        lax.fori_loop(0, zt_ref.shape[0], zissue, 0)
        lax.fori_loop(0, zt_ref.shape[0], zdrain, 0)

    for s in range(n_streams):
        h_ref = h_refs[s]

        def row_copy(src_row, dst_row, h_ref=h_ref):
            return pltpu.make_async_copy(h_ref.at[pl.ds(src_row, 1)], xs_hbm.at[pl.ds(dst_row, 1)], sem)

        @pl.when((i >= starts[s]) & (i < starts[s + 1]))
        def _(row_copy=row_copy):
            def issue(j, carry):
                row_copy(j, dest_ref[0, 0, 2 * j]).start()
                row_copy(j, dest_ref[0, 0, 2 * j + 1]).start()
                return carry

            lax.fori_loop(0, td, issue, 0, unroll=ROW_DMA_UNROLL)
            all_rows = xs_hbm.at[pl.ds(0, 2 * td)]
            pltpu.make_async_copy(all_rows, all_rows, sem).wait()


def _dispatch(h2s, dest, ztiles, nrows):
    d = h2s[0].shape[1]
    td = 256
    starts = [0]
    for h in h2s:
        starts.append(starts[-1] + h.shape[0] // td)
    nt = starts[-1]

    def stream_spec(s):
        return pl.BlockSpec((td, d), lambda i: (jnp.clip(i - starts[s], 0, starts[s + 1] - starts[s] - 1), 0))

    return pl.pallas_call(
        functools.partial(_dispatch_kernel, td=td, starts=tuple(starts)), grid=(nt,),
        in_specs=[pl.BlockSpec((1, 1, 2 * td), lambda i: (i, 0, 0), memory_space=pltpu.SMEM),
                  pl.BlockSpec(memory_space=pltpu.SMEM)] + [stream_spec(s) for s in range(len(h2s))],
        out_specs=pl.BlockSpec(memory_space=pl.ANY),
        out_shape=jax.ShapeDtypeStruct((nrows, d), F32),
        scratch_shapes=[pltpu.VMEM((MOE_TILE, d), F32), pltpu.SemaphoreType.DMA(()), pltpu.SemaphoreType.DMA(())],
        compiler_params=_params("arbitrary"), name="dispatch",
    )(dest.reshape(nt, 1, 2 * td), ztiles, *h2s)


def _ffn_kernel(be_ref, nu_ref, x_ref, w1_ref, w3_ref, w2_ref, y_ref, w1_bf, w3_bf, w2_bf):
    i = pl.program_id(0)
    used = i < nu_ref[0]

    @pl.when(used & ((i == 0) | (be_ref[i] != be_ref[jnp.maximum(i - 1, 0)])))
    def _():
        w1_bf[...] = w1_ref[0, 0].astype(BF16)
        w3_bf[...] = w3_ref[0, 0].astype(BF16)
        w2_bf[...] = w2_ref[0, 0].astype(BF16)

    @pl.when(used)
    def _():
        xb = x_ref[...].astype(BF16)
        a = jnp.dot(xb, w1_bf[...], preferred_element_type=F32)
        b = jnp.dot(xb, w3_bf[...], preferred_element_type=F32)
        y_ref[...] = jnp.dot((_silu(a) * b).astype(BF16), w2_bf[...], preferred_element_type=F32)

    @pl.when(i >= nu_ref[0])
    def _():
        y_ref[...] = jnp.zeros_like(y_ref)


def _ffn(xs, block_expert, n_used, w1, w3, w2, layer):
    nrows, d = xs.shape
    de = w1.shape[3]
    tm = MOE_TILE
    grid_spec = pltpu.PrefetchScalarGridSpec(
        num_scalar_prefetch=2, grid=(nrows // tm,),
        in_specs=[pl.BlockSpec((tm, d), lambda i, be, nu: (jnp.minimum(i, nu[0] - 1), 0)),
                  pl.BlockSpec((1, 1, d, de), lambda i, be, nu: (layer, be[i], 0, 0)),
                  pl.BlockSpec((1, 1, d, de), lambda i, be, nu: (layer, be[i], 0, 0)),
                  pl.BlockSpec((1, 1, de, d), lambda i, be, nu: (layer, be[i], 0, 0))],
        out_specs=pl.BlockSpec((tm, d), lambda i, be, nu: (i, 0)),
        scratch_shapes=[pltpu.VMEM((d, de), BF16), pltpu.VMEM((d, de), BF16), pltpu.VMEM((de, d), BF16)])
    return pl.pallas_call(
        _ffn_kernel, grid_spec=grid_spec, out_shape=jax.ShapeDtypeStruct((nrows, d), F32),
        compiler_params=_params("arbitrary"), name="ffn",
    )(block_expert, n_used, xs, w1, w3, w2)


def _combine_kernel(dest_ref, dnext_ref, x_ref, mod_ref, rw_ref, fw_ref, y_hbm, o_ref, buf, sems, *, tc, final):
    i = pl.program_id(0)
    nt = pl.num_programs(0)
    slot = i % 2

    def gather(d_ref, s):
        def issue(j, carry):
            for k in range(2):
                pltpu.make_async_copy(y_hbm.at[pl.ds(d_ref[0, 0, 2 * j + k], 1)],
                                      buf.at[s, k, pl.ds(j, 1)], sems.at[s]).start()
            return carry
        lax.fori_loop(0, tc, issue, 0, unroll=ROW_DMA_UNROLL)

    @pl.when(i == 0)
    def _():
        gather(dest_ref, 0)

    @pl.when(i + 1 < nt)
    def _():
        gather(dnext_ref, 1 - slot)

    pltpu.make_async_copy(buf.at[slot], buf.at[slot], sems.at[slot]).wait()
    rw = rw_ref[...]
    f = rw[:, 0:1] * buf[slot, 0] + rw[:, 1:2] * buf[slot, 1]
    x = x_ref[...] + mod_ref[0][5:6] * f
    if final:
        x = x * lax.rsqrt(jnp.mean(x * x, axis=-1, keepdims=True) + EPS) * fw_ref[...]
    o_ref[...] = x


def _combine(x1, dest, rw, y, mod, mod_row, seq, fw, final):
    ntok, d = x1.shape
    tc = 256
    nt = ntok // tc
    per_seq = seq // tc
    mod_map = (lambda i: (i // per_seq, 0, 0)) if mod_row is None else (lambda i: (mod_row, 0, 0))
    return pl.pallas_call(
        functools.partial(_combine_kernel, tc=tc, final=final), grid=(nt,),
        in_specs=[pl.BlockSpec((1, 1, 2 * tc), lambda i: (i, 0, 0), memory_space=pltpu.SMEM),
                  pl.BlockSpec((1, 1, 2 * tc), lambda i: (jnp.minimum(i + 1, nt - 1), 0, 0), memory_space=pltpu.SMEM),
                  pl.BlockSpec((tc, d), lambda i: (i, 0)),
                  pl.BlockSpec((1, 8, d), mod_map),
                  pl.BlockSpec((tc, 8), lambda i: (i, 0)),
                  _const_spec((1, d)),
                  pl.BlockSpec(memory_space=pl.ANY)],
        out_specs=pl.BlockSpec((tc, d), lambda i: (i, 0)),
        out_shape=jax.ShapeDtypeStruct((ntok, d), F32),
        scratch_shapes=[pltpu.VMEM((2, 2, tc, d), F32), pltpu.SemaphoreType.DMA((2,))],
        compiler_params=_params("arbitrary"), name="combine",
    )(dest.reshape(nt, 1, 2 * tc), dest.reshape(nt, 1, 2 * tc), x1, mod, rw, fw, y)


def _moe(streams, cnt, mod, w1, w3, w2, layer, fw, final):
    counts = cnt[:, 0].astype(I32)
    padded = (counts + MOE_TILE - 1) // MOE_TILE * MOE_TILE
    pad_end = jnp.cumsum(padded)
    offs = pad_end - padded
    ntok = sum(s[0].shape[0] for s in streams)
    n_tiles = -(-2 * ntok // MOE_TILE) + N_EXPERTS
    tile_start = jnp.arange(n_tiles, dtype=I32) * MOE_TILE
    block_expert = jnp.minimum(jnp.sum((pad_end[None, :] <= tile_start[:, None]).astype(I32), axis=1), N_EXPERTS - 1)
    n_used = (pad_end[-1:] // MOE_TILE).astype(I32)
    dests = [(offs[s[2][:, 0:2]] + s[2][:, 2:4]).reshape(-1) for s in streams]
    tail = n_used + jnp.arange(N_EXPERTS, dtype=I32)
    ztiles = jnp.concatenate([jnp.where(padded > 0, pad_end // MOE_TILE - 1, -1),
                              jnp.where(tail < n_tiles, tail, -1)]).astype(I32)
    xs = _dispatch([s[1] for s in streams], jnp.concatenate(dests), ztiles, n_tiles * MOE_TILE)
    y = _ffn(xs, block_expert, n_used, w1, w3, w2, layer)
    return [_combine(x1, dest, rw, y, mod, mod_row, seq, fw, final)
            for (x1, _, _, rw, seq, mod_row), dest in zip(streams, dests)]


def _pad_rows(a, rows):
    return jnp.concatenate([a, jnp.zeros((rows - a.shape[0],) + a.shape[1:], a.dtype)], axis=0)


def _pad_cols(a, cols, value=0.0):
    return jnp.concatenate([a, jnp.full(a.shape[:-1] + (cols - a.shape[-1],), value, a.dtype)], axis=-1)


def _rope_tables(seq):
    rows = seq // GRID_W
    row = jnp.repeat(jnp.arange(rows, dtype=F32), GRID_W)
    col = jnp.tile(jnp.arange(GRID_W, dtype=F32), rows)
    pairs = HEAD_DIM // 4
    inv_freq = ROPE_THETA ** (-jnp.arange(pairs, dtype=F32) / pairs)
    ang = jnp.concatenate([row[:, None] * inv_freq, col[:, None] * inv_freq], axis=-1)
    cos, sin = jnp.cos(ang), jnp.sin(ang)
    cos_t = jnp.tile(cos, (1, LANES // (HEAD_DIM // 2)))
    sin_t = jnp.tile(jnp.concatenate([-sin, sin], axis=-1), (1, LANES // HEAD_DIM))
    return cos_t, sin_t


def kernel(x, c, ctx, c_ctx, norm1_w, norm2_w, w_ada, b_ada, w_in, attn_sink, ssm_conv_w, ssm_conv_b,
           ssm_dt_bias, ssm_a_log, ssm_d, ssm_norm_w, sconv_w, w_gate, b_gate, w_branch, w_o,
           w_router, router_bias, moe_w1, moe_w3, moe_w2, final_norm_w):
    nb, seq, d = x.shape
    lc = ctx.shape[1]
    depth = w_in.shape[0]
    n_lat = nb * seq
    n_ctx = nb * lc
    assert seq % 512 == 0 and seq % (FFT_N2 * 8) == 0 and lc % 256 == 0 and nb < 8

    xl = x.reshape(n_lat, d)
    xc = ctx.reshape(n_ctx, d)
    c8 = _pad_rows(jnp.concatenate([c, c_ctx[None]], axis=0), 8)
    mods = _ada(c8, w_ada, b_ada)
    mods = _pad_cols(mods.reshape(depth, 8, 6, d).swapaxes(-1, -2), 8).swapaxes(-1, -2)

    cos_l, sin_l = _rope_tables(seq)
    cos_c, sin_c = jnp.ones((lc, LANES), F32), jnp.zeros((lc, LANES), F32)

    wr = w_router.T.reshape(N_EXPERT_GROUPS, EXPERTS_PER_GROUP, d).transpose(1, 0, 2).reshape(N_EXPERTS, d)
    wr = wr.astype(BF16)
    rb = jnp.broadcast_to(router_bias.reshape(N_EXPERT_GROUPS, EXPERTS_PER_GROUP).T.reshape(N_EXPERTS, 1),
                          (N_EXPERTS, LANES))

    for layer in range(depth):
        last = layer + 1 == depth
        mod = mods[layer]
        wi = w_in[layer]
        w_pad = jnp.concatenate([wi[:, :768], _pad_cols(wi[:, 2048:2048 + DT_W], 2 * LANES), wi[:, 768:2048],
                                 wi[:, 2048 + DT_W:]], axis=1).astype(BF16)
        g1 = norm1_w[layer][None]
        g2 = norm2_w[layer][None]
        cw = _pad_rows(ssm_conv_w[layer], 8)
        cb = ssm_conv_b[layer][None]
        dtb = _pad_cols(ssm_dt_bias[layer].reshape(1, DT_W), LANES)
        alog = _pad_cols(ssm_a_log[layer].reshape(1, DT_W), LANES)
        dsk = jnp.repeat(ssm_d[layer], SSM_HEAD_DIM)[None]
        nw = ssm_norm_w[layer][None]
        sw = _pad_rows(sconv_w[layer], 8)
        wg = w_gate[layer].astype(BF16)
        bg = b_gate[layer][None]
        wb = w_branch[layer].astype(BF16)
        wo = w_o[layer].astype(BF16)
        sink = attn_sink[layer]

        q, k, v, z, xbc, dtr, fu, su = _inproj(xl, nb, seq, 512, mod, None, g1, w_pad, cos_l, sin_l)
        qc, kc, vc, zc, xbcc, dtrc, fuc, suc = _inproj(xc, nb, lc, 256, mod, nb, g1, w_pad, cos_c, sin_c)

        att = _attention(sink, q, k, v, kc, vc, 256, True)

        zero_state = jnp.zeros((nb, SSM_GROUPS, SSM_STATE, SSM_INNER // SSM_GROUPS), F32)
        ssd_c = functools.partial(_ssd, xbcc, dtrc, cw, cb, dtb, alog)
        ssd_l = functools.partial(_ssd, xbc, dtr, cw, cb, dtb, alog)
        if last:
            _, fin_f = ssd_c(zero_state, False, False)
            _, fin_b = ssd_c(zero_state, True, False)
        else:
            yc_f, fin_f = ssd_c(zero_state, False, True)
            ssm_c, fin_b = ssd_c(zero_state, True, True, fin=(yc_f, zc, dsk, nw))
        y_f, _ = ssd_l(fin_f, False, True)
        ssm, _ = ssd_l(fin_b, True, True, fin=(y_f, z, dsk, nw))

        fft = _fourier_long(fu)

        x1, h2, ri, rw, cnt = _merge(xl, nb, seq, 512, mod, None, g1, g2, att, ssm, fft, su, sw, wg, bg, wb, wo,
                                     wr, rb, jnp.zeros((N_EXPERTS, LANES), F32))
        streams = [(x1, h2, ri, rw, seq, None)]
        if not last:
            att_c = _attention(sink, qc, None, None, kc, vc, lc, False)
            fft_c = _fourier_short(fuc)
            x1c, h2c, ric, rwc, cnt = _merge(xc, nb, lc, 256, mod, nb, g1, g2, att_c, ssm_c, fft_c, suc, sw, wg,
                                             bg, wb, wo, wr, rb, cnt)
            streams.append((x1c, h2c, ric, rwc, lc, nb))
        new = _moe(streams, cnt, mod, moe_w1, moe_w3, moe_w2, layer, final_norm_w[None], last)
        xl = new[0]
        if not last:
            xc = new[1]
    return xl.reshape(nb, seq, d)
```

```python
import functools
import math

import numpy as np
import jax
import jax.numpy as jnp
from jax import lax
from jax.experimental import pallas as pl
from jax.experimental.pallas import tpu as pltpu

F32 = jnp.float32
BF16 = jnp.bfloat16
I32 = jnp.int32

EPS = 1e-6
GRID_W = 64
ROPE_THETA = 10000.0
HEAD_DIM = 64
ATT_HEADS = 8
ATT_KV_HEADS = 2
ATT_GROUP = ATT_HEADS // ATT_KV_HEADS
WINDOW = 128
SSM_HEADS = 8
SSM_HEAD_DIM = 64
SSM_INNER = SSM_HEADS * SSM_HEAD_DIM
SSM_GROUPS = 2
SSM_STATE = 64
SSM_BC_W = SSM_GROUPS * SSM_STATE
SSM_XBC_W = SSM_INNER + 2 * SSM_BC_W
SSM_CHUNK = 128
FNET_GROUPS = 8
FNET_GROUP_DIM = 64
FNET_WIDTH = FNET_GROUPS * FNET_GROUP_DIM
SCONV_WIDTH = 512
BRANCH_W = 512
N_BRANCHES = 4
N_EXPERTS = 32
N_EXPERT_GROUPS = 8
EXPERTS_PER_GROUP = N_EXPERTS // N_EXPERT_GROUPS
D_EXPERT = 512

LANES = 128
BF16_ROWS = 16
VMEM_LIMIT = 56 * 1024 * 1024
FFT_N2 = 128
F32_ROWS = 8
SSD_CHUNKS_PER_STEP = 8
MERGE_ROW_GROUP = 256
MOE_TILE = 256
ROW_DMA_UNROLL = 8

Q_OFF, K_OFF, Z_OFF, XBC_OFF, FU_OFF, SU_OFF = 0, 512, 1024, 1536, 2304, 2816
IN_W_PAD = SU_OFF + 3 * SCONV_WIDTH
DT_W = 2 * SSM_HEADS


def _dot(a, b):
    return jnp.dot(a.astype(BF16), b.astype(BF16), preferred_element_type=F32)


def _dot_nt(a, b):
    return lax.dot_general(a.astype(BF16), b.astype(BF16), (((1,), (1,)), ((), ())),
                           preferred_element_type=F32)


def _dot_split(a_bf16, x):
    hi = x.astype(BF16)
    rest = x - hi.astype(F32)
    mid = rest.astype(BF16)
    lo = (rest - mid.astype(F32)).astype(BF16)
    return sum(jnp.dot(a_bf16, t, preferred_element_type=F32) for t in (hi, mid, lo))


def _sigmoid(x):
    return 1.0 / (1.0 + jnp.exp(-x))


def _silu(x):
    return x * _sigmoid(x)


def _norm_mod(x, g, shift, scale):
    y = x * lax.rsqrt(jnp.mean(x * x, axis=-1, keepdims=True) + EPS) * g
    return y * (1.0 + scale) + shift


def _params(*sem):
    return pltpu.CompilerParams(dimension_semantics=sem, vmem_limit_bytes=VMEM_LIMIT)


def _const_spec(shape):
    n = len(shape)
    return pl.BlockSpec(shape, lambda *_: (0,) * n, pipeline_mode=pl.Buffered(1))


def _ada_kernel(c_ref, w_ref, b_ref, o_ref):
    o_ref[0] = _dot(_silu(c_ref[...]), w_ref[0]) + b_ref[0]


def _ada(c8, w_ada, b_ada):
    nl, d, w = w_ada.shape
    tn = 512
    return pl.pallas_call(
        _ada_kernel, grid=(nl, w // tn),
        in_specs=[pl.BlockSpec((8, d), lambda l, j: (0, 0)),
                  pl.BlockSpec((1, d, tn), lambda l, j: (l, 0, j)),
                  pl.BlockSpec((1, 1, tn), lambda l, j: (l, 0, j))],
        out_specs=pl.BlockSpec((1, 8, tn), lambda l, j: (l, 0, j)),
        out_shape=jax.ShapeDtypeStruct((nl, 8, w), F32),
        compiler_params=_params("arbitrary", "arbitrary"), name="ada",
    )(c8, w_ada, b_ada.reshape(nl, 1, w))


def _inproj_kernel(x_ref, mod_ref, g_ref, w_ref, cos_ref, sin_ref,
                   q_ref, k_ref, v_ref, z_ref, xbc_ref, dt_ref, fu_ref, su_ref):
    m = mod_ref[0]
    hb = _norm_mod(x_ref[...], g_ref[...], m[0:1], m[1:2]).astype(BF16)
    cos = cos_ref[...]
    sin = sin_ref[...]
    lane = lax.broadcasted_iota(I32, cos.shape, 1)
    first_half = (lane % HEAD_DIM) < HEAD_DIM // 2

    def proj(start, width):
        return jnp.dot(hb, w_ref[:, start:start + width], preferred_element_type=F32)

    def rope(r):
        rot = jnp.where(first_half, pltpu.roll(r, LANES - HEAD_DIM // 2, 1), pltpu.roll(r, HEAD_DIM // 2, 1))
        return r * cos + rot * sin

    scale = HEAD_DIM ** -0.5
    qq = proj(Q_OFF, ATT_HEADS * HEAD_DIM)
    for j in range(ATT_HEADS // 2):
        r = rope(qq[:, j * LANES:(j + 1) * LANES]) * scale
        q_ref[0, 2 * j] = r[:, :HEAD_DIM].astype(BF16)
        q_ref[0, 2 * j + 1] = r[:, HEAD_DIM:].astype(BF16)
    kvd = proj(K_OFF, 4 * LANES)
    r = rope(kvd[:, :LANES])
    k_ref[0, 0] = r[:, :HEAD_DIM].astype(BF16)
    k_ref[0, 1] = r[:, HEAD_DIM:].astype(BF16)
    r = kvd[:, LANES:2 * LANES]
    v_ref[0, 0] = r[:, :HEAD_DIM].astype(BF16)
    v_ref[0, 1] = r[:, HEAD_DIM:].astype(BF16)
    dt_ref[0] = kvd[:, 2 * LANES:3 * LANES]
    z_ref[0] = proj(Z_OFF, SSM_INNER).astype(BF16)
    xbc_ref[0] = proj(XBC_OFF, SSM_XBC_W).astype(BF16)
    fu_ref[0] = proj(FU_OFF, FNET_WIDTH)
    su_ref[0] = proj(SU_OFF, 3 * SCONV_WIDTH).astype(BF16)


def _inproj(xflat, nb, seq, t, mod, mod_row, g, w, cos, sin):
    d = xflat.shape[1]
    nt = seq // t
    if mod_row is None:
        mod_map = lambda b, i: (b, 0, 0)
    else:
        mod_map = lambda b, i: (mod_row, 0, 0)
    sds = jax.ShapeDtypeStruct
    outs = (sds((nb, ATT_HEADS, seq, HEAD_DIM), BF16), sds((nb, ATT_KV_HEADS, seq, HEAD_DIM), BF16),
            sds((nb, ATT_KV_HEADS, seq, HEAD_DIM), BF16), sds((nb, seq, SSM_INNER), BF16),
            sds((nb, seq, SSM_XBC_W), BF16), sds((nb, seq, LANES), F32),
            sds((nb, seq, FNET_WIDTH), F32), sds((nb, seq, 3 * SCONV_WIDTH), BF16))
    head_spec = lambda nh: pl.BlockSpec((1, nh, t, HEAD_DIM), lambda b, i: (b, 0, i, 0))
    row_spec = lambda wd: pl.BlockSpec((1, t, wd), lambda b, i: (b, i, 0))
    return pl.pallas_call(
        _inproj_kernel, grid=(nb, nt),
        in_specs=[pl.BlockSpec((t, d), lambda b, i: (b * nt + i, 0)),
                  pl.BlockSpec((1, 8, d), mod_map),
                  _const_spec((1, d)), _const_spec((d, IN_W_PAD)),
                  pl.BlockSpec((t, LANES), lambda b, i: (i, 0)),
                  pl.BlockSpec((t, LANES), lambda b, i: (i, 0))],
        out_specs=(head_spec(ATT_HEADS), head_spec(ATT_KV_HEADS), head_spec(ATT_KV_HEADS),
                   row_spec(SSM_INNER), row_spec(SSM_XBC_W), row_spec(LANES),
                   row_spec(FNET_WIDTH), row_spec(3 * SCONV_WIDTH)),
        out_shape=outs, compiler_params=_params("arbitrary", "arbitrary"), name="inproj",
    )(xflat, mod, g, w, cos, sin)


def _attn_kernel(sink_ref, q_ref, *refs, tq, seq, band):
    if band:
        bias_ref, kp_ref, kc_ref, kn_ref, vp_ref, vc_ref, vn_ref, kx_ref, vx_ref, o_ref = refs
    else:
        kx_ref, vx_ref, o_ref = refs
    outs = []
    for kh in range(ATT_KV_HEADS):
        if band:
            keys = jnp.concatenate([kp_ref[0, kh], kc_ref[0, kh], kn_ref[0, kh], kx_ref[0, kh]], axis=0)
            vals = jnp.concatenate([vp_ref[0, kh], vc_ref[0, kh], vn_ref[0, kh], vx_ref[0, kh]], axis=0)
            bias = bias_ref[0]
        else:
            keys, vals = kx_ref[0, kh], vx_ref[0, kh]
        ones_col = (lax.broadcasted_iota(I32, vals.shape, 1) == 0).astype(BF16)
        v_ext = jnp.concatenate([vals, ones_col], axis=1)
        for g in range(ATT_GROUP):
            head = kh * ATT_GROUP + g
            s = _dot_nt(q_ref[0, head], keys)
            if band:
                s = s + bias
            sink = sink_ref[head]
            m = jnp.maximum(jnp.max(s, axis=1, keepdims=True), sink)
            acc = jnp.dot(jnp.exp(s - m).astype(BF16), v_ext, preferred_element_type=F32)
            den = acc[:, HEAD_DIM:HEAD_DIM + 1] + jnp.exp(sink - m)
            outs.append(acc[:, :HEAD_DIM] / den)
    o_ref[0] = jnp.concatenate(outs, axis=1).astype(BF16)


def _attention(sink, q, k, v, kx, vx, tq, band):
    nb, _, seq, _ = q.shape
    lc = kx.shape[2]
    nt = seq // tq
    r = tq // WINDOW
    nblk = seq // WINDOW
    kvh = ATT_KV_HEADS
    qs = pl.BlockSpec((1, ATT_HEADS, tq, HEAD_DIM), lambda b, i: (b, 0, i, 0))
    cur = pl.BlockSpec((1, kvh, tq, HEAD_DIM), lambda b, i: (b, 0, i, 0))
    prev = pl.BlockSpec((1, kvh, WINDOW, HEAD_DIM), lambda b, i: (b, 0, jnp.maximum(i * r - 1, 0), 0))
    nxt = pl.BlockSpec((1, kvh, WINDOW, HEAD_DIM), lambda b, i: (b, 0, jnp.minimum((i + 1) * r, nblk - 1), 0))
    ctx = pl.BlockSpec((1, kvh, lc, HEAD_DIM), lambda b, i: (b, 0, 0, 0))
    smem = pl.BlockSpec(memory_space=pltpu.SMEM)
    if band:
        nk = tq + 2 * WINDOW
        rel = np.arange(nk)[None, :] - WINDOW - np.arange(tq)[:, None]
        inside = np.abs(rel) <= WINDOW
        col = np.arange(nk)[None, :]
        kinds = [inside & ((col >= WINDOW) | (not first)) & ((col < tq + WINDOW) | (not lastt))
                 for lastt in (False, True) for first in (False, True)]
        kinds = np.concatenate([np.stack(kinds), np.ones((4, tq, lc), bool)], axis=2)
        bias = jnp.asarray(np.where(kinds, 0.0, -np.inf), F32)
        bias_spec = pl.BlockSpec((1, tq, nk + lc),
                                 lambda b, i: ((i == 0).astype(I32) + 2 * (i == nt - 1).astype(I32), 0, 0))
        in_specs = [smem, qs, bias_spec, prev, cur, nxt, prev, cur, nxt, ctx, ctx]
        args = (sink, q, bias, k, k, k, v, v, v, kx, vx)
    else:
        in_specs = [smem, qs, ctx, ctx]
        args = (sink, q, kx, vx)
    return pl.pallas_call(
        functools.partial(_attn_kernel, tq=tq, seq=seq, band=band),
        grid=(nb, nt), in_specs=in_specs,
        out_specs=pl.BlockSpec((1, tq, ATT_HEADS * HEAD_DIM), lambda b, i: (b, i, 0)),
        out_shape=jax.ShapeDtypeStruct((nb, seq, ATT_HEADS * HEAD_DIM), BF16),
        compiler_params=_params("arbitrary", "arbitrary"), name="attn_band" if band else "attn_ctx",
    )(*args)


def _ssd_kernel(*refs, reverse, with_y, finalize, nc, act_in, emit_act):
    refs = list(refs)
    if act_in:
        act_ref, dt_ref, dtb_ref, alog_ref, init_ref = refs[:5]
        refs = refs[5:]
    else:
        xp_ref, xc_ref, xn_ref, dt_ref, cw_ref, cb_ref, dtb_ref, alog_ref, init_ref = refs[:9]
        refs = refs[9:]
    if finalize:
        yf_ref, z_ref, dsk_ref, nw_ref = refs[:4]
        refs = refs[4:]
    if with_y:
        y_ref = refs.pop(0)
    if emit_act:
        act_out_ref = refs.pop(0)
    (st_ref,) = refs
    q = SSM_CHUNK
    rows = dt_ref.shape[1]
    c = pl.program_id(1)
    ce = (nc - 1 - c) if reverse else c

    @pl.when(c == 0)
    def _():
        st_ref[...] = init_ref[...]

    if act_in:
        act_all = act_ref[0].astype(F32)
    else:
        xc = xc_ref[0].astype(F32)
        prow = jnp.where(ce == 0, 0.0, xp_ref[0].astype(F32)[BF16_ROWS - 1:BF16_ROWS])
        nrow = jnp.where(ce == nc - 1, 0.0, xn_ref[0].astype(F32)[0:1])
        rid = lax.broadcasted_iota(I32, xc.shape, 0)
        up = jnp.where(rid == 0, prow, pltpu.roll(xc, 1, 0))
        dn = jnp.where(rid == rows - 1, nrow, pltpu.roll(xc, rows - 1, 0))
        cw = cw_ref[...]
        act_all = _silu(cw[0:1] * up + cw[1:2] * xc + cw[2:3] * dn + cb_ref[...])
        if emit_act:
            act_out_ref[0] = act_all.astype(BF16)
    pre = dt_ref[0] + dtb_ref[...]
    dt_all = jnp.maximum(pre, 0.0) + jnp.log1p(jnp.exp(-jnp.abs(pre)))
    neg_a = -jnp.exp(alog_ref[...])

    subs = range(rows // q)
    for sub in (reversed(subs) if reverse else subs):
        r0 = sub * q
        _ssd_chunk(act_all[r0:r0 + q], dt_all[r0:r0 + q], neg_a, st_ref,
                   (yf_ref[0, r0:r0 + q], z_ref[0, r0:r0 + q], dsk_ref[...], nw_ref[...]) if finalize else None,
                   y_ref.at[0, r0:r0 + q] if with_y else None, reverse)


def _ssd_chunk(act, dt, neg_a, st_ref, fin, y_out, reverse):
    q = SSM_CHUNK
    with_y = y_out is not None
    xs = act[:, :SSM_INNER]
    bmat = act[:, SSM_INNER:SSM_INNER + SSM_BC_W]
    cmat = act[:, SSM_INNER + SSM_BC_W:]
    dta = dt * neg_a
    ri = lax.broadcasted_iota(I32, (q, q), 0)
    ci = lax.broadcasted_iota(I32, (q, q), 1)
    tri = (ri <= ci) if reverse else (ri >= ci)
    acs = _dot_split(tri.astype(BF16), dta)
    col0 = SSM_HEADS if reverse else 0
    lo_half = lax.broadcasted_iota(I32, (q, LANES), 1) < SSM_HEAD_DIM

    def bcast(mat, h):
        return jnp.broadcast_to(mat[:, col0 + h:col0 + h + 1], (q, LANES))

    def lane_expand(cols):
        return jnp.concatenate([jnp.where(lo_half, cols[2 * j], cols[2 * j + 1]) for j in range(SSM_HEADS // 2)],
                               axis=1)

    acs_b = [bcast(acs, h) for h in range(SSM_HEADS)]
    acs_e = lane_expand(acs_b)
    xd = xs * lane_expand([bcast(dt, h) for h in range(SSM_HEADS)])
    tot_e = acs_e[0:1] if reverse else acs_e[q - 1:q]
    hpg = SSM_HEADS // SSM_GROUPS
    gw = hpg * SSM_HEAD_DIM

    def grp(mat, g):
        return mat[:, g * SSM_STATE:(g + 1) * SSM_STATE]

    if with_y:
        acs_t = acs.T
        cb = [_dot_nt(grp(cmat, g), grp(bmat, g)) for g in range(SSM_GROUPS)]
        y_diag = []
        for j in range(SSM_HEADS // 2):
            sc = []
            for h in (2 * j, 2 * j + 1):
                seg = acs_b[h] - acs_t[col0 + h:col0 + h + 1, :]
                sc.append((cb[h // hpg] * jnp.exp(jnp.where(tri, seg, -jnp.inf))).astype(BF16))
            slab = xd[:, j * LANES:(j + 1) * LANES].astype(BF16)
            rhs = jnp.concatenate([jnp.where(lo_half, slab, 0), jnp.where(lo_half, 0, slab)], axis=0)
            y_diag.append(jnp.dot(jnp.concatenate(sc, axis=1), rhs, preferred_element_type=F32))
        y_off = [_dot(grp(cmat, g), st_ref[0, g]) for g in range(SSM_GROUPS)]
        y = jnp.concatenate(y_diag, axis=1) + jnp.concatenate(y_off, axis=1) * jnp.exp(acs_e)

    xdd = xd * jnp.exp(tot_e - acs_e)
    b_t = bmat.T
    for g in range(SSM_GROUPS):
        upd = _dot(b_t[g * SSM_STATE:(g + 1) * SSM_STATE, :], xdd[:, g * gw:(g + 1) * gw])
        st_ref[0, g] = st_ref[0, g] * jnp.exp(tot_e[:, g * gw:(g + 1) * gw]) + upd

    if fin is not None:
        y_fwd, z, dsk, nw = fin
        yt = y_fwd + y + dsk * xs
        yt = yt * _silu(z.astype(F32))
        yt = yt * lax.rsqrt(jnp.mean(yt * yt, axis=-1, keepdims=True) + EPS) * nw
        y_out[...] = yt.astype(BF16)
    elif with_y:
        y_out[...] = y


def _ssd(xbc, dtr, cw, cb, dtb, alog, init, reverse, with_y, fin=None, act=None, emit_act=False):
    nb, seq, _ = dtr.shape
    st_block = (1, SSM_GROUPS, SSM_STATE, SSM_INNER // SSM_GROUPS)
    blk = min(SSM_CHUNK * SSD_CHUNKS_PER_STEP, seq)
    nc = seq // blk
    per = blk // BF16_ROWS
    nhalo = seq // BF16_ROWS
    ce = (lambda c: nc - 1 - c) if reverse else (lambda c: c)
    xw = SSM_XBC_W
    blk_spec = lambda wd: pl.BlockSpec((1, blk, wd), lambda b, c: (b, ce(c), 0))
    tail_specs = [_const_spec((1, LANES)), _const_spec((1, LANES)), pl.BlockSpec(st_block, lambda b, c: (b, 0, 0, 0))]
    if act is not None:
        in_specs = [blk_spec(xw), blk_spec(LANES)] + tail_specs
        args = [act, dtr, dtb, alog, init]
    else:
        in_specs = [pl.BlockSpec((1, BF16_ROWS, xw), lambda b, c: (b, jnp.maximum(ce(c) * per - 1, 0), 0)),
                    blk_spec(xw),
                    pl.BlockSpec((1, BF16_ROWS, xw), lambda b, c: (b, jnp.minimum((ce(c) + 1) * per, nhalo - 1), 0)),
                    blk_spec(LANES), _const_spec((8, xw)), _const_spec((1, xw))] + tail_specs
        args = [xbc, xbc, xbc, dtr, cw, cb, dtb, alog, init]
    st_spec = pl.BlockSpec(st_block, lambda b, c: (b, 0, 0, 0))
    st_shape = jax.ShapeDtypeStruct((nb,) + st_block[1:], F32)
    y_spec = pl.BlockSpec((1, blk, SSM_INNER), lambda b, c: (b, ce(c), 0))
    finalize = fin is not None
    if finalize:
        yf, z, dsk, nw = fin
        in_specs += [y_spec, y_spec, _const_spec((1, SSM_INNER)), _const_spec((1, SSM_INNER))]
        args += [yf, z, dsk, nw]
    out_specs, out_shape = [], []
    if with_y:
        out_specs.append(y_spec)
        out_shape.append(jax.ShapeDtypeStruct((nb, seq, SSM_INNER), BF16 if finalize else F32))
    if emit_act:
        out_specs.append(blk_spec(xw))
        out_shape.append(jax.ShapeDtypeStruct((nb, seq, xw), BF16))
    out_specs.append(st_spec)
    out_shape.append(st_shape)
    res = pl.pallas_call(
        functools.partial(_ssd_kernel, reverse=reverse, with_y=with_y, finalize=finalize, nc=nc,
                          act_in=act is not None, emit_act=emit_act),
        grid=(nb, nc), in_specs=in_specs, out_specs=tuple(out_specs), out_shape=tuple(out_shape),
        compiler_params=_params("arbitrary", "arbitrary"), name="ssd_bwd" if reverse else "ssd_fwd",
    )(*args)
    return res


def _fft_a_kernel(u_ref, wc_ref, m1_ref, y_ref, wc_bf):
    @pl.when((pl.program_id(0) == 0) & (pl.program_id(1) == 0))
    def _():
        wc_bf[...] = wc_ref[...].astype(BF16)

    per = m1_ref.shape[0]
    u_all = jnp.concatenate([u_ref[0, :, j, :] for j in range(per)], axis=0)
    v_all = _dot(u_all, wc_bf[...])
    for j in range(per):
        v = v_all[j * FFT_N2:(j + 1) * FFT_N2]
        vs = jnp.concatenate([v[:, :FNET_WIDTH], v[:, FNET_WIDTH:]], axis=0)
        b = _dot(m1_ref[j], vs)
        y_ref[0, 0, j] = b[:FFT_N2]
        y_ref[0, 1, j] = b[FFT_N2:]


def _fft_c_kernel(y_ref, m2_ref, o_ref, m2_bf):
    @pl.when((pl.program_id(0) == 0) & (pl.program_id(1) == 0))
    def _():
        m2_bf[...] = m2_ref[...].astype(BF16)

    rows = y_ref.shape[1] * y_ref.shape[2] * y_ref.shape[4]
    y2d = y_ref[0].reshape(rows, FNET_WIDTH).astype(BF16)
    o_ref[0] = jnp.dot(m2_bf[...], y2d, preferred_element_type=F32).reshape(o_ref.shape[1:])


def _fft_small_kernel(u_ref, wc_ref, m_ref, o_ref):
    v = _dot(u_ref[0], wc_ref[...])
    vs = jnp.concatenate([v[:, :FNET_WIDTH], v[:, FNET_WIDTH:]], axis=0)
    o_ref[0] = _dot(m_ref[...], vs).astype(BF16)


def _channel_dft():
    idx = np.arange(FNET_GROUP_DIM)
    ang = 2.0 * np.pi * np.outer(idx, idx) / FNET_GROUP_DIM
    eye = np.eye(FNET_GROUPS)
    return jnp.asarray(np.concatenate([np.kron(eye, np.cos(ang)), -np.kron(eye, np.sin(ang))], axis=1), F32)


def _fourier_long(u):
    nb, seq, w = u.shape
    n2 = FFT_N2
    n1 = seq // n2
    t2 = np.arange(n2)
    k2 = np.arange(n2)
    t1 = np.arange(n1)
    ang = 2.0 * np.pi * (np.outer(k2, t2)[None] / n2 + (t1[:, None, None] * k2[None, :, None]) / seq)
    co, si = np.cos(ang), np.sin(ang)
    m1 = jnp.asarray(np.concatenate([np.concatenate([co, si], axis=2),
                                     np.concatenate([-si, co], axis=2)], axis=1), F32)
    ang1 = 2.0 * np.pi * np.outer(t1, t1) / n1
    norm = 1.0 / math.sqrt(seq * FNET_GROUP_DIM)
    m2 = np.concatenate([np.cos(ang1), np.sin(ang1)], axis=1) * norm
    m2k = jnp.asarray(np.kron(m2, np.eye(F32_ROWS)), F32)
    per = F32_ROWS
    y = pl.pallas_call(
        _fft_a_kernel, grid=(n1 // per, nb),
        in_specs=[pl.BlockSpec((1, n2, per, w), lambda j, b: (b, 0, j, 0)),
                  _const_spec((w, 2 * w)),
                  pl.BlockSpec((per, 2 * n2, 2 * n2), lambda j, b: (j, 0, 0))],
        out_specs=pl.BlockSpec((1, 2, per, n2, w), lambda j, b: (b, 0, j, 0, 0)),
        out_shape=jax.ShapeDtypeStruct((nb, 2, n1, n2, w), F32),
        scratch_shapes=[pltpu.VMEM((w, 2 * w), BF16)],
        compiler_params=_params("arbitrary", "arbitrary"), name="fft_a",
    )(u.reshape(nb, n2, n1, w), _channel_dft(), m1)
    out = pl.pallas_call(
        _fft_c_kernel, grid=(nb, n2 // per),
        in_specs=[pl.BlockSpec((1, 2, n1, 1, per, w), lambda b, j: (b, 0, 0, j, 0, 0)),
                  _const_spec((n1 * per, 2 * n1 * per))],
        out_specs=pl.BlockSpec((1, n1, 1, per, w), lambda b, j: (b, 0, j, 0, 0)),
        out_shape=jax.ShapeDtypeStruct((nb, n1, n2 // per, per, w), F32),
        scratch_shapes=[pltpu.VMEM((n1 * per, 2 * n1 * per), BF16)],
        compiler_params=_params("arbitrary", "arbitrary"), name="fft_c",
    )(y.reshape(nb, 2, n1, n2 // per, per, w), m2k)
    return out.reshape(nb, seq, w)


def _fourier_short(u):
    nb, seq, w = u.shape
    t = np.arange(seq)
    ang = 2.0 * np.pi * np.outer(t, t) / seq
    norm = 1.0 / math.sqrt(seq * FNET_GROUP_DIM)
    m = jnp.asarray(np.concatenate([np.cos(ang), np.sin(ang)], axis=1) * norm, F32)
    return pl.pallas_call(
        _fft_small_kernel, grid=(nb,),
        in_specs=[pl.BlockSpec((1, seq, w), lambda b: (b, 0, 0)),
                  _const_spec((w, 2 * w)), _const_spec((seq, 2 * seq))],
        out_specs=pl.BlockSpec((1, seq, w), lambda b: (b, 0, 0)),
        out_shape=jax.ShapeDtypeStruct((nb, seq, w), BF16),
        compiler_params=_params("arbitrary"), name="fft_small",
    )(u, _channel_dft(), m)


def _merge_kernel(x_ref, mod_ref, g1_ref, g2_ref, att_ref, ssm_ref, fft_ref, sup_ref, su_ref, sun_ref,
                  sw_ref, wg_ref, bg_ref, wb_ref, wo_ref, wr_ref, rb_ref, cin_ref,
                  x1_ref, h2_ref, ri_ref, rw_ref, cnt_ref, *, nt):
    d = x_ref.shape[1]
    t = x_ref.shape[0]
    i = pl.program_id(1)

    @pl.when((pl.program_id(0) == 0) & (i == 0))
    def _():
        cnt_ref[...] = cin_ref[...]

    m = mod_ref[0]

    def gated(su):
        return su[:, SCONV_WIDTH:2 * SCONV_WIDTH] * su[:, 2 * SCONV_WIDTH:]

    su = su_ref[0].astype(F32)
    p = gated(su)
    prow = jnp.where(i == 0, 0.0, gated(sup_ref[0].astype(F32)[BF16_ROWS - 1:BF16_ROWS]))
    nrow = jnp.where(i == nt - 1, 0.0, gated(sun_ref[0].astype(F32)[0:1]))
    rid = lax.broadcasted_iota(I32, p.shape, 0)
    up = jnp.where(rid == 0, prow, pltpu.roll(p, 1, 0))
    dn = jnp.where(rid == t - 1, nrow, pltpu.roll(p, t - 1, 0))
    sw = sw_ref[...]
    sconv = su[:, :SCONV_WIDTH] * (sw[0:1] * up + sw[1:2] * p + sw[2:3] * dn)

    rg = min(t, MERGE_ROW_GROUP)
    h2_parts = []
    for a in range(0, t, rg):
        x = x_ref[a:a + rg]
        hb = _norm_mod(x, g1_ref[...], m[0:1], m[1:2]).astype(BF16)
        branches = (att_ref[0, a:a + rg], ssm_ref[0, a:a + rg], fft_ref[0, a:a + rg].astype(BF16),
                    sconv[a:a + rg].astype(BF16))
        acc = jnp.zeros((rg, d), F32)
        for n in range(N_BRANCHES):
            gate = _sigmoid(jnp.dot(hb, wg_ref[:, n * d:(n + 1) * d], preferred_element_type=F32)
                            + bg_ref[:, n * d:(n + 1) * d])
            acc = acc + gate * jnp.dot(branches[n], wb_ref[n], preferred_element_type=F32)
        y = jnp.dot(acc.astype(BF16), wo_ref[...], preferred_element_type=F32)
        x1 = x + m[2:3] * y
        x1_ref[a:a + rg] = x1
        h2 = _norm_mod(x1, g2_ref[...], m[3:4], m[4:5])
        h2_ref[a:a + rg] = h2
        h2_parts.append(h2)
    _route_tile(jnp.concatenate(h2_parts, axis=0), wr_ref, rb_ref, ri_ref, rw_ref, cnt_ref)


def _merge(xflat, nb, seq, t, mod, mod_row, g1, g2, att, ssm, fft, su, sw, wg, bg, wb, wo, wr, rb, cnt_in):
    ntok, d = xflat.shape
    nt = seq // t
    per = t // BF16_ROWS
    nhalo = seq // BF16_ROWS
    mod_map = (lambda b, i: (b, 0, 0)) if mod_row is None else (lambda b, i: (mod_row, 0, 0))
    row = pl.BlockSpec((t, d), lambda b, i: (b * nt + i, 0))
    br = pl.BlockSpec((1, t, BRANCH_W), lambda b, i: (b, i, 0))
    suw = 3 * SCONV_WIDTH
    in_specs = [row, pl.BlockSpec((1, 8, d), mod_map), _const_spec((1, d)), _const_spec((1, d)),
                br, br, br,
                pl.BlockSpec((1, BF16_ROWS, suw), lambda b, i: (b, jnp.maximum(i * per - 1, 0), 0)),
                pl.BlockSpec((1, t, suw), lambda b, i: (b, i, 0)),
                pl.BlockSpec((1, BF16_ROWS, suw), lambda b, i: (b, jnp.minimum((i + 1) * per, nhalo - 1), 0)),
                _const_spec((8, SCONV_WIDTH)), _const_spec((d, N_BRANCHES * d)), _const_spec((1, N_BRANCHES * d)),
                _const_spec((N_BRANCHES, BRANCH_W, d)), _const_spec((d, d)),
                _const_spec((N_EXPERTS, d)), _const_spec((N_EXPERTS, t)), _const_spec((N_EXPERTS, LANES))]
    tile_rows = pl.BlockSpec((1, 8, t), lambda b, i: (b * nt + i, 0, 0))
    x1, h2, ri, rw, cnt = pl.pallas_call(
        functools.partial(_merge_kernel, nt=nt),
        grid=(nb, nt), in_specs=in_specs,
        out_specs=(row, row, tile_rows, tile_rows, pl.BlockSpec((N_EXPERTS, LANES), lambda b, i: (0, 0))),
        out_shape=(jax.ShapeDtypeStruct((ntok, d), F32), jax.ShapeDtypeStruct((ntok, d), F32),
                   jax.ShapeDtypeStruct((ntok // t, 8, t), I32), jax.ShapeDtypeStruct((ntok // t, 8, t), F32),
                   jax.ShapeDtypeStruct((N_EXPERTS, LANES), F32)),
        compiler_params=_params("arbitrary", "arbitrary"), name="merge",
    )(xflat, mod, g1, g2, att, ssm, fft, su, su, su, sw, wg, bg, wb, wo, wr, jnp.tile(rb, (1, t // LANES)), cnt_in)
    return x1, h2, ri.transpose(0, 2, 1).reshape(ntok, 8), rw.transpose(0, 2, 1).reshape(ntok, 8), cnt


def _route_tile(h2, wr_ref, rb_ref, ri_ref, rw_ref, cnt_ref):
    t = h2.shape[0]
    ng = N_EXPERT_GROUPS
    sc = _sigmoid(_dot_nt(wr_ref[...], h2))
    sel = sc + rb_ref[...]
    s = [sel[j * ng:(j + 1) * ng] for j in range(EXPERTS_PER_GROUP)]
    u = [sc[j * ng:(j + 1) * ng] for j in range(EXPERTS_PER_GROUP)]
    gs = None
    for a in range(EXPERTS_PER_GROUP):
        for b in range(a + 1, EXPERTS_PER_GROUP):
            pair = s[a] + s[b]
            gs = pair if gs is None else jnp.maximum(gs, pair)
    grp = lax.broadcasted_iota(I32, (ng, t), 0)
    gmax = jnp.max(gs, axis=0, keepdims=True)
    best = jnp.min(jnp.where(gs == gmax, grp, ng), axis=0, keepdims=True)
    on = grp == best
    v = [jnp.sum(jnp.where(on, s[j], 0.0), axis=0, keepdims=True) for j in range(EXPERTS_PER_GROUP)]
    w = [jnp.sum(jnp.where(on, u[j], 0.0), axis=0, keepdims=True) for j in range(EXPERTS_PER_GROUP)]

    def first_argmax(vals):
        mx = vals[0]
        for x in vals[1:]:
            mx = jnp.maximum(mx, x)
        idx = jnp.full_like(best, len(vals) - 1)
        for j in range(len(vals) - 2, -1, -1):
            idx = jnp.where(vals[j] == mx, j, idx)
        return idx

    def pick(vals, idx):
        out = vals[-1]
        for j in range(len(vals) - 2, -1, -1):
            out = jnp.where(idx == j, vals[j], out)
        return out

    l1 = first_argmax(v)
    l2 = first_argmax([jnp.where(l1 == j, -jnp.inf, v[j]) for j in range(EXPERTS_PER_GROUP)])
    w1, w2 = pick(w, l1), pick(w, l2)
    tot = w1 + w2
    e1 = best * EXPERTS_PER_GROUP + l1
    e2 = best * EXPERTS_PER_GROUP + l2

    eid = lax.broadcasted_iota(I32, (N_EXPERTS, t), 0)
    oh1 = (eid == e1).astype(F32)
    oh2 = (eid == e2).astype(F32)
    oh = oh1 + oh2
    ri_ = lax.broadcasted_iota(I32, (t, t), 0)
    ci_ = lax.broadcasted_iota(I32, (t, t), 1)
    cnt = cnt_ref[...]
    before = _dot(oh, (ri_ < ci_).astype(F32)) + jnp.tile(cnt, (1, t // LANES))
    r1 = jnp.sum(oh1 * before, axis=0, keepdims=True).astype(I32)
    r2 = jnp.sum(oh2 * before, axis=0, keepdims=True).astype(I32)
    cnt_ref[...] = cnt + jnp.sum(oh, axis=1, keepdims=True)

    row = lax.broadcasted_iota(I32, (8, t), 0)
    ri_ref[0] = jnp.where(row == 0, e1, jnp.where(row == 1, e2, jnp.where(row == 2, r1, jnp.where(row == 3, r2, 0))))
    rw_ref[0] = jnp.where(row == 0, w1 / tot, jnp.where(row == 1, w2 / tot, 0.0))


def _dispatch_kernel(dest_ref, zt_ref, *refs, td, starts):
    n_streams = len(starts) - 1
    h_refs = refs[:n_streams]
    xs_hbm, zbuf, zsem, sem = refs[n_streams:]
    i = pl.program_id(0)

    @pl.when(i == 0)
    def _():
        zbuf[...] = jnp.zeros_like(zbuf)

        def zero_copy(j):
            start = pl.multiple_of(zt_ref[j] * MOE_TILE, MOE_TILE)
            return pltpu.make_async_copy(zbuf, xs_hbm.at[pl.ds(start, MOE_TILE)], zsem)

        def zissue(j, carry):
            @pl.when(zt_ref[j] >= 0)
            def _():
                zero_copy(j).start()
            return carry

        def zdrain(j, carry):
            @pl.when(zt_ref[j] >= 0)
            def _():
                zero_copy(j).wait()
            return carry

        lax.fori_loop(0, zt_ref.shape[0], zissue, 0)
        lax.fori_loop(0, zt_ref.shape[0], zdrain, 0)

    for s in range(n_streams):
        h_ref = h_refs[s]

        def row_copy(src_row, dst_row, h_ref=h_ref):
            return pltpu.make_async_copy(h_ref.at[pl.ds(src_row, 1)], xs_hbm.at[pl.ds(dst_row, 1)], sem)

        @pl.when((i >= starts[s]) & (i < starts[s + 1]))
        def _(row_copy=row_copy):
            def issue(j, carry):
                row_copy(j, dest_ref[0, 0, 2 * j]).start()
                row_copy(j, dest_ref[0, 0, 2 * j + 1]).start()
                return carry

            lax.fori_loop(0, td, issue, 0, unroll=ROW_DMA_UNROLL)
            all_rows = xs_hbm.at[pl.ds(0, 2 * td)]
            pltpu.make_async_copy(all_rows, all_rows, sem).wait()


def _dispatch(h2s, dest, ztiles, nrows):
    d = h2s[0].shape[1]
    td = 256
    starts = [0]
    for h in h2s:
        starts.append(starts[-1] + h.shape[0] // td)
    nt = starts[-1]

    def stream_spec(s):
        return pl.BlockSpec((td, d), lambda i: (jnp.clip(i - starts[s], 0, starts[s + 1] - starts[s] - 1), 0))

    return pl.pallas_call(
        functools.partial(_dispatch_kernel, td=td, starts=tuple(starts)), grid=(nt,),
        in_specs=[pl.BlockSpec((1, 1, 2 * td), lambda i: (i, 0, 0), memory_space=pltpu.SMEM),
                  pl.BlockSpec(memory_space=pltpu.SMEM)] + [stream_spec(s) for s in range(len(h2s))],
        out_specs=pl.BlockSpec(memory_space=pl.ANY),
        out_shape=jax.ShapeDtypeStruct((nrows, d), F32),
        scratch_shapes=[pltpu.VMEM((MOE_TILE, d), F32), pltpu.SemaphoreType.DMA(()), pltpu.SemaphoreType.DMA(())],
        compiler_params=_params("arbitrary"), name="dispatch",
    )(dest.reshape(nt, 1, 2 * td), ztiles, *h2s)


def _ffn_kernel(be_ref, nu_ref, x_ref, w1_ref, w3_ref, w2_ref, y_ref, w1_bf, w3_bf, w2_bf):
    i = pl.program_id(0)
    used = i < nu_ref[0]

    @pl.when(used & ((i == 0) | (be_ref[i] != be_ref[jnp.maximum(i - 1, 0)])))
    def _():
        w1_bf[...] = w1_ref[0, 0].astype(BF16)
        w3_bf[...] = w3_ref[0, 0].astype(BF16)
        w2_bf[...] = w2_ref[0, 0].astype(BF16)

    @pl.when(used)
    def _():
        xb = x_ref[...].astype(BF16)
        a = jnp.dot(xb, w1_bf[...], preferred_element_type=F32)
        b = jnp.dot(xb, w3_bf[...], preferred_element_type=F32)
        y_ref[...] = jnp.dot((_silu(a) * b).astype(BF16), w2_bf[...], preferred_element_type=F32)

    @pl.when(i >= nu_ref[0])
    def _():
        y_ref[...] = jnp.zeros_like(y_ref)


def _ffn(xs, block_expert, n_used, w1, w3, w2, layer):
    nrows, d = xs.shape
    de = w1.shape[3]
    tm = MOE_TILE
    grid_spec = pltpu.PrefetchScalarGridSpec(
        num_scalar_prefetch=2, grid=(nrows // tm,),
        in_specs=[pl.BlockSpec((tm, d), lambda i, be, nu: (jnp.minimum(i, nu[0] - 1), 0)),
                  pl.BlockSpec((1, 1, d, de), lambda i, be, nu: (layer, be[i], 0, 0)),
                  pl.BlockSpec((1, 1, d, de), lambda i, be, nu: (layer, be[i], 0, 0)),
                  pl.BlockSpec((1, 1, de, d), lambda i, be, nu: (layer, be[i], 0, 0))],
        out_specs=pl.BlockSpec((tm, d), lambda i, be, nu: (i, 0)),
        scratch_shapes=[pltpu.VMEM((d, de), BF16), pltpu.VMEM((d, de), BF16), pltpu.VMEM((de, d), BF16)])
    return pl.pallas_call(
        _ffn_kernel, grid_spec=grid_spec, out_shape=jax.ShapeDtypeStruct((nrows, d), F32),
        compiler_params=_params("arbitrary"), name="ffn",
    )(block_expert, n_used, xs, w1, w3, w2)


def _combine_kernel(dest_ref, dnext_ref, x_ref, mod_ref, rw_ref, fw_ref, y_hbm, o_ref, buf, sems, *, tc, final):
    i = pl.program_id(0)
    nt = pl.num_programs(0)
    slot = i % 2

    def gather(d_ref, s):
        def issue(j, carry):
            for k in range(2):
                pltpu.make_async_copy(y_hbm.at[pl.ds(d_ref[0, 0, 2 * j + k], 1)],
                                      buf.at[s, k, pl.ds(j, 1)], sems.at[s]).start()
            return carry
        lax.fori_loop(0, tc, issue, 0, unroll=ROW_DMA_UNROLL)

    @pl.when(i == 0)
    def _():
        gather(dest_ref, 0)

    @pl.when(i + 1 < nt)
    def _():
        gather(dnext_ref, 1 - slot)

    pltpu.make_async_copy(buf.at[slot], buf.at[slot], sems.at[slot]).wait()
    rw = rw_ref[...]
    f = rw[:, 0:1] * buf[slot, 0] + rw[:, 1:2] * buf[slot, 1]
    x = x_ref[...] + mod_ref[0][5:6] * f
    if final:
        x = x * lax.rsqrt(jnp.mean(x * x, axis=-1, keepdims=True) + EPS) * fw_ref[...]
    o_ref[...] = x


def _combine(x1, dest, rw, y, mod, mod_row, seq, fw, final):
    ntok, d = x1.shape
    tc = 256
    nt = ntok // tc
    per_seq = seq // tc
    mod_map = (lambda i: (i // per_seq, 0, 0)) if mod_row is None else (lambda i: (mod_row, 0, 0))
    return pl.pallas_call(
        functools.partial(_combine_kernel, tc=tc, final=final), grid=(nt,),
        in_specs=[pl.BlockSpec((1, 1, 2 * tc), lambda i: (i, 0, 0), memory_space=pltpu.SMEM),
                  pl.BlockSpec((1, 1, 2 * tc), lambda i: (jnp.minimum(i + 1, nt - 1), 0, 0), memory_space=pltpu.SMEM),
                  pl.BlockSpec((tc, d), lambda i: (i, 0)),
                  pl.BlockSpec((1, 8, d), mod_map),
                  pl.BlockSpec((tc, 8), lambda i: (i, 0)),
                  _const_spec((1, d)),
                  pl.BlockSpec(memory_space=pl.ANY)],
        out_specs=pl.BlockSpec((tc, d), lambda i: (i, 0)),
        out_shape=jax.ShapeDtypeStruct((ntok, d), F32),
        scratch_shapes=[pltpu.VMEM((2, 2, tc, d), F32), pltpu.SemaphoreType.DMA((2,))],
        compiler_params=_params("arbitrary"), name="combine",
    )(dest.reshape(nt, 1, 2 * tc), dest.reshape(nt, 1, 2 * tc), x1, mod, rw, fw, y)


def _moe(streams, cnt, mod, w1, w3, w2, layer, fw, final):
    counts = cnt[:, 0].astype(I32)
    padded = (counts + MOE_TILE - 1) // MOE_TILE * MOE_TILE
    pad_end = jnp.cumsum(padded)
    offs = pad_end - padded
    ntok = sum(s[0].shape[0] for s in streams)
    n_tiles = -(-2 * ntok // MOE_TILE) + N_EXPERTS
    tile_start = jnp.arange(n_tiles, dtype=I32) * MOE_TILE
    block_expert = jnp.minimum(jnp.sum((pad_end[None, :] <= tile_start[:, None]).astype(I32), axis=1), N_EXPERTS - 1)
    n_used = (pad_end[-1:] // MOE_TILE).astype(I32)
    dests = [(offs[s[2][:, 0:2]] + s[2][:, 2:4]).reshape(-1) for s in streams]
    tail = n_used + jnp.arange(N_EXPERTS, dtype=I32)
    ztiles = jnp.concatenate([jnp.where(padded > 0, pad_end // MOE_TILE - 1, -1),
                              jnp.where(tail < n_tiles, tail, -1)]).astype(I32)
    xs = _dispatch([s[1] for s in streams], jnp.concatenate(dests), ztiles, n_tiles * MOE_TILE)
    y = _ffn(xs, block_expert, n_used, w1, w3, w2, layer)
    return [_combine(x1, dest, rw, y, mod, mod_row, seq, fw, final)
            for (x1, _, _, rw, seq, mod_row), dest in zip(streams, dests)]


def _pad_rows(a, rows):
    return jnp.concatenate([a, jnp.zeros((rows - a.shape[0],) + a.shape[1:], a.dtype)], axis=0)


def _pad_cols(a, cols, value=0.0):
    return jnp.concatenate([a, jnp.full(a.shape[:-1] + (cols - a.shape[-1],), value, a.dtype)], axis=-1)


def _rope_tables(seq):
    rows = seq // GRID_W
    row = jnp.repeat(jnp.arange(rows, dtype=F32), GRID_W)
    col = jnp.tile(jnp.arange(GRID_W, dtype=F32), rows)
    pairs = HEAD_DIM // 4
    inv_freq = ROPE_THETA ** (-jnp.arange(pairs, dtype=F32) / pairs)
    ang = jnp.concatenate([row[:, None] * inv_freq, col[:, None] * inv_freq], axis=-1)
    cos, sin = jnp.cos(ang), jnp.sin(ang)
    cos_t = jnp.tile(cos, (1, LANES // (HEAD_DIM // 2)))
    sin_t = jnp.tile(jnp.concatenate([-sin, sin], axis=-1), (1, LANES // HEAD_DIM))
    return cos_t, sin_t


def kernel(x, c, ctx, c_ctx, norm1_w, norm2_w, w_ada, b_ada, w_in, attn_sink, ssm_conv_w, ssm_conv_b,
           ssm_dt_bias, ssm_a_log, ssm_d, ssm_norm_w, sconv_w, w_gate, b_gate, w_branch, w_o,
           w_router, router_bias, moe_w1, moe_w3, moe_w2, final_norm_w):
    nb, seq, d = x.shape
    lc = ctx.shape[1]
    depth = w_in.shape[0]
    n_lat = nb * seq
    n_ctx = nb * lc
    assert seq % 512 == 0 and seq % (FFT_N2 * 8) == 0 and lc % 256 == 0 and nb < 8

    xl = x.reshape(n_lat, d)
    xc = ctx.reshape(n_ctx, d)
    c8 = _pad_rows(jnp.concatenate([c, c_ctx[None]], axis=0), 8)
    mods = _ada(c8, w_ada, b_ada)
    mods = _pad_cols(mods.reshape(depth, 8, 6, d).swapaxes(-1, -2), 8).swapaxes(-1, -2)

    cos_l, sin_l = _rope_tables(seq)
    cos_c, sin_c = jnp.ones((lc, LANES), F32), jnp.zeros((lc, LANES), F32)

    wr = w_router.T.reshape(N_EXPERT_GROUPS, EXPERTS_PER_GROUP, d).transpose(1, 0, 2).reshape(N_EXPERTS, d)
    wr = wr.astype(BF16)
    rb = jnp.broadcast_to(router_bias.reshape(N_EXPERT_GROUPS, EXPERTS_PER_GROUP).T.reshape(N_EXPERTS, 1),
                          (N_EXPERTS, LANES))

    for layer in range(depth):
        last = layer + 1 == depth
        mod = mods[layer]
        wi = w_in[layer]
        w_pad = jnp.concatenate([wi[:, :768], _pad_cols(wi[:, 2048:2048 + DT_W], 2 * LANES), wi[:, 768:2048],
                                 wi[:, 2048 + DT_W:]], axis=1).astype(BF16)
        g1 = norm1_w[layer][None]
        g2 = norm2_w[layer][None]
        cw = _pad_rows(ssm_conv_w[layer], 8)
        cb = ssm_conv_b[layer][None]
        dtb = _pad_cols(ssm_dt_bias[layer].reshape(1, DT_W), LANES)
        alog = _pad_cols(ssm_a_log[layer].reshape(1, DT_W), LANES)
        dsk = jnp.repeat(ssm_d[layer], SSM_HEAD_DIM)[None]
        nw = ssm_norm_w[layer][None]
        sw = _pad_rows(sconv_w[layer], 8)
        wg = w_gate[layer].astype(BF16)
        bg = b_gate[layer][None]
        wb = w_branch[layer].astype(BF16)
        wo = w_o[layer].astype(BF16)
        sink = attn_sink[layer]

        q, k, v, z, xbc, dtr, fu, su = _inproj(xl, nb, seq, 512, mod, None, g1, w_pad, cos_l, sin_l)
        qc, kc, vc, zc, xbcc, dtrc, fuc, suc = _inproj(xc, nb, lc, 256, mod, nb, g1, w_pad, cos_c, sin_c)

        att = _attention(sink, q, k, v, kc, vc, 256, True)

        zero_state = jnp.zeros((nb, SSM_GROUPS, SSM_STATE, SSM_INNER // SSM_GROUPS), F32)
        ssd_c = functools.partial(_ssd, xbcc, dtrc, cw, cb, dtb, alog)
        ssd_l = functools.partial(_ssd, xbc, dtr, cw, cb, dtb, alog)
        if last:
            act_c, fin_f = ssd_c(zero_state, False, False, emit_act=True)
            (fin_b,) = ssd_c(zero_state, True, False, act=act_c)
        else:
            yc_f, act_c, fin_f = ssd_c(zero_state, False, True, emit_act=True)
            ssm_c, fin_b = ssd_c(zero_state, True, True, fin=(yc_f, zc, dsk, nw), act=act_c)
        y_f, act_l, _ = ssd_l(fin_f, False, True, emit_act=True)
        ssm, _ = ssd_l(fin_b, True, True, fin=(y_f, z, dsk, nw), act=act_l)

        fft = _fourier_long(fu)

        x1, h2, ri, rw, cnt = _merge(xl, nb, seq, 512, mod, None, g1, g2, att, ssm, fft, su, sw, wg, bg, wb, wo,
                                     wr, rb, jnp.zeros((N_EXPERTS, LANES), F32))
        streams = [(x1, h2, ri, rw, seq, None)]
        if not last:
            att_c = _attention(sink, qc, None, None, kc, vc, lc, False)
            fft_c = _fourier_short(fuc)
            x1c, h2c, ric, rwc, cnt = _merge(xc, nb, lc, 256, mod, nb, g1, g2, att_c, ssm_c, fft_c, suc, sw, wg,
                                             bg, wb, wo, wr, rb, cnt)
            streams.append((x1c, h2c, ric, rwc, lc, nb))
        new = _moe(streams, cnt, mod, moe_w1, moe_w3, moe_w2, layer, final_norm_w[None], last)
        xl = new[0]
        if not last:
            xc = new[1]
    return xl.reshape(nb, seq, d)
```

```python
import functools
import math

import numpy as np
import jax
import jax.numpy as jnp
from jax import lax
from jax.experimental import pallas as pl
from jax.experimental.pallas import tpu as pltpu

F32 = jnp.float32
BF16 = jnp.bfloat16
I32 = jnp.int32

EPS = 1e-6
GRID_W = 64
ROPE_THETA = 10000.0
HEAD_DIM = 64
ATT_HEADS = 8
ATT_KV_HEADS = 2
ATT_GROUP = ATT_HEADS // ATT_KV_HEADS
WINDOW = 128
SSM_HEADS = 8
SSM_HEAD_DIM = 64
SSM_INNER = SSM_HEADS * SSM_HEAD_DIM
SSM_GROUPS = 2
SSM_STATE = 64
SSM_BC_W = SSM_GROUPS * SSM_STATE
SSM_XBC_W = SSM_INNER + 2 * SSM_BC_W
SSM_CHUNK = 128
FNET_GROUPS = 8
FNET_GROUP_DIM = 64
FNET_WIDTH = FNET_GROUPS * FNET_GROUP_DIM
SCONV_WIDTH = 512
BRANCH_W = 512
N_BRANCHES = 4
N_EXPERTS = 32
N_EXPERT_GROUPS = 8
EXPERTS_PER_GROUP = N_EXPERTS // N_EXPERT_GROUPS
D_EXPERT = 512

LANES = 128
BF16_ROWS = 16
VMEM_LIMIT = 56 * 1024 * 1024
FFT_N2 = 128
F32_ROWS = 8
SSD_CHUNKS_PER_STEP = 8
MERGE_ROW_GROUP = 256
LATENT_TILE = 512
CTX_TILE = 256
ATT_TILE = 256
ADA_TILE = 512
ROW_TILE = 512
MOE_TILE = 256
ROW_DMA_UNROLL = 8

Q_OFF, K_OFF, Z_OFF, XBC_OFF, FU_OFF, SU_OFF = 0, 512, 1024, 1536, 2304, 2816
IN_W_PAD = SU_OFF + 3 * SCONV_WIDTH
DT_W = 2 * SSM_HEADS


def _dot(a, b):
    return jnp.dot(a.astype(BF16), b.astype(BF16), preferred_element_type=F32)


def _dot_nt(a, b):
    return lax.dot_general(a.astype(BF16), b.astype(BF16), (((1,), (1,)), ((), ())),
                           preferred_element_type=F32)


def _dot_split(a_bf16, x):
    hi = x.astype(BF16)
    rest = x - hi.astype(F32)
    mid = rest.astype(BF16)
    lo = (rest - mid.astype(F32)).astype(BF16)
    return sum(jnp.dot(a_bf16, t, preferred_element_type=F32) for t in (hi, mid, lo))


def _sigmoid(x):
    return 1.0 / (1.0 + jnp.exp(-x))


def _silu(x):
    return x * _sigmoid(x)


def _norm_mod(x, g, shift, scale):
    y = x * lax.rsqrt(jnp.mean(x * x, axis=-1, keepdims=True) + EPS) * g
    return y * (1.0 + scale) + shift


def _params(*sem):
    return pltpu.CompilerParams(dimension_semantics=sem, vmem_limit_bytes=VMEM_LIMIT)


def _const_spec(shape):
    n = len(shape)
    return pl.BlockSpec(shape, lambda *_: (0,) * n, pipeline_mode=pl.Buffered(1))


def _ada_kernel(c_ref, w_ref, b_ref, o_ref):
    o_ref[0] = _dot(_silu(c_ref[...]), w_ref[0]) + b_ref[0]


def _ada(c8, w_ada, b_ada):
    nl, d, w = w_ada.shape
    tn = ADA_TILE
    return pl.pallas_call(
        _ada_kernel, grid=(nl, w // tn),
        in_specs=[pl.BlockSpec((8, d), lambda l, j: (0, 0)),
                  pl.BlockSpec((1, d, tn), lambda l, j: (l, 0, j)),
                  pl.BlockSpec((1, 1, tn), lambda l, j: (l, 0, j))],
        out_specs=pl.BlockSpec((1, 8, tn), lambda l, j: (l, 0, j)),
        out_shape=jax.ShapeDtypeStruct((nl, 8, w), F32),
        compiler_params=_params("arbitrary", "arbitrary"), name="ada",
    )(c8, w_ada, b_ada.reshape(nl, 1, w))


def _inproj_kernel(x_ref, mod_ref, g_ref, w_ref, cos_ref, sin_ref,
                   q_ref, k_ref, v_ref, z_ref, xbc_ref, dt_ref, fu_ref, su_ref):
    m = mod_ref[0]
    hb = _norm_mod(x_ref[...], g_ref[...], m[0:1], m[1:2]).astype(BF16)
    cos = cos_ref[...]
    sin = sin_ref[...]
    lane = lax.broadcasted_iota(I32, cos.shape, 1)
    first_half = (lane % HEAD_DIM) < HEAD_DIM // 2

    def proj(start, width):
        return jnp.dot(hb, w_ref[:, start:start + width], preferred_element_type=F32)

    def rope(r):
        rot = jnp.where(first_half, pltpu.roll(r, LANES - HEAD_DIM // 2, 1), pltpu.roll(r, HEAD_DIM // 2, 1))
        return r * cos + rot * sin

    scale = HEAD_DIM ** -0.5
    qq = proj(Q_OFF, ATT_HEADS * HEAD_DIM)
    for j in range(ATT_HEADS // 2):
        r = rope(qq[:, j * LANES:(j + 1) * LANES]) * scale
        q_ref[0, 2 * j] = r[:, :HEAD_DIM].astype(BF16)
        q_ref[0, 2 * j + 1] = r[:, HEAD_DIM:].astype(BF16)
    kvd = proj(K_OFF, 4 * LANES)
    r = rope(kvd[:, :LANES])
    k_ref[0, 0] = r[:, :HEAD_DIM].astype(BF16)
    k_ref[0, 1] = r[:, HEAD_DIM:].astype(BF16)
    r = kvd[:, LANES:2 * LANES]
    v_ref[0, 0] = r[:, :HEAD_DIM].astype(BF16)
    v_ref[0, 1] = r[:, HEAD_DIM:].astype(BF16)
    dt_ref[0] = kvd[:, 2 * LANES:3 * LANES]
    z_ref[0] = proj(Z_OFF, SSM_INNER).astype(BF16)
    xbc_ref[0] = proj(XBC_OFF, SSM_XBC_W).astype(BF16)
    fu_ref[0] = proj(FU_OFF, FNET_WIDTH)
    su_ref[0] = proj(SU_OFF, 3 * SCONV_WIDTH).astype(BF16)


def _inproj(xflat, nb, seq, t, mod, mod_row, g, w, cos, sin):
    d = xflat.shape[1]
    nt = seq // t
    if mod_row is None:
        mod_map = lambda b, i: (b, 0, 0)
    else:
        mod_map = lambda b, i: (mod_row, 0, 0)
    sds = jax.ShapeDtypeStruct
    outs = (sds((nb, ATT_HEADS, seq, HEAD_DIM), BF16), sds((nb, ATT_KV_HEADS, seq, HEAD_DIM), BF16),
            sds((nb, ATT_KV_HEADS, seq, HEAD_DIM), BF16), sds((nb, seq, SSM_INNER), BF16),
            sds((nb, seq, SSM_XBC_W), BF16), sds((nb, seq, LANES), F32),
            sds((nb, seq, FNET_WIDTH), F32), sds((nb, seq, 3 * SCONV_WIDTH), BF16))
    head_spec = lambda nh: pl.BlockSpec((1, nh, t, HEAD_DIM), lambda b, i: (b, 0, i, 0))
    row_spec = lambda wd: pl.BlockSpec((1, t, wd), lambda b, i: (b, i, 0))
    return pl.pallas_call(
        _inproj_kernel, grid=(nb, nt),
        in_specs=[pl.BlockSpec((t, d), lambda b, i: (b * nt + i, 0)),
                  pl.BlockSpec((1, 8, d), mod_map),
                  _const_spec((1, d)), _const_spec((d, IN_W_PAD)),
                  pl.BlockSpec((t, LANES), lambda b, i: (i, 0)),
                  pl.BlockSpec((t, LANES), lambda b, i: (i, 0))],
        out_specs=(head_spec(ATT_HEADS), head_spec(ATT_KV_HEADS), head_spec(ATT_KV_HEADS),
                   row_spec(SSM_INNER), row_spec(SSM_XBC_W), row_spec(LANES),
                   row_spec(FNET_WIDTH), row_spec(3 * SCONV_WIDTH)),
        out_shape=outs, compiler_params=_params("arbitrary", "arbitrary"), name="inproj",
    )(xflat, mod, g, w, cos, sin)


def _attn_kernel(sink_ref, q_ref, *refs, band):
    if band:
        bias_ref, kp_ref, kc_ref, kn_ref, vp_ref, vc_ref, vn_ref, kx_ref, vx_ref, o_ref = refs
    else:
        kx_ref, vx_ref, o_ref = refs
    outs = []
    for kh in range(ATT_KV_HEADS):
        if band:
            keys = jnp.concatenate([kp_ref[0, kh], kc_ref[0, kh], kn_ref[0, kh], kx_ref[0, kh]], axis=0)
            vals = jnp.concatenate([vp_ref[0, kh], vc_ref[0, kh], vn_ref[0, kh], vx_ref[0, kh]], axis=0)
            bias = bias_ref[0]
        else:
            keys, vals = kx_ref[0, kh], vx_ref[0, kh]
        ones_col = (lax.broadcasted_iota(I32, vals.shape, 1) == 0).astype(BF16)
        v_ext = jnp.concatenate([vals, ones_col], axis=1)
        for g in range(ATT_GROUP):
            head = kh * ATT_GROUP + g
            s = _dot_nt(q_ref[0, head], keys)
            if band:
                s = s + bias
            sink = sink_ref[head]
            m = jnp.maximum(jnp.max(s, axis=1, keepdims=True), sink)
            acc = jnp.dot(jnp.exp(s - m).astype(BF16), v_ext, preferred_element_type=F32)
            den = acc[:, HEAD_DIM:HEAD_DIM + 1] + jnp.exp(sink - m)
            outs.append(acc[:, :HEAD_DIM] / den)
    o_ref[0] = jnp.concatenate(outs, axis=1).astype(BF16)


def _attention(sink, q, k, v, kx, vx, tq, band):
    nb, _, seq, _ = q.shape
    lc = kx.shape[2]
    nt = seq // tq
    r = tq // WINDOW
    nblk = seq // WINDOW
    kvh = ATT_KV_HEADS
    qs = pl.BlockSpec((1, ATT_HEADS, tq, HEAD_DIM), lambda b, i: (b, 0, i, 0))
    cur = pl.BlockSpec((1, kvh, tq, HEAD_DIM), lambda b, i: (b, 0, i, 0))
    prev = pl.BlockSpec((1, kvh, WINDOW, HEAD_DIM), lambda b, i: (b, 0, jnp.maximum(i * r - 1, 0), 0))
    nxt = pl.BlockSpec((1, kvh, WINDOW, HEAD_DIM), lambda b, i: (b, 0, jnp.minimum((i + 1) * r, nblk - 1), 0))
    ctx = pl.BlockSpec((1, kvh, lc, HEAD_DIM), lambda b, i: (b, 0, 0, 0))
    smem = pl.BlockSpec(memory_space=pltpu.SMEM)
    if band:
        nk = tq + 2 * WINDOW
        rel = np.arange(nk)[None, :] - WINDOW - np.arange(tq)[:, None]
        inside = np.abs(rel) <= WINDOW
        col = np.arange(nk)[None, :]
        kinds = [inside & ((col >= WINDOW) | (not first)) & ((col < tq + WINDOW) | (not lastt))
                 for lastt in (False, True) for first in (False, True)]
        kinds = np.concatenate([np.stack(kinds), np.ones((4, tq, lc), bool)], axis=2)
        bias = jnp.asarray(np.where(kinds, 0.0, -np.inf), F32)
        bias_spec = pl.BlockSpec((1, tq, nk + lc),
                                 lambda b, i: ((i == 0).astype(I32) + 2 * (i == nt - 1).astype(I32), 0, 0))
        in_specs = [smem, qs, bias_spec, prev, cur, nxt, prev, cur, nxt, ctx, ctx]
        args = (sink, q, bias, k, k, k, v, v, v, kx, vx)
    else:
        in_specs = [smem, qs, ctx, ctx]
        args = (sink, q, kx, vx)
    return pl.pallas_call(
        functools.partial(_attn_kernel, band=band),
        grid=(nb, nt), in_specs=in_specs,
        out_specs=pl.BlockSpec((1, tq, ATT_HEADS * HEAD_DIM), lambda b, i: (b, i, 0)),
        out_shape=jax.ShapeDtypeStruct((nb, seq, ATT_HEADS * HEAD_DIM), BF16),
        compiler_params=_params("arbitrary", "arbitrary"), name="attn_band" if band else "attn_ctx",
    )(*args)


def _ssd_kernel(*refs, reverse, with_y, finalize, nc, act_in, emit_act):
    refs = list(refs)
    if act_in:
        act_ref, dt_ref, dtb_ref, alog_ref, init_ref = refs[:5]
        refs = refs[5:]
    else:
        xp_ref, xc_ref, xn_ref, dt_ref, cw_ref, cb_ref, dtb_ref, alog_ref, init_ref = refs[:9]
        refs = refs[9:]
    if finalize:
        yf_ref, z_ref, dsk_ref, nw_ref = refs[:4]
        refs = refs[4:]
    if with_y:
        y_ref = refs.pop(0)
    if emit_act:
        act_out_ref = refs.pop(0)
    (st_ref,) = refs
    q = SSM_CHUNK
    rows = dt_ref.shape[1]
    c = pl.program_id(1)
    ce = (nc - 1 - c) if reverse else c

    @pl.when(c == 0)
    def _():
        st_ref[...] = init_ref[...]

    if act_in:
        act_all = act_ref[0].astype(F32)
    else:
        xc = xc_ref[0].astype(F32)
        prow = jnp.where(ce == 0, 0.0, xp_ref[0].astype(F32)[BF16_ROWS - 1:BF16_ROWS])
        nrow = jnp.where(ce == nc - 1, 0.0, xn_ref[0].astype(F32)[0:1])
        rid = lax.broadcasted_iota(I32, xc.shape, 0)
        up = jnp.where(rid == 0, prow, pltpu.roll(xc, 1, 0))
        dn = jnp.where(rid == rows - 1, nrow, pltpu.roll(xc, rows - 1, 0))
        cw = cw_ref[...]
        act_all = _silu(cw[0:1] * up + cw[1:2] * xc + cw[2:3] * dn + cb_ref[...])
        if emit_act:
            act_out_ref[0] = act_all.astype(BF16)
    pre = dt_ref[0] + dtb_ref[...]
    dt_all = jnp.maximum(pre, 0.0) + jnp.log1p(jnp.exp(-jnp.abs(pre)))
    neg_a = -jnp.exp(alog_ref[...])

    subs = range(rows // q)
    for sub in (reversed(subs) if reverse else subs):
        r0 = sub * q
        _ssd_chunk(act_all[r0:r0 + q], dt_all[r0:r0 + q], neg_a, st_ref,
                   (yf_ref[0, r0:r0 + q], z_ref[0, r0:r0 + q], dsk_ref[...], nw_ref[...]) if finalize else None,
                   y_ref.at[0, r0:r0 + q] if with_y else None, reverse)


def _ssd_chunk(act, dt, neg_a, st_ref, fin, y_out, reverse):
    q = SSM_CHUNK
    with_y = y_out is not None
    xs = act[:, :SSM_INNER]
    bmat = act[:, SSM_INNER:SSM_INNER + SSM_BC_W]
    cmat = act[:, SSM_INNER + SSM_BC_W:]
    dta = dt * neg_a
    ri = lax.broadcasted_iota(I32, (q, q), 0)
    ci = lax.broadcasted_iota(I32, (q, q), 1)
    tri = (ri <= ci) if reverse else (ri >= ci)
    acs = _dot_split(tri.astype(BF16), dta)
    col0 = SSM_HEADS if reverse else 0
    lo_half = lax.broadcasted_iota(I32, (q, LANES), 1) < SSM_HEAD_DIM

    def bcast(mat, h):
        return jnp.broadcast_to(mat[:, col0 + h:col0 + h + 1], (q, LANES))

    def lane_expand(cols):
        return jnp.concatenate([jnp.where(lo_half, cols[2 * j], cols[2 * j + 1]) for j in range(SSM_HEADS // 2)],
                               axis=1)

    acs_b = [bcast(acs, h) for h in range(SSM_HEADS)]
    acs_e = lane_expand(acs_b)
    xd = xs * lane_expand([bcast(dt, h) for h in range(SSM_HEADS)])
    tot_e = acs_e[0:1] if reverse else acs_e[q - 1:q]
    hpg = SSM_HEADS // SSM_GROUPS
    gw = hpg * SSM_HEAD_DIM

    def grp(mat, g):
        return mat[:, g * SSM_STATE:(g + 1) * SSM_STATE]

    if with_y:
        acs_t = acs.T
        cb = [_dot_nt(grp(cmat, g), grp(bmat, g)) for g in range(SSM_GROUPS)]
        y_diag = []
        for j in range(SSM_HEADS // 2):
            sc = []
            for h in (2 * j, 2 * j + 1):
                seg = acs_b[h] - acs_t[col0 + h:col0 + h + 1, :]
                sc.append((cb[h // hpg] * jnp.exp(jnp.where(tri, seg, -jnp.inf))).astype(BF16))
            slab = xd[:, j * LANES:(j + 1) * LANES].astype(BF16)
            rhs = jnp.concatenate([jnp.where(lo_half, slab, 0), jnp.where(lo_half, 0, slab)], axis=0)
            y_diag.append(jnp.dot(jnp.concatenate(sc, axis=1), rhs, preferred_element_type=F32))
        y_off = [_dot(grp(cmat, g), st_ref[0, g]) for g in range(SSM_GROUPS)]
        y = jnp.concatenate(y_diag, axis=1) + jnp.concatenate(y_off, axis=1) * jnp.exp(acs_e)

    xdd = xd * jnp.exp(tot_e - acs_e)
    b_t = bmat.T
    for g in range(SSM_GROUPS):
        upd = _dot(b_t[g * SSM_STATE:(g + 1) * SSM_STATE, :], xdd[:, g * gw:(g + 1) * gw])
        st_ref[0, g] = st_ref[0, g] * jnp.exp(tot_e[:, g * gw:(g + 1) * gw]) + upd

    if fin is not None:
        y_fwd, z, dsk, nw = fin
        yt = y_fwd + y + dsk * xs
        yt = yt * _silu(z.astype(F32))
        yt = yt * lax.rsqrt(jnp.mean(yt * yt, axis=-1, keepdims=True) + EPS) * nw
        y_out[...] = yt.astype(BF16)
    elif with_y:
        y_out[...] = y


def _ssd(xbc, dtr, cw, cb, dtb, alog, init, reverse, with_y, fin=None, act=None, emit_act=False):
    nb, seq, _ = dtr.shape
    st_block = (1, SSM_GROUPS, SSM_STATE, SSM_INNER // SSM_GROUPS)
    blk = min(SSM_CHUNK * SSD_CHUNKS_PER_STEP, seq)
    nc = seq // blk
    per = blk // BF16_ROWS
    nhalo = seq // BF16_ROWS
    ce = (lambda c: nc - 1 - c) if reverse else (lambda c: c)
    xw = SSM_XBC_W
    blk_spec = lambda wd: pl.BlockSpec((1, blk, wd), lambda b, c: (b, ce(c), 0))
    tail_specs = [_const_spec((1, LANES)), _const_spec((1, LANES)), pl.BlockSpec(st_block, lambda b, c: (b, 0, 0, 0))]
    if act is not None:
        in_specs = [blk_spec(xw), blk_spec(LANES)] + tail_specs
        args = [act, dtr, dtb, alog, init]
    else:
        in_specs = [pl.BlockSpec((1, BF16_ROWS, xw), lambda b, c: (b, jnp.maximum(ce(c) * per - 1, 0), 0)),
                    blk_spec(xw),
                    pl.BlockSpec((1, BF16_ROWS, xw), lambda b, c: (b, jnp.minimum((ce(c) + 1) * per, nhalo - 1), 0)),
                    blk_spec(LANES), _const_spec((8, xw)), _const_spec((1, xw))] + tail_specs
        args = [xbc, xbc, xbc, dtr, cw, cb, dtb, alog, init]
    st_spec = pl.BlockSpec(st_block, lambda b, c: (b, 0, 0, 0))
    st_shape = jax.ShapeDtypeStruct((nb,) + st_block[1:], F32)
    y_spec = pl.BlockSpec((1, blk, SSM_INNER), lambda b, c: (b, ce(c), 0))
    finalize = fin is not None
    if finalize:
        yf, z, dsk, nw = fin
        in_specs += [y_spec, y_spec, _const_spec((1, SSM_INNER)), _const_spec((1, SSM_INNER))]
        args += [yf, z, dsk, nw]
    out_specs, out_shape = [], []
    if with_y:
        out_specs.append(y_spec)
        out_shape.append(jax.ShapeDtypeStruct((nb, seq, SSM_INNER), BF16 if finalize else F32))
    if emit_act:
        out_specs.append(blk_spec(xw))
        out_shape.append(jax.ShapeDtypeStruct((nb, seq, xw), BF16))
    out_specs.append(st_spec)
    out_shape.append(st_shape)
    res = pl.pallas_call(
        functools.partial(_ssd_kernel, reverse=reverse, with_y=with_y, finalize=finalize, nc=nc,
                          act_in=act is not None, emit_act=emit_act),
        grid=(nb, nc), in_specs=in_specs, out_specs=tuple(out_specs), out_shape=tuple(out_shape),
        compiler_params=_params("arbitrary", "arbitrary"), name="ssd_bwd" if reverse else "ssd_fwd",
    )(*args)
    return res


def _fft_a_kernel(u_ref, wc_ref, m1_ref, y_ref, wc_bf):
    @pl.when((pl.program_id(0) == 0) & (pl.program_id(1) == 0))
    def _():
        wc_bf[...] = wc_ref[...].astype(BF16)

    per = m1_ref.shape[0]
    u_all = jnp.concatenate([u_ref[0, :, j, :] for j in range(per)], axis=0)
    v_all = _dot(u_all, wc_bf[...])
    for j in range(per):
        v = v_all[j * FFT_N2:(j + 1) * FFT_N2]
        vs = jnp.concatenate([v[:, :FNET_WIDTH], v[:, FNET_WIDTH:]], axis=0)
        b = _dot(m1_ref[j], vs)
        y_ref[0, 0, j] = b[:FFT_N2]
        y_ref[0, 1, j] = b[FFT_N2:]


def _fft_c_kernel(y_ref, m2_ref, o_ref, m2_bf):
    @pl.when((pl.program_id(0) == 0) & (pl.program_id(1) == 0))
    def _():
        m2_bf[...] = m2_ref[...].astype(BF16)

    rows = y_ref.shape[1] * y_ref.shape[2] * y_ref.shape[4]
    y2d = y_ref[0].reshape(rows, FNET_WIDTH).astype(BF16)
    o_ref[0] = jnp.dot(m2_bf[...], y2d, preferred_element_type=F32).reshape(o_ref.shape[1:])


def _fft_small_kernel(u_ref, wc_ref, m_ref, o_ref):
    v = _dot(u_ref[0], wc_ref[...])
    vs = jnp.concatenate([v[:, :FNET_WIDTH], v[:, FNET_WIDTH:]], axis=0)
    o_ref[0] = _dot(m_ref[...], vs).astype(BF16)


def _channel_dft():
    idx = np.arange(FNET_GROUP_DIM)
    ang = 2.0 * np.pi * np.outer(idx, idx) / FNET_GROUP_DIM
    eye = np.eye(FNET_GROUPS)
    return jnp.asarray(np.concatenate([np.kron(eye, np.cos(ang)), -np.kron(eye, np.sin(ang))], axis=1), F32)


def _fourier_long(u):
    nb, seq, w = u.shape
    n2 = FFT_N2
    n1 = seq // n2
    t2 = np.arange(n2)
    k2 = np.arange(n2)
    t1 = np.arange(n1)
    ang = 2.0 * np.pi * (np.outer(k2, t2)[None] / n2 + (t1[:, None, None] * k2[None, :, None]) / seq)
    co, si = np.cos(ang), np.sin(ang)
    m1 = jnp.asarray(np.concatenate([np.concatenate([co, si], axis=2),
                                     np.concatenate([-si, co], axis=2)], axis=1), F32)
    ang1 = 2.0 * np.pi * np.outer(t1, t1) / n1
    norm = 1.0 / math.sqrt(seq * FNET_GROUP_DIM)
    m2 = np.concatenate([np.cos(ang1), np.sin(ang1)], axis=1) * norm
    m2k = jnp.asarray(np.kron(m2, np.eye(F32_ROWS)), F32)
    per = F32_ROWS
    y = pl.pallas_call(
        _fft_a_kernel, grid=(n1 // per, nb),
        in_specs=[pl.BlockSpec((1, n2, per, w), lambda j, b: (b, 0, j, 0)),
                  _const_spec((w, 2 * w)),
                  pl.BlockSpec((per, 2 * n2, 2 * n2), lambda j, b: (j, 0, 0))],
        out_specs=pl.BlockSpec((1, 2, per, n2, w), lambda j, b: (b, 0, j, 0, 0)),
        out_shape=jax.ShapeDtypeStruct((nb, 2, n1, n2, w), F32),
        scratch_shapes=[pltpu.VMEM((w, 2 * w), BF16)],
        compiler_params=_params("arbitrary", "arbitrary"), name="fft_a",
    )(u.reshape(nb, n2, n1, w), _channel_dft(), m1)
    out = pl.pallas_call(
        _fft_c_kernel, grid=(nb, n2 // per),
        in_specs=[pl.BlockSpec((1, 2, n1, 1, per, w), lambda b, j: (b, 0, 0, j, 0, 0)),
                  _const_spec((n1 * per, 2 * n1 * per))],
        out_specs=pl.BlockSpec((1, n1, 1, per, w), lambda b, j: (b, 0, j, 0, 0)),
        out_shape=jax.ShapeDtypeStruct((nb, n1, n2 // per, per, w), F32),
        scratch_shapes=[pltpu.VMEM((n1 * per, 2 * n1 * per), BF16)],
        compiler_params=_params("arbitrary", "arbitrary"), name="fft_c",
    )(y.reshape(nb, 2, n1, n2 // per, per, w), m2k)
    return out.reshape(nb, seq, w)


def _fourier_short(u):
    nb, seq, w = u.shape
    t = np.arange(seq)
    ang = 2.0 * np.pi * np.outer(t, t) / seq
    norm = 1.0 / math.sqrt(seq * FNET_GROUP_DIM)
    m = jnp.asarray(np.concatenate([np.cos(ang), np.sin(ang)], axis=1) * norm, F32)
    return pl.pallas_call(
        _fft_small_kernel, grid=(nb,),
        in_specs=[pl.BlockSpec((1, seq, w), lambda b: (b, 0, 0)),
                  _const_spec((w, 2 * w)), _const_spec((seq, 2 * seq))],
        out_specs=pl.BlockSpec((1, seq, w), lambda b: (b, 0, 0)),
        out_shape=jax.ShapeDtypeStruct((nb, seq, w), BF16),
        compiler_params=_params("arbitrary"), name="fft_small",
    )(u, _channel_dft(), m)


def _merge_kernel(x_ref, mod_ref, g1_ref, g2_ref, att_ref, ssm_ref, fft_ref, sup_ref, su_ref, sun_ref,
                  sw_ref, wg_ref, bg_ref, wb_ref, wo_ref, wr_ref, rb_ref, cin_ref,
                  x1_ref, h2_ref, ri_ref, rw_ref, cnt_ref, *, nt):
    d = x_ref.shape[1]
    t = x_ref.shape[0]
    i = pl.program_id(1)

    @pl.when((pl.program_id(0) == 0) & (i == 0))
    def _():
        cnt_ref[...] = cin_ref[...]

    m = mod_ref[0]

    def gated(su):
        return su[:, SCONV_WIDTH:2 * SCONV_WIDTH] * su[:, 2 * SCONV_WIDTH:]

    su = su_ref[0].astype(F32)
    p = gated(su)
    prow = jnp.where(i == 0, 0.0, gated(sup_ref[0].astype(F32)[BF16_ROWS - 1:BF16_ROWS]))
    nrow = jnp.where(i == nt - 1, 0.0, gated(sun_ref[0].astype(F32)[0:1]))
    rid = lax.broadcasted_iota(I32, p.shape, 0)
    up = jnp.where(rid == 0, prow, pltpu.roll(p, 1, 0))
    dn = jnp.where(rid == t - 1, nrow, pltpu.roll(p, t - 1, 0))
    sw = sw_ref[...]
    sconv = su[:, :SCONV_WIDTH] * (sw[0:1] * up + sw[1:2] * p + sw[2:3] * dn)

    rg = min(t, MERGE_ROW_GROUP)
    h2_parts = []
    for a in range(0, t, rg):
        x = x_ref[a:a + rg]
        hb = _norm_mod(x, g1_ref[...], m[0:1], m[1:2]).astype(BF16)
        branches = (att_ref[0, a:a + rg], ssm_ref[0, a:a + rg], fft_ref[0, a:a + rg].astype(BF16),
                    sconv[a:a + rg].astype(BF16))
        acc = jnp.zeros((rg, d), F32)
        for n in range(N_BRANCHES):
            gate = _sigmoid(jnp.dot(hb, wg_ref[:, n * d:(n + 1) * d], preferred_element_type=F32)
                            + bg_ref[:, n * d:(n + 1) * d])
            acc = acc + gate * jnp.dot(branches[n], wb_ref[n], preferred_element_type=F32)
        y = jnp.dot(acc.astype(BF16), wo_ref[...], preferred_element_type=F32)
        x1 = x + m[2:3] * y
        x1_ref[a:a + rg] = x1
        h2 = _norm_mod(x1, g2_ref[...], m[3:4], m[4:5])
        h2_ref[a:a + rg] = h2
        h2_parts.append(h2)
    _route_tile(jnp.concatenate(h2_parts, axis=0), wr_ref, rb_ref, ri_ref, rw_ref, cnt_ref)


def _merge(xflat, nb, seq, t, mod, mod_row, g1, g2, att, ssm, fft, su, sw, wg, bg, wb, wo, wr, rb, cnt_in):
    ntok, d = xflat.shape
    nt = seq // t
    per = t // BF16_ROWS
    nhalo = seq // BF16_ROWS
    mod_map = (lambda b, i: (b, 0, 0)) if mod_row is None else (lambda b, i: (mod_row, 0, 0))
    row = pl.BlockSpec((t, d), lambda b, i: (b * nt + i, 0))
    br = pl.BlockSpec((1, t, BRANCH_W), lambda b, i: (b, i, 0))
    suw = 3 * SCONV_WIDTH
    in_specs = [row, pl.BlockSpec((1, 8, d), mod_map), _const_spec((1, d)), _const_spec((1, d)),
                br, br, br,
                pl.BlockSpec((1, BF16_ROWS, suw), lambda b, i: (b, jnp.maximum(i * per - 1, 0), 0)),
                pl.BlockSpec((1, t, suw), lambda b, i: (b, i, 0)),
                pl.BlockSpec((1, BF16_ROWS, suw), lambda b, i: (b, jnp.minimum((i + 1) * per, nhalo - 1), 0)),
                _const_spec((8, SCONV_WIDTH)), _const_spec((d, N_BRANCHES * d)), _const_spec((1, N_BRANCHES * d)),
                _const_spec((N_BRANCHES, BRANCH_W, d)), _const_spec((d, d)),
                _const_spec((N_EXPERTS, d)), _const_spec((N_EXPERTS, t)), _const_spec((N_EXPERTS, LANES))]
    tile_rows = pl.BlockSpec((1, 8, t), lambda b, i: (b * nt + i, 0, 0))
    x1, h2, ri, rw, cnt = pl.pallas_call(
        functools.partial(_merge_kernel, nt=nt),
        grid=(nb, nt), in_specs=in_specs,
        out_specs=(row, row, tile_rows, tile_rows, pl.BlockSpec((N_EXPERTS, LANES), lambda b, i: (0, 0))),
        out_shape=(jax.ShapeDtypeStruct((ntok, d), F32), jax.ShapeDtypeStruct((ntok, d), F32),
                   jax.ShapeDtypeStruct((ntok // t, 8, t), I32), jax.ShapeDtypeStruct((ntok // t, 8, t), F32),
                   jax.ShapeDtypeStruct((N_EXPERTS, LANES), F32)),
        compiler_params=_params("arbitrary", "arbitrary"), name="merge",
    )(xflat, mod, g1, g2, att, ssm, fft, su, su, su, sw, wg, bg, wb, wo, wr, jnp.tile(rb, (1, t // LANES)), cnt_in)
    return x1, h2, ri.transpose(0, 2, 1).reshape(ntok, 8), rw.transpose(0, 2, 1).reshape(ntok, 8), cnt


def _route_tile(h2, wr_ref, rb_ref, ri_ref, rw_ref, cnt_ref):
    t = h2.shape[0]
    ng = N_EXPERT_GROUPS
    sc = _sigmoid(_dot_nt(wr_ref[...], h2))
    sel = sc + rb_ref[...]
    s = [sel[j * ng:(j + 1) * ng] for j in range(EXPERTS_PER_GROUP)]
    u = [sc[j * ng:(j + 1) * ng] for j in range(EXPERTS_PER_GROUP)]
    gs = None
    for a in range(EXPERTS_PER_GROUP):
        for b in range(a + 1, EXPERTS_PER_GROUP):
            pair = s[a] + s[b]
            gs = pair if gs is None else jnp.maximum(gs, pair)
    grp = lax.broadcasted_iota(I32, (ng, t), 0)
    gmax = jnp.max(gs, axis=0, keepdims=True)
    best = jnp.min(jnp.where(gs == gmax, grp, ng), axis=0, keepdims=True)
    on = grp == best
    v = [jnp.sum(jnp.where(on, s[j], 0.0), axis=0, keepdims=True) for j in range(EXPERTS_PER_GROUP)]
    w = [jnp.sum(jnp.where(on, u[j], 0.0), axis=0, keepdims=True) for j in range(EXPERTS_PER_GROUP)]

    def first_argmax(vals):
        mx = vals[0]
        for x in vals[1:]:
            mx = jnp.maximum(mx, x)
        idx = jnp.full_like(best, len(vals) - 1)
        for j in range(len(vals) - 2, -1, -1):
            idx = jnp.where(vals[j] == mx, j, idx)
        return idx

    def pick(vals, idx):
        out = vals[-1]
        for j in range(len(vals) - 2, -1, -1):
            out = jnp.where(idx == j, vals[j], out)
        return out

    l1 = first_argmax(v)
    l2 = first_argmax([jnp.where(l1 == j, -jnp.inf, v[j]) for j in range(EXPERTS_PER_GROUP)])
    w1, w2 = pick(w, l1), pick(w, l2)
    tot = w1 + w2
    e1 = best * EXPERTS_PER_GROUP + l1
    e2 = best * EXPERTS_PER_GROUP + l2

    eid = lax.broadcasted_iota(I32, (N_EXPERTS, t), 0)
    oh1 = (eid == e1).astype(F32)
    oh2 = (eid == e2).astype(F32)
    oh = oh1 + oh2
    ri_ = lax.broadcasted_iota(I32, (t, t), 0)
    ci_ = lax.broadcasted_iota(I32, (t, t), 1)
    cnt = cnt_ref[...]
    before = _dot(oh, (ri_ < ci_).astype(F32)) + jnp.tile(cnt, (1, t // LANES))
    r1 = jnp.sum(oh1 * before, axis=0, keepdims=True).astype(I32)
    r2 = jnp.sum(oh2 * before, axis=0, keepdims=True).astype(I32)
    cnt_ref[...] = cnt + jnp.sum(oh, axis=1, keepdims=True)

    row = lax.broadcasted_iota(I32, (8, t), 0)
    ri_ref[0] = jnp.where(row == 0, e1, jnp.where(row == 1, e2, jnp.where(row == 2, r1, jnp.where(row == 3, r2, 0))))
    rw_ref[0] = jnp.where(row == 0, w1 / tot, jnp.where(row == 1, w2 / tot, 0.0))


def _dispatch_kernel(dest_ref, zt_ref, *refs, td, starts):
    n_streams = len(starts) - 1
    h_refs = refs[:n_streams]
    xs_hbm, zbuf, zsem, sem = refs[n_streams:]
    i = pl.program_id(0)

    @pl.when(i == 0)
    def _():
        zbuf[...] = jnp.zeros_like(zbuf)

        def zero_copy(j):
            start = pl.multiple_of(zt_ref[j] * MOE_TILE, MOE_TILE)
            return pltpu.make_async_copy(zbuf, xs_hbm.at[pl.ds(start, MOE_TILE)], zsem)

        def zissue(j, carry):
            @pl.when(zt_ref[j] >= 0)
            def _():
                zero_copy(j).start()
            return carry

        def zdrain(j, carry):
            @pl.when(zt_ref[j] >= 0)
            def _():
                zero_copy(j).wait()
            return carry

        lax.fori_loop(0, zt_ref.shape[0], zissue, 0)
        lax.fori_loop(0, zt_ref.shape[0], zdrain, 0)

    for s in range(n_streams):
        h_ref = h_refs[s]

        def row_copy(src_row, dst_row, h_ref=h_ref):
            return pltpu.make_async_copy(h_ref.at[pl.ds(src_row, 1)], xs_hbm.at[pl.ds(dst_row, 1)], sem)

        @pl.when((i >= starts[s]) & (i < starts[s + 1]))
        def _(row_copy=row_copy):
            def issue(j, carry):
                row_copy(j, dest_ref[0, 0, 2 * j]).start()
                row_copy(j, dest_ref[0, 0, 2 * j + 1]).start()
                return carry

            lax.fori_loop(0, td, issue, 0, unroll=ROW_DMA_UNROLL)
            all_rows = xs_hbm.at[pl.ds(0, 2 * td)]
            pltpu.make_async_copy(all_rows, all_rows, sem).wait()


def _dispatch(h2s, dest, ztiles, nrows):
    d = h2s[0].shape[1]
    td = ROW_TILE
    starts = [0]
    for h in h2s:
        starts.append(starts[-1] + h.shape[0] // td)
    nt = starts[-1]

    def stream_spec(s):
        return pl.BlockSpec((td, d), lambda i: (jnp.clip(i - starts[s], 0, starts[s + 1] - starts[s] - 1), 0))

    return pl.pallas_call(
        functools.partial(_dispatch_kernel, td=td, starts=tuple(starts)), grid=(nt,),
        in_specs=[pl.BlockSpec((1, 1, 2 * td), lambda i: (i, 0, 0), memory_space=pltpu.SMEM),
                  pl.BlockSpec(memory_space=pltpu.SMEM)] + [stream_spec(s) for s in range(len(h2s))],
        out_specs=pl.BlockSpec(memory_space=pl.ANY),
        out_shape=jax.ShapeDtypeStruct((nrows, d), F32),
        scratch_shapes=[pltpu.VMEM((MOE_TILE, d), F32), pltpu.SemaphoreType.DMA(()), pltpu.SemaphoreType.DMA(())],
        compiler_params=_params("arbitrary"), name="dispatch",
    )(dest.reshape(nt, 1, 2 * td), ztiles, *h2s)


def _ffn_kernel(be_ref, nu_ref, x_ref, w1_ref, w3_ref, w2_ref, y_ref, w1_bf, w3_bf, w2_bf):
    i = pl.program_id(0)
    used = i < nu_ref[0]

    @pl.when(used & ((i == 0) | (be_ref[i] != be_ref[jnp.maximum(i - 1, 0)])))
    def _():
        w1_bf[...] = w1_ref[0, 0].astype(BF16)
        w3_bf[...] = w3_ref[0, 0].astype(BF16)
        w2_bf[...] = w2_ref[0, 0].astype(BF16)

    @pl.when(used)
    def _():
        xb = x_ref[...].astype(BF16)
        a = jnp.dot(xb, w1_bf[...], preferred_element_type=F32)
        b = jnp.dot(xb, w3_bf[...], preferred_element_type=F32)
        y_ref[...] = jnp.dot((_silu(a) * b).astype(BF16), w2_bf[...], preferred_element_type=F32)

    @pl.when(i >= nu_ref[0])
    def _():
        y_ref[...] = jnp.zeros_like(y_ref)


def _ffn(xs, block_expert, n_used, w1, w3, w2, layer):
    nrows, d = xs.shape
    de = w1.shape[3]
    tm = MOE_TILE
    grid_spec = pltpu.PrefetchScalarGridSpec(
        num_scalar_prefetch=2, grid=(nrows // tm,),
        in_specs=[pl.BlockSpec((tm, d), lambda i, be, nu: (jnp.minimum(i, nu[0] - 1), 0)),
                  pl.BlockSpec((1, 1, d, de), lambda i, be, nu: (layer, be[i], 0, 0)),
                  pl.BlockSpec((1, 1, d, de), lambda i, be, nu: (layer, be[i], 0, 0)),
                  pl.BlockSpec((1, 1, de, d), lambda i, be, nu: (layer, be[i], 0, 0))],
        out_specs=pl.BlockSpec((tm, d), lambda i, be, nu: (i, 0)),
        scratch_shapes=[pltpu.VMEM((d, de), BF16), pltpu.VMEM((d, de), BF16), pltpu.VMEM((de, d), BF16)])
    return pl.pallas_call(
        _ffn_kernel, grid_spec=grid_spec, out_shape=jax.ShapeDtypeStruct((nrows, d), F32),
        compiler_params=_params("arbitrary"), name="ffn",
    )(block_expert, n_used, xs, w1, w3, w2)


def _combine_kernel(dest_ref, dnext_ref, x_ref, mod_ref, rw_ref, fw_ref, y_hbm, o_ref, buf, sems, *, tc, final):
    i = pl.program_id(0)
    nt = pl.num_programs(0)
    slot = i % 2

    def gather(d_ref, s):
        def issue(j, carry):
            for k in range(2):
                pltpu.make_async_copy(y_hbm.at[pl.ds(d_ref[0, 0, 2 * j + k], 1)],
                                      buf.at[s, k, pl.ds(j, 1)], sems.at[s]).start()
            return carry
        lax.fori_loop(0, tc, issue, 0, unroll=ROW_DMA_UNROLL)

    @pl.when(i == 0)
    def _():
        gather(dest_ref, 0)

    @pl.when(i + 1 < nt)
    def _():
        gather(dnext_ref, 1 - slot)

    pltpu.make_async_copy(buf.at[slot], buf.at[slot], sems.at[slot]).wait()
    rw = rw_ref[...]
    f = rw[:, 0:1] * buf[slot, 0] + rw[:, 1:2] * buf[slot, 1]
    x = x_ref[...] + mod_ref[0][5:6] * f
    if final:
        x = x * lax.rsqrt(jnp.mean(x * x, axis=-1, keepdims=True) + EPS) * fw_ref[...]
    o_ref[...] = x


def _combine(x1, dest, rw, y, mod, mod_row, seq, fw, final):
    ntok, d = x1.shape
    tc = ROW_TILE
    nt = ntok // tc
    per_seq = seq // tc
    mod_map = (lambda i: (i // per_seq, 0, 0)) if mod_row is None else (lambda i: (mod_row, 0, 0))
    return pl.pallas_call(
        functools.partial(_combine_kernel, tc=tc, final=final), grid=(nt,),
        in_specs=[pl.BlockSpec((1, 1, 2 * tc), lambda i: (i, 0, 0), memory_space=pltpu.SMEM),
                  pl.BlockSpec((1, 1, 2 * tc), lambda i: (jnp.minimum(i + 1, nt - 1), 0, 0), memory_space=pltpu.SMEM),
                  pl.BlockSpec((tc, d), lambda i: (i, 0)),
                  pl.BlockSpec((1, 8, d), mod_map),
                  pl.BlockSpec((tc, 8), lambda i: (i, 0)),
                  _const_spec((1, d)),
                  pl.BlockSpec(memory_space=pl.ANY)],
        out_specs=pl.BlockSpec((tc, d), lambda i: (i, 0)),
        out_shape=jax.ShapeDtypeStruct((ntok, d), F32),
        scratch_shapes=[pltpu.VMEM((2, 2, tc, d), F32), pltpu.SemaphoreType.DMA((2,))],
        compiler_params=_params("arbitrary"), name="combine",
    )(dest.reshape(nt, 1, 2 * tc), dest.reshape(nt, 1, 2 * tc), x1, mod, rw, fw, y)


def _moe(streams, cnt, mod, w1, w3, w2, layer, fw, final):
    counts = cnt[:, 0].astype(I32)
    padded = (counts + MOE_TILE - 1) // MOE_TILE * MOE_TILE
    pad_end = jnp.cumsum(padded)
    offs = pad_end - padded
    ntok = sum(s[0].shape[0] for s in streams)
    n_tiles = -(-2 * ntok // MOE_TILE) + N_EXPERTS
    tile_start = jnp.arange(n_tiles, dtype=I32) * MOE_TILE
    block_expert = jnp.minimum(jnp.sum((pad_end[None, :] <= tile_start[:, None]).astype(I32), axis=1), N_EXPERTS - 1)
    n_used = (pad_end[-1:] // MOE_TILE).astype(I32)
    dests = [(offs[s[2][:, 0:2]] + s[2][:, 2:4]).reshape(-1) for s in streams]
    tail = n_used + jnp.arange(N_EXPERTS, dtype=I32)
    ztiles = jnp.concatenate([jnp.where(padded > 0, pad_end // MOE_TILE - 1, -1),
                              jnp.where(tail < n_tiles, tail, -1)]).astype(I32)
    xs = _dispatch([s[1] for s in streams], jnp.concatenate(dests), ztiles, n_tiles * MOE_TILE)
    y = _ffn(xs, block_expert, n_used, w1, w3, w2, layer)
    return [_combine(x1, dest, rw, y, mod, mod_row, seq, fw, final)
            for (x1, _, _, rw, seq, mod_row), dest in zip(streams, dests)]


def _pad_rows(a, rows):
    return jnp.concatenate([a, jnp.zeros((rows - a.shape[0],) + a.shape[1:], a.dtype)], axis=0)


def _pad_cols(a, cols, value=0.0):
    return jnp.concatenate([a, jnp.full(a.shape[:-1] + (cols - a.shape[-1],), value, a.dtype)], axis=-1)


def _rope_tables(seq):
    rows = seq // GRID_W
    row = jnp.repeat(jnp.arange(rows, dtype=F32), GRID_W)
    col = jnp.tile(jnp.arange(GRID_W, dtype=F32), rows)
    pairs = HEAD_DIM // 4
    inv_freq = ROPE_THETA ** (-jnp.arange(pairs, dtype=F32) / pairs)
    ang = jnp.concatenate([row[:, None] * inv_freq, col[:, None] * inv_freq], axis=-1)
    cos, sin = jnp.cos(ang), jnp.sin(ang)
    cos_t = jnp.tile(cos, (1, LANES // (HEAD_DIM // 2)))
    sin_t = jnp.tile(jnp.concatenate([-sin, sin], axis=-1), (1, LANES // HEAD_DIM))
    return cos_t, sin_t


def kernel(x, c, ctx, c_ctx, norm1_w, norm2_w, w_ada, b_ada, w_in, attn_sink, ssm_conv_w, ssm_conv_b,
           ssm_dt_bias, ssm_a_log, ssm_d, ssm_norm_w, sconv_w, w_gate, b_gate, w_branch, w_o,
           w_router, router_bias, moe_w1, moe_w3, moe_w2, final_norm_w):
    nb, seq, d = x.shape
    lc = ctx.shape[1]
    depth = w_in.shape[0]
    n_lat = nb * seq
    n_ctx = nb * lc
    assert seq % LATENT_TILE == 0 and seq % (FFT_N2 * F32_ROWS) == 0 and seq % GRID_W == 0
    assert lc % CTX_TILE == 0 and n_ctx % ROW_TILE == 0 and seq % ROW_TILE == 0
    assert nb < 8

    xl = x.reshape(n_lat, d)
    xc = ctx.reshape(n_ctx, d)
    c8 = _pad_rows(jnp.concatenate([c, c_ctx[None]], axis=0), 8)
    mods = _ada(c8, w_ada, b_ada)
    mods = _pad_cols(mods.reshape(depth, 8, 6, d).swapaxes(-1, -2), 8).swapaxes(-1, -2)

    cos_l, sin_l = _rope_tables(seq)
    cos_c, sin_c = jnp.ones((lc, LANES), F32), jnp.zeros((lc, LANES), F32)

    wr = w_router.T.reshape(N_EXPERT_GROUPS, EXPERTS_PER_GROUP, d).transpose(1, 0, 2).reshape(N_EXPERTS, d)
    wr = wr.astype(BF16)
    rb = jnp.broadcast_to(router_bias.reshape(N_EXPERT_GROUPS, EXPERTS_PER_GROUP).T.reshape(N_EXPERTS, 1),
                          (N_EXPERTS, LANES))

    for layer in range(depth):
        last = layer + 1 == depth
        mod = mods[layer]
        wi = w_in[layer]
        w_pad = jnp.concatenate([wi[:, :768], _pad_cols(wi[:, 2048:2048 + DT_W], 2 * LANES), wi[:, 768:2048],
                                 wi[:, 2048 + DT_W:]], axis=1).astype(BF16)
        g1 = norm1_w[layer][None]
        g2 = norm2_w[layer][None]
        cw = _pad_rows(ssm_conv_w[layer], 8)
        cb = ssm_conv_b[layer][None]
        dtb = _pad_cols(ssm_dt_bias[layer].reshape(1, DT_W), LANES)
        alog = _pad_cols(ssm_a_log[layer].reshape(1, DT_W), LANES)
        dsk = jnp.repeat(ssm_d[layer], SSM_HEAD_DIM)[None]
        nw = ssm_norm_w[layer][None]
        sw = _pad_rows(sconv_w[layer], 8)
        wg = w_gate[layer].astype(BF16)
        bg = b_gate[layer][None]
        wb = w_branch[layer].astype(BF16)
        wo = w_o[layer].astype(BF16)
        sink = attn_sink[layer]

        q, k, v, z, xbc, dtr, fu, su = _inproj(xl, nb, seq, LATENT_TILE, mod, None, g1, w_pad, cos_l, sin_l)
        qc, kc, vc, zc, xbcc, dtrc, fuc, suc = _inproj(xc, nb, lc, CTX_TILE, mod, nb, g1, w_pad, cos_c, sin_c)

        att = _attention(sink, q, k, v, kc, vc, ATT_TILE, True)

        zero_state = jnp.zeros((nb, SSM_GROUPS, SSM_STATE, SSM_INNER // SSM_GROUPS), F32)
        ssd_c = functools.partial(_ssd, xbcc, dtrc, cw, cb, dtb, alog)
        ssd_l = functools.partial(_ssd, xbc, dtr, cw, cb, dtb, alog)
        if last:
            act_c, fin_f = ssd_c(zero_state, False, False, emit_act=True)
            (fin_b,) = ssd_c(zero_state, True, False, act=act_c)
        else:
            yc_f, act_c, fin_f = ssd_c(zero_state, False, True, emit_act=True)
            ssm_c, fin_b = ssd_c(zero_state, True, True, fin=(yc_f, zc, dsk, nw), act=act_c)
        y_f, act_l, _ = ssd_l(fin_f, False, True, emit_act=True)
        ssm, _ = ssd_l(fin_b, True, True, fin=(y_f, z, dsk, nw), act=act_l)

        fft = _fourier_long(fu)

        x1, h2, ri, rw, cnt = _merge(xl, nb, seq, LATENT_TILE, mod, None, g1, g2, att, ssm, fft, su, sw, wg, bg, wb, wo,
                                     wr, rb, jnp.zeros((N_EXPERTS, LANES), F32))
        streams = [(x1, h2, ri, rw, seq, None)]
        if not last:
            att_c = _attention(sink, qc, None, None, kc, vc, lc, False)
            fft_c = _fourier_short(fuc)
            x1c, h2c, ric, rwc, cnt = _merge(xc, nb, lc, CTX_TILE, mod, nb, g1, g2, att_c, ssm_c, fft_c, suc, sw, wg,
                                             bg, wb, wo, wr, rb, cnt)
            streams.append((x1c, h2c, ric, rwc, lc, nb))
        new = _moe(streams, cnt, mod, moe_w1, moe_w3, moe_w2, layer, final_norm_w[None], last)
        xl = new[0]
        if not last:
            xc = new[1]
    return xl.reshape(nb, seq, d)
```

```python
import functools
import math

import numpy as np
import jax
import jax.numpy as jnp
from jax import lax
from jax.experimental import pallas as pl
from jax.experimental.pallas import tpu as pltpu

F32 = jnp.float32
BF16 = jnp.bfloat16
I32 = jnp.int32

EPS = 1e-6
GRID_W = 64
ROPE_THETA = 10000.0
HEAD_DIM = 64
ATT_HEADS = 8
ATT_KV_HEADS = 2
ATT_GROUP = ATT_HEADS // ATT_KV_HEADS
WINDOW = 128
SSM_HEADS = 8
SSM_HEAD_DIM = 64
SSM_INNER = SSM_HEADS * SSM_HEAD_DIM
SSM_GROUPS = 2
SSM_STATE = 64
SSM_BC_W = SSM_GROUPS * SSM_STATE
SSM_XBC_W = SSM_INNER + 2 * SSM_BC_W
SSM_CHUNK = 128
FNET_GROUPS = 8
FNET_GROUP_DIM = 64
FNET_WIDTH = FNET_GROUPS * FNET_GROUP_DIM
SCONV_WIDTH = 512
BRANCH_W = 512
N_BRANCHES = 4
N_EXPERTS = 32
N_EXPERT_GROUPS = 8
EXPERTS_PER_GROUP = N_EXPERTS // N_EXPERT_GROUPS
D_EXPERT = 512

LANES = 128
BF16_ROWS = 16
VMEM_LIMIT = 56 * 1024 * 1024
FFT_N2 = 128
F32_ROWS = 8
SSD_CHUNKS_PER_STEP = 8
MERGE_ROW_GROUP = 256
LATENT_TILE = 512
CTX_TILE = 256
ATT_TILE = 256
ADA_TILE = 512
ROW_TILE = 1024
MOE_TILE = 256
ROW_DMA_UNROLL = 8

Q_OFF, K_OFF, Z_OFF, XBC_OFF, FU_OFF, SU_OFF = 0, 512, 1024, 1536, 2304, 2816
IN_W_PAD = SU_OFF + 3 * SCONV_WIDTH
DT_W = 2 * SSM_HEADS


def _dot(a, b):
    return jnp.dot(a.astype(BF16), b.astype(BF16), preferred_element_type=F32)


def _dot_nt(a, b):
    return lax.dot_general(a.astype(BF16), b.astype(BF16), (((1,), (1,)), ((), ())),
                           preferred_element_type=F32)


def _dot_split(a_bf16, x):
    hi = x.astype(BF16)
    rest = x - hi.astype(F32)
    mid = rest.astype(BF16)
    lo = (rest - mid.astype(F32)).astype(BF16)
    return sum(jnp.dot(a_bf16, t, preferred_element_type=F32) for t in (hi, mid, lo))


def _sigmoid(x):
    return 1.0 / (1.0 + jnp.exp(-x))


def _silu(x):
    return x * _sigmoid(x)


def _norm_mod(x, g, shift, scale):
    y = x * lax.rsqrt(jnp.mean(x * x, axis=-1, keepdims=True) + EPS) * g
    return y * (1.0 + scale) + shift


def _params(*sem):
    return pltpu.CompilerParams(dimension_semantics=sem, vmem_limit_bytes=VMEM_LIMIT)


def _const_spec(shape):
    n = len(shape)
    return pl.BlockSpec(shape, lambda *_: (0,) * n, pipeline_mode=pl.Buffered(1))


def _ada_kernel(c_ref, w_ref, b_ref, o_ref):
    o_ref[0] = _dot(_silu(c_ref[...]), w_ref[0]) + b_ref[0]


def _ada(c8, w_ada, b_ada):
    nl, d, w = w_ada.shape
    tn = ADA_TILE
    return pl.pallas_call(
        _ada_kernel, grid=(nl, w // tn),
        in_specs=[pl.BlockSpec((8, d), lambda l, j: (0, 0)),
                  pl.BlockSpec((1, d, tn), lambda l, j: (l, 0, j)),
                  pl.BlockSpec((1, 1, tn), lambda l, j: (l, 0, j))],
        out_specs=pl.BlockSpec((1, 8, tn), lambda l, j: (l, 0, j)),
        out_shape=jax.ShapeDtypeStruct((nl, 8, w), F32),
        compiler_params=_params("arbitrary", "arbitrary"), name="ada",
    )(c8, w_ada, b_ada.reshape(nl, 1, w))


def _inproj_kernel(x_ref, mod_ref, g_ref, w_ref, cos_ref, sin_ref,
                   q_ref, k_ref, v_ref, z_ref, xbc_ref, dt_ref, fu_ref, su_ref):
    m = mod_ref[0]
    hb = _norm_mod(x_ref[...], g_ref[...], m[0:1], m[1:2]).astype(BF16)
    cos = cos_ref[...]
    sin = sin_ref[...]
    lane = lax.broadcasted_iota(I32, cos.shape, 1)
    first_half = (lane % HEAD_DIM) < HEAD_DIM // 2

    def proj(start, width):
        return jnp.dot(hb, w_ref[:, start:start + width], preferred_element_type=F32)

    def rope(r):
        rot = jnp.where(first_half, pltpu.roll(r, LANES - HEAD_DIM // 2, 1), pltpu.roll(r, HEAD_DIM // 2, 1))
        return r * cos + rot * sin

    scale = HEAD_DIM ** -0.5
    qq = proj(Q_OFF, ATT_HEADS * HEAD_DIM)
    for j in range(ATT_HEADS // 2):
        r = rope(qq[:, j * LANES:(j + 1) * LANES]) * scale
        q_ref[0, 2 * j] = r[:, :HEAD_DIM].astype(BF16)
        q_ref[0, 2 * j + 1] = r[:, HEAD_DIM:].astype(BF16)
    kvd = proj(K_OFF, 4 * LANES)
    r = rope(kvd[:, :LANES])
    k_ref[0, 0] = r[:, :HEAD_DIM].astype(BF16)
    k_ref[0, 1] = r[:, HEAD_DIM:].astype(BF16)
    r = kvd[:, LANES:2 * LANES]
    v_ref[0, 0] = r[:, :HEAD_DIM].astype(BF16)
    v_ref[0, 1] = r[:, HEAD_DIM:].astype(BF16)
    dt_ref[0] = kvd[:, 2 * LANES:3 * LANES]
    z_ref[0] = proj(Z_OFF, SSM_INNER).astype(BF16)
    xbc_ref[0] = proj(XBC_OFF, SSM_XBC_W).astype(BF16)
    fu_ref[0] = proj(FU_OFF, FNET_WIDTH)
    su_ref[0] = proj(SU_OFF, 3 * SCONV_WIDTH).astype(BF16)


def _inproj(xflat, nb, seq, t, mod, mod_row, g, w, cos, sin):
    d = xflat.shape[1]
    nt = seq // t
    if mod_row is None:
        mod_map = lambda b, i: (b, 0, 0)
    else:
        mod_map = lambda b, i: (mod_row, 0, 0)
    sds = jax.ShapeDtypeStruct
    outs = (sds((nb, ATT_HEADS, seq, HEAD_DIM), BF16), sds((nb, ATT_KV_HEADS, seq, HEAD_DIM), BF16),
            sds((nb, ATT_KV_HEADS, seq, HEAD_DIM), BF16), sds((nb, seq, SSM_INNER), BF16),
            sds((nb, seq, SSM_XBC_W), BF16), sds((nb, seq, LANES), F32),
            sds((nb, seq, FNET_WIDTH), F32), sds((nb, seq, 3 * SCONV_WIDTH), BF16))
    head_spec = lambda nh: pl.BlockSpec((1, nh, t, HEAD_DIM), lambda b, i: (b, 0, i, 0))
    row_spec = lambda wd: pl.BlockSpec((1, t, wd), lambda b, i: (b, i, 0))
    return pl.pallas_call(
        _inproj_kernel, grid=(nb, nt),
        in_specs=[pl.BlockSpec((t, d), lambda b, i: (b * nt + i, 0)),
                  pl.BlockSpec((1, 8, d), mod_map),
                  _const_spec((1, d)), _const_spec((d, IN_W_PAD)),
                  pl.BlockSpec((t, LANES), lambda b, i: (i, 0)),
                  pl.BlockSpec((t, LANES), lambda b, i: (i, 0))],
        out_specs=(head_spec(ATT_HEADS), head_spec(ATT_KV_HEADS), head_spec(ATT_KV_HEADS),
                   row_spec(SSM_INNER), row_spec(SSM_XBC_W), row_spec(LANES),
                   row_spec(FNET_WIDTH), row_spec(3 * SCONV_WIDTH)),
        out_shape=outs, compiler_params=_params("arbitrary", "arbitrary"), name="inproj",
    )(xflat, mod, g, w, cos, sin)


def _attn_kernel(sink_ref, q_ref, *refs, band):
    if band:
        bias_ref, kp_ref, kc_ref, kn_ref, vp_ref, vc_ref, vn_ref, kx_ref, vx_ref, o_ref = refs
    else:
        kx_ref, vx_ref, o_ref = refs
    outs = []
    for kh in range(ATT_KV_HEADS):
        if band:
            keys = jnp.concatenate([kp_ref[0, kh], kc_ref[0, kh], kn_ref[0, kh], kx_ref[0, kh]], axis=0)
            vals = jnp.concatenate([vp_ref[0, kh], vc_ref[0, kh], vn_ref[0, kh], vx_ref[0, kh]], axis=0)
            bias = bias_ref[0]
        else:
            keys, vals = kx_ref[0, kh], vx_ref[0, kh]
        ones_col = (lax.broadcasted_iota(I32, vals.shape, 1) == 0).astype(BF16)
        v_ext = jnp.concatenate([vals, ones_col], axis=1)
        for g in range(ATT_GROUP):
            head = kh * ATT_GROUP + g
            s = _dot_nt(q_ref[0, head], keys)
            if band:
                s = s + bias
            sink = sink_ref[head]
            m = jnp.maximum(jnp.max(s, axis=1, keepdims=True), sink)
            acc = jnp.dot(jnp.exp(s - m).astype(BF16), v_ext, preferred_element_type=F32)
            den = acc[:, HEAD_DIM:HEAD_DIM + 1] + jnp.exp(sink - m)
            outs.append(acc[:, :HEAD_DIM] / den)
    o_ref[0] = jnp.concatenate(outs, axis=1).astype(BF16)


def _attention(sink, q, k, v, kx, vx, tq, band):
    nb, _, seq, _ = q.shape
    lc = kx.shape[2]
    nt = seq // tq
    r = tq // WINDOW
    nblk = seq // WINDOW
    kvh = ATT_KV_HEADS
    qs = pl.BlockSpec((1, ATT_HEADS, tq, HEAD_DIM), lambda b, i: (b, 0, i, 0))
    cur = pl.BlockSpec((1, kvh, tq, HEAD_DIM), lambda b, i: (b, 0, i, 0))
    prev = pl.BlockSpec((1, kvh, WINDOW, HEAD_DIM), lambda b, i: (b, 0, jnp.maximum(i * r - 1, 0), 0))
    nxt = pl.BlockSpec((1, kvh, WINDOW, HEAD_DIM), lambda b, i: (b, 0, jnp.minimum((i + 1) * r, nblk - 1), 0))
    ctx = pl.BlockSpec((1, kvh, lc, HEAD_DIM), lambda b, i: (b, 0, 0, 0))
    smem = pl.BlockSpec(memory_space=pltpu.SMEM)
    if band:
        nk = tq + 2 * WINDOW
        rel = np.arange(nk)[None, :] - WINDOW - np.arange(tq)[:, None]
        inside = np.abs(rel) <= WINDOW
        col = np.arange(nk)[None, :]
        kinds = [inside & ((col >= WINDOW) | (not first)) & ((col < tq + WINDOW) | (not lastt))
                 for lastt in (False, True) for first in (False, True)]
        kinds = np.concatenate([np.stack(kinds), np.ones((4, tq, lc), bool)], axis=2)
        bias = jnp.asarray(np.where(kinds, 0.0, -np.inf), F32)
        bias_spec = pl.BlockSpec((1, tq, nk + lc),
                                 lambda b, i: ((i == 0).astype(I32) + 2 * (i == nt - 1).astype(I32), 0, 0))
        in_specs = [smem, qs, bias_spec, prev, cur, nxt, prev, cur, nxt, ctx, ctx]
        args = (sink, q, bias, k, k, k, v, v, v, kx, vx)
    else:
        in_specs = [smem, qs, ctx, ctx]
        args = (sink, q, kx, vx)
    return pl.pallas_call(
        functools.partial(_attn_kernel, band=band),
        grid=(nb, nt), in_specs=in_specs,
        out_specs=pl.BlockSpec((1, tq, ATT_HEADS * HEAD_DIM), lambda b, i: (b, i, 0)),
        out_shape=jax.ShapeDtypeStruct((nb, seq, ATT_HEADS * HEAD_DIM), BF16),
        compiler_params=_params("arbitrary", "arbitrary"), name="attn_band" if band else "attn_ctx",
    )(*args)


def _ssd_kernel(*refs, reverse, with_y, finalize, nc, act_in, emit_act):
    refs = list(refs)
    if act_in:
        act_ref, dt_ref, dtb_ref, alog_ref, init_ref = refs[:5]
        refs = refs[5:]
    else:
        xp_ref, xc_ref, xn_ref, dt_ref, cw_ref, cb_ref, dtb_ref, alog_ref, init_ref = refs[:9]
        refs = refs[9:]
    if finalize:
        yf_ref, z_ref, dsk_ref, nw_ref = refs[:4]
        refs = refs[4:]
    if with_y:
        y_ref = refs.pop(0)
    if emit_act:
        act_out_ref = refs.pop(0)
    (st_ref,) = refs
    q = SSM_CHUNK
    rows = dt_ref.shape[1]
    c = pl.program_id(1)
    ce = (nc - 1 - c) if reverse else c

    @pl.when(c == 0)
    def _():
        st_ref[...] = init_ref[...]

    if act_in:
        act_all = act_ref[0].astype(F32)
    else:
        xc = xc_ref[0].astype(F32)
        prow = jnp.where(ce == 0, 0.0, xp_ref[0].astype(F32)[BF16_ROWS - 1:BF16_ROWS])
        nrow = jnp.where(ce == nc - 1, 0.0, xn_ref[0].astype(F32)[0:1])
        rid = lax.broadcasted_iota(I32, xc.shape, 0)
        up = jnp.where(rid == 0, prow, pltpu.roll(xc, 1, 0))
        dn = jnp.where(rid == rows - 1, nrow, pltpu.roll(xc, rows - 1, 0))
        cw = cw_ref[...]
        act_all = _silu(cw[0:1] * up + cw[1:2] * xc + cw[2:3] * dn + cb_ref[...])
        if emit_act:
            act_out_ref[0] = act_all.astype(BF16)
    pre = dt_ref[0] + dtb_ref[...]
    dt_all = jnp.maximum(pre, 0.0) + jnp.log1p(jnp.exp(-jnp.abs(pre)))
    neg_a = -jnp.exp(alog_ref[...])

    subs = range(rows // q)
    for sub in (reversed(subs) if reverse else subs):
        r0 = sub * q
        _ssd_chunk(act_all[r0:r0 + q], dt_all[r0:r0 + q], neg_a, st_ref,
                   (yf_ref[0, r0:r0 + q], z_ref[0, r0:r0 + q], dsk_ref[...], nw_ref[...]) if finalize else None,
                   y_ref.at[0, r0:r0 + q] if with_y else None, reverse)


def _ssd_chunk(act, dt, neg_a, st_ref, fin, y_out, reverse):
    q = SSM_CHUNK
    with_y = y_out is not None
    xs = act[:, :SSM_INNER]
    bmat = act[:, SSM_INNER:SSM_INNER + SSM_BC_W]
    cmat = act[:, SSM_INNER + SSM_BC_W:]
    dta = dt * neg_a
    ri = lax.broadcasted_iota(I32, (q, q), 0)
    ci = lax.broadcasted_iota(I32, (q, q), 1)
    tri = (ri <= ci) if reverse else (ri >= ci)
    acs = _dot_split(tri.astype(BF16), dta)
    col0 = SSM_HEADS if reverse else 0
    lo_half = lax.broadcasted_iota(I32, (q, LANES), 1) < SSM_HEAD_DIM

    def bcast(mat, h):
        return jnp.broadcast_to(mat[:, col0 + h:col0 + h + 1], (q, LANES))

    def lane_expand(cols):
        return jnp.concatenate([jnp.where(lo_half, cols[2 * j], cols[2 * j + 1]) for j in range(SSM_HEADS // 2)],
                               axis=1)

    acs_b = [bcast(acs, h) for h in range(SSM_HEADS)]
    acs_e = lane_expand(acs_b)
    xd = xs * lane_expand([bcast(dt, h) for h in range(SSM_HEADS)])
    tot_e = acs_e[0:1] if reverse else acs_e[q - 1:q]
    hpg = SSM_HEADS // SSM_GROUPS
    gw = hpg * SSM_HEAD_DIM

    def grp(mat, g):
        return mat[:, g * SSM_STATE:(g + 1) * SSM_STATE]

    if with_y:
        acs_t = acs.T
        cb = [_dot_nt(grp(cmat, g), grp(bmat, g)) for g in range(SSM_GROUPS)]
        y_diag = []
        for j in range(SSM_HEADS // 2):
            sc = []
            for h in (2 * j, 2 * j + 1):
                seg = acs_b[h] - acs_t[col0 + h:col0 + h + 1, :]
                sc.append((cb[h // hpg] * jnp.exp(jnp.where(tri, seg, -jnp.inf))).astype(BF16))
            slab = xd[:, j * LANES:(j + 1) * LANES].astype(BF16)
            rhs = jnp.concatenate([jnp.where(lo_half, slab, 0), jnp.where(lo_half, 0, slab)], axis=0)
            y_diag.append(jnp.dot(jnp.concatenate(sc, axis=1), rhs, preferred_element_type=F32))
        y_off = [_dot(grp(cmat, g), st_ref[0, g]) for g in range(SSM_GROUPS)]
        y = jnp.concatenate(y_diag, axis=1) + jnp.concatenate(y_off, axis=1) * jnp.exp(acs_e)

    xdd = xd * jnp.exp(tot_e - acs_e)
    b_t = bmat.T
    for g in range(SSM_GROUPS):
        upd = _dot(b_t[g * SSM_STATE:(g + 1) * SSM_STATE, :], xdd[:, g * gw:(g + 1) * gw])
        st_ref[0, g] = st_ref[0, g] * jnp.exp(tot_e[:, g * gw:(g + 1) * gw]) + upd

    if fin is not None:
        y_fwd, z, dsk, nw = fin
        yt = y_fwd + y + dsk * xs
        yt = yt * _silu(z.astype(F32))
        yt = yt * lax.rsqrt(jnp.mean(yt * yt, axis=-1, keepdims=True) + EPS) * nw
        y_out[...] = yt.astype(BF16)
    elif with_y:
        y_out[...] = y


def _ssd(xbc, dtr, cw, cb, dtb, alog, init, reverse, with_y, fin=None, act=None, emit_act=False):
    nb, seq, _ = dtr.shape
    st_block = (1, SSM_GROUPS, SSM_STATE, SSM_INNER // SSM_GROUPS)
    blk = min(SSM_CHUNK * SSD_CHUNKS_PER_STEP, seq)
    nc = seq // blk
    per = blk // BF16_ROWS
    nhalo = seq // BF16_ROWS
    ce = (lambda c: nc - 1 - c) if reverse else (lambda c: c)
    xw = SSM_XBC_W
    blk_spec = lambda wd: pl.BlockSpec((1, blk, wd), lambda b, c: (b, ce(c), 0))
    tail_specs = [_const_spec((1, LANES)), _const_spec((1, LANES)), pl.BlockSpec(st_block, lambda b, c: (b, 0, 0, 0))]
    if act is not None:
        in_specs = [blk_spec(xw), blk_spec(LANES)] + tail_specs
        args = [act, dtr, dtb, alog, init]
    else:
        in_specs = [pl.BlockSpec((1, BF16_ROWS, xw), lambda b, c: (b, jnp.maximum(ce(c) * per - 1, 0), 0)),
                    blk_spec(xw),
                    pl.BlockSpec((1, BF16_ROWS, xw), lambda b, c: (b, jnp.minimum((ce(c) + 1) * per, nhalo - 1), 0)),
                    blk_spec(LANES), _const_spec((8, xw)), _const_spec((1, xw))] + tail_specs
        args = [xbc, xbc, xbc, dtr, cw, cb, dtb, alog, init]
    st_spec = pl.BlockSpec(st_block, lambda b, c: (b, 0, 0, 0))
    st_shape = jax.ShapeDtypeStruct((nb,) + st_block[1:], F32)
    y_spec = pl.BlockSpec((1, blk, SSM_INNER), lambda b, c: (b, ce(c), 0))
    finalize = fin is not None
    if finalize:
        yf, z, dsk, nw = fin
        in_specs += [y_spec, y_spec, _const_spec((1, SSM_INNER)), _const_spec((1, SSM_INNER))]
        args += [yf, z, dsk, nw]
    out_specs, out_shape = [], []
    if with_y:
        out_specs.append(y_spec)
        out_shape.append(jax.ShapeDtypeStruct((nb, seq, SSM_INNER), BF16 if finalize else F32))
    if emit_act:
        out_specs.append(blk_spec(xw))
        out_shape.append(jax.ShapeDtypeStruct((nb, seq, xw), BF16))
    out_specs.append(st_spec)
    out_shape.append(st_shape)
    res = pl.pallas_call(
        functools.partial(_ssd_kernel, reverse=reverse, with_y=with_y, finalize=finalize, nc=nc,
                          act_in=act is not None, emit_act=emit_act),
        grid=(nb, nc), in_specs=in_specs, out_specs=tuple(out_specs), out_shape=tuple(out_shape),
        compiler_params=_params("arbitrary", "arbitrary"), name="ssd_bwd" if reverse else "ssd_fwd",
    )(*args)
    return res


def _fft_a_kernel(u_ref, wc_ref, m1_ref, y_ref, wc_bf):
    @pl.when((pl.program_id(0) == 0) & (pl.program_id(1) == 0))
    def _():
        wc_bf[...] = wc_ref[...].astype(BF16)

    per = m1_ref.shape[0]
    u_all = jnp.concatenate([u_ref[0, :, j, :] for j in range(per)], axis=0)
    v_all = _dot(u_all, wc_bf[...])
    for j in range(per):
        v = v_all[j * FFT_N2:(j + 1) * FFT_N2]
        vs = jnp.concatenate([v[:, :FNET_WIDTH], v[:, FNET_WIDTH:]], axis=0)
        b = _dot(m1_ref[j], vs)
        y_ref[0, 0, j] = b[:FFT_N2]
        y_ref[0, 1, j] = b[FFT_N2:]


def _fft_c_kernel(y_ref, m2_ref, o_ref, m2_bf):
    @pl.when((pl.program_id(0) == 0) & (pl.program_id(1) == 0))
    def _():
        m2_bf[...] = m2_ref[...].astype(BF16)

    rows = y_ref.shape[1] * y_ref.shape[2] * y_ref.shape[4]
    y2d = y_ref[0].reshape(rows, FNET_WIDTH).astype(BF16)
    o_ref[0] = jnp.dot(m2_bf[...], y2d, preferred_element_type=F32).reshape(o_ref.shape[1:])


def _fft_small_kernel(u_ref, wc_ref, m_ref, o_ref):
    v = _dot(u_ref[0], wc_ref[...])
    vs = jnp.concatenate([v[:, :FNET_WIDTH], v[:, FNET_WIDTH:]], axis=0)
    o_ref[0] = _dot(m_ref[...], vs).astype(BF16)


def _channel_dft():
    idx = np.arange(FNET_GROUP_DIM)
    ang = 2.0 * np.pi * np.outer(idx, idx) / FNET_GROUP_DIM
    eye = np.eye(FNET_GROUPS)
    return jnp.asarray(np.concatenate([np.kron(eye, np.cos(ang)), -np.kron(eye, np.sin(ang))], axis=1), F32)


def _fourier_long(u):
    nb, seq, w = u.shape
    n2 = FFT_N2
    n1 = seq // n2
    t2 = np.arange(n2)
    k2 = np.arange(n2)
    t1 = np.arange(n1)
    ang = 2.0 * np.pi * (np.outer(k2, t2)[None] / n2 + (t1[:, None, None] * k2[None, :, None]) / seq)
    co, si = np.cos(ang), np.sin(ang)
    m1 = jnp.asarray(np.concatenate([np.concatenate([co, si], axis=2),
                                     np.concatenate([-si, co], axis=2)], axis=1), F32)
    ang1 = 2.0 * np.pi * np.outer(t1, t1) / n1
    norm = 1.0 / math.sqrt(seq * FNET_GROUP_DIM)
    m2 = np.concatenate([np.cos(ang1), np.sin(ang1)], axis=1) * norm
    m2k = jnp.asarray(np.kron(m2, np.eye(F32_ROWS)), F32)
    per = F32_ROWS
    y = pl.pallas_call(
        _fft_a_kernel, grid=(n1 // per, nb),
        in_specs=[pl.BlockSpec((1, n2, per, w), lambda j, b: (b, 0, j, 0)),
                  _const_spec((w, 2 * w)),
                  pl.BlockSpec((per, 2 * n2, 2 * n2), lambda j, b: (j, 0, 0))],
        out_specs=pl.BlockSpec((1, 2, per, n2, w), lambda j, b: (b, 0, j, 0, 0)),
        out_shape=jax.ShapeDtypeStruct((nb, 2, n1, n2, w), F32),
        scratch_shapes=[pltpu.VMEM((w, 2 * w), BF16)],
        compiler_params=_params("arbitrary", "arbitrary"), name="fft_a",
    )(u.reshape(nb, n2, n1, w), _channel_dft(), m1)
    out = pl.pallas_call(
        _fft_c_kernel, grid=(nb, n2 // per),
        in_specs=[pl.BlockSpec((1, 2, n1, 1, per, w), lambda b, j: (b, 0, 0, j, 0, 0)),
                  _const_spec((n1 * per, 2 * n1 * per))],
        out_specs=pl.BlockSpec((1, n1, 1, per, w), lambda b, j: (b, 0, j, 0, 0)),
        out_shape=jax.ShapeDtypeStruct((nb, n1, n2 // per, per, w), F32),
        scratch_shapes=[pltpu.VMEM((n1 * per, 2 * n1 * per), BF16)],
        compiler_params=_params("arbitrary", "arbitrary"), name="fft_c",
    )(y.reshape(nb, 2, n1, n2 // per, per, w), m2k)
    return out.reshape(nb, seq, w)


def _fourier_short(u):
    nb, seq, w = u.shape
    t = np.arange(seq)
    ang = 2.0 * np.pi * np.outer(t, t) / seq
    norm = 1.0 / math.sqrt(seq * FNET_GROUP_DIM)
    m = jnp.asarray(np.concatenate([np.cos(ang), np.sin(ang)], axis=1) * norm, F32)
    return pl.pallas_call(
        _fft_small_kernel, grid=(nb,),
        in_specs=[pl.BlockSpec((1, seq, w), lambda b: (b, 0, 0)),
                  _const_spec((w, 2 * w)), _const_spec((seq, 2 * seq))],
        out_specs=pl.BlockSpec((1, seq, w), lambda b: (b, 0, 0)),
        out_shape=jax.ShapeDtypeStruct((nb, seq, w), BF16),
        compiler_params=_params("arbitrary"), name="fft_small",
    )(u, _channel_dft(), m)


def _merge_kernel(x_ref, mod_ref, g1_ref, g2_ref, att_ref, ssm_ref, fft_ref, sup_ref, su_ref, sun_ref,
                  sw_ref, wg_ref, bg_ref, wb_ref, wo_ref, wr_ref, rb_ref, cin_ref,
                  x1_ref, h2_ref, ri_ref, rw_ref, cnt_ref, *, nt):
    d = x_ref.shape[1]
    t = x_ref.shape[0]
    i = pl.program_id(1)

    @pl.when((pl.program_id(0) == 0) & (i == 0))
    def _():
        cnt_ref[...] = cin_ref[...]

    m = mod_ref[0]

    def gated(su):
        return su[:, SCONV_WIDTH:2 * SCONV_WIDTH] * su[:, 2 * SCONV_WIDTH:]

    su = su_ref[0].astype(F32)
    p = gated(su)
    prow = jnp.where(i == 0, 0.0, gated(sup_ref[0].astype(F32)[BF16_ROWS - 1:BF16_ROWS]))
    nrow = jnp.where(i == nt - 1, 0.0, gated(sun_ref[0].astype(F32)[0:1]))
    rid = lax.broadcasted_iota(I32, p.shape, 0)
    up = jnp.where(rid == 0, prow, pltpu.roll(p, 1, 0))
    dn = jnp.where(rid == t - 1, nrow, pltpu.roll(p, t - 1, 0))
    sw = sw_ref[...]
    sconv = su[:, :SCONV_WIDTH] * (sw[0:1] * up + sw[1:2] * p + sw[2:3] * dn)

    rg = min(t, MERGE_ROW_GROUP)
    h2_parts = []
    for a in range(0, t, rg):
        x = x_ref[a:a + rg]
        hb = _norm_mod(x, g1_ref[...], m[0:1], m[1:2]).astype(BF16)
        branches = (att_ref[0, a:a + rg], ssm_ref[0, a:a + rg], fft_ref[0, a:a + rg].astype(BF16),
                    sconv[a:a + rg].astype(BF16))
        acc = jnp.zeros((rg, d), F32)
        for n in range(N_BRANCHES):
            gate = _sigmoid(jnp.dot(hb, wg_ref[:, n * d:(n + 1) * d], preferred_element_type=F32)
                            + bg_ref[:, n * d:(n + 1) * d])
            acc = acc + gate * jnp.dot(branches[n], wb_ref[n], preferred_element_type=F32)
        y = jnp.dot(acc.astype(BF16), wo_ref[...], preferred_element_type=F32)
        x1 = x + m[2:3] * y
        x1_ref[a:a + rg] = x1
        h2 = _norm_mod(x1, g2_ref[...], m[3:4], m[4:5])
        h2_ref[a:a + rg] = h2
        h2_parts.append(h2)
    _route_tile(jnp.concatenate(h2_parts, axis=0), wr_ref, rb_ref, ri_ref, rw_ref, cnt_ref)


def _merge(xflat, nb, seq, t, mod, mod_row, g1, g2, att, ssm, fft, su, sw, wg, bg, wb, wo, wr, rb, cnt_in):
    ntok, d = xflat.shape
    nt = seq // t
    per = t // BF16_ROWS
    nhalo = seq // BF16_ROWS
    mod_map = (lambda b, i: (b, 0, 0)) if mod_row is None else (lambda b, i: (mod_row, 0, 0))
    row = pl.BlockSpec((t, d), lambda b, i: (b * nt + i, 0))
    br = pl.BlockSpec((1, t, BRANCH_W), lambda b, i: (b, i, 0))
    suw = 3 * SCONV_WIDTH
    in_specs = [row, pl.BlockSpec((1, 8, d), mod_map), _const_spec((1, d)), _const_spec((1, d)),
                br, br, br,
                pl.BlockSpec((1, BF16_ROWS, suw), lambda b, i: (b, jnp.maximum(i * per - 1, 0), 0)),
                pl.BlockSpec((1, t, suw), lambda b, i: (b, i, 0)),
                pl.BlockSpec((1, BF16_ROWS, suw), lambda b, i: (b, jnp.minimum((i + 1) * per, nhalo - 1), 0)),
                _const_spec((8, SCONV_WIDTH)), _const_spec((d, N_BRANCHES * d)), _const_spec((1, N_BRANCHES * d)),
                _const_spec((N_BRANCHES, BRANCH_W, d)), _const_spec((d, d)),
                _const_spec((N_EXPERTS, d)), _const_spec((N_EXPERTS, t)), _const_spec((N_EXPERTS, LANES))]
    tile_rows = pl.BlockSpec((1, 8, t), lambda b, i: (b * nt + i, 0, 0))
    x1, h2, ri, rw, cnt = pl.pallas_call(
        functools.partial(_merge_kernel, nt=nt),
        grid=(nb, nt), in_specs=in_specs,
        out_specs=(row, row, tile_rows, tile_rows, pl.BlockSpec((N_EXPERTS, LANES), lambda b, i: (0, 0))),
        out_shape=(jax.ShapeDtypeStruct((ntok, d), F32), jax.ShapeDtypeStruct((ntok, d), F32),
                   jax.ShapeDtypeStruct((ntok // t, 8, t), I32), jax.ShapeDtypeStruct((ntok // t, 8, t), F32),
                   jax.ShapeDtypeStruct((N_EXPERTS, LANES), F32)),
        compiler_params=_params("arbitrary", "arbitrary"), name="merge",
    )(xflat, mod, g1, g2, att, ssm, fft, su, su, su, sw, wg, bg, wb, wo, wr, jnp.tile(rb, (1, t // LANES)), cnt_in)
    return x1, h2, ri.transpose(0, 2, 1).reshape(ntok, 8), rw.transpose(0, 2, 1).reshape(ntok, 8), cnt


def _route_tile(h2, wr_ref, rb_ref, ri_ref, rw_ref, cnt_ref):
    t = h2.shape[0]
    ng = N_EXPERT_GROUPS
    sc = _sigmoid(_dot_nt(wr_ref[...], h2))
    sel = sc + rb_ref[...]
    s = [sel[j * ng:(j + 1) * ng] for j in range(EXPERTS_PER_GROUP)]
    u = [sc[j * ng:(j + 1) * ng] for j in range(EXPERTS_PER_GROUP)]
    gs = None
    for a in range(EXPERTS_PER_GROUP):
        for b in range(a + 1, EXPERTS_PER_GROUP):
            pair = s[a] + s[b]
            gs = pair if gs is None else jnp.maximum(gs, pair)
    grp = lax.broadcasted_iota(I32, (ng, t), 0)
    gmax = jnp.max(gs, axis=0, keepdims=True)
    best = jnp.min(jnp.where(gs == gmax, grp, ng), axis=0, keepdims=True)
    on = grp == best
    v = [jnp.sum(jnp.where(on, s[j], 0.0), axis=0, keepdims=True) for j in range(EXPERTS_PER_GROUP)]
    w = [jnp.sum(jnp.where(on, u[j], 0.0), axis=0, keepdims=True) for j in range(EXPERTS_PER_GROUP)]

    def first_argmax(vals):
        mx = vals[0]
        for x in vals[1:]:
            mx = jnp.maximum(mx, x)
        idx = jnp.full_like(best, len(vals) - 1)
        for j in range(len(vals) - 2, -1, -1):
            idx = jnp.where(vals[j] == mx, j, idx)
        return idx

    def pick(vals, idx):
        out = vals[-1]
        for j in range(len(vals) - 2, -1, -1):
            out = jnp.where(idx == j, vals[j], out)
        return out

    l1 = first_argmax(v)
    l2 = first_argmax([jnp.where(l1 == j, -jnp.inf, v[j]) for j in range(EXPERTS_PER_GROUP)])
    w1, w2 = pick(w, l1), pick(w, l2)
    tot = w1 + w2
    e1 = best * EXPERTS_PER_GROUP + l1
    e2 = best * EXPERTS_PER_GROUP + l2

    eid = lax.broadcasted_iota(I32, (N_EXPERTS, t), 0)
    oh1 = (eid == e1).astype(F32)
    oh2 = (eid == e2).astype(F32)
    oh = oh1 + oh2
    ri_ = lax.broadcasted_iota(I32, (t, t), 0)
    ci_ = lax.broadcasted_iota(I32, (t, t), 1)
    cnt = cnt_ref[...]
    before = _dot(oh, (ri_ < ci_).astype(F32)) + jnp.tile(cnt, (1, t // LANES))
    r1 = jnp.sum(oh1 * before, axis=0, keepdims=True).astype(I32)
    r2 = jnp.sum(oh2 * before, axis=0, keepdims=True).astype(I32)
    cnt_ref[...] = cnt + jnp.sum(oh, axis=1, keepdims=True)

    row = lax.broadcasted_iota(I32, (8, t), 0)
    ri_ref[0] = jnp.where(row == 0, e1, jnp.where(row == 1, e2, jnp.where(row == 2, r1, jnp.where(row == 3, r2, 0))))
    rw_ref[0] = jnp.where(row == 0, w1 / tot, jnp.where(row == 1, w2 / tot, 0.0))


def _dispatch_kernel(dest_ref, zt_ref, *refs, td, starts):
    n_streams = len(starts) - 1
    h_refs = refs[:n_streams]
    xs_hbm, zbuf, zsem, sem = refs[n_streams:]
    i = pl.program_id(0)

    @pl.when(i == 0)
    def _():
        zbuf[...] = jnp.zeros_like(zbuf)

        def zero_copy(j):
            start = pl.multiple_of(zt_ref[j] * MOE_TILE, MOE_TILE)
            return pltpu.make_async_copy(zbuf, xs_hbm.at[pl.ds(start, MOE_TILE)], zsem)

        def zissue(j, carry):
            @pl.when(zt_ref[j] >= 0)
            def _():
                zero_copy(j).start()
            return carry

        def zdrain(j, carry):
            @pl.when(zt_ref[j] >= 0)
            def _():
                zero_copy(j).wait()
            return carry

        lax.fori_loop(0, zt_ref.shape[0], zissue, 0)
        lax.fori_loop(0, zt_ref.shape[0], zdrain, 0)

    for s in range(n_streams):
        h_ref = h_refs[s]

        def row_copy(src_row, dst_row, h_ref=h_ref):
            return pltpu.make_async_copy(h_ref.at[pl.ds(src_row, 1)], xs_hbm.at[pl.ds(dst_row, 1)], sem)

        @pl.when((i >= starts[s]) & (i < starts[s + 1]))
        def _(row_copy=row_copy):
            def issue(j, carry):
                row_copy(j, dest_ref[0, 0, 2 * j]).start()
                row_copy(j, dest_ref[0, 0, 2 * j + 1]).start()
                return carry

            lax.fori_loop(0, td, issue, 0, unroll=ROW_DMA_UNROLL)
            all_rows = xs_hbm.at[pl.ds(0, 2 * td)]
            pltpu.make_async_copy(all_rows, all_rows, sem).wait()


def _dispatch(h2s, dest, ztiles, nrows):
    d = h2s[0].shape[1]
    td = ROW_TILE
    starts = [0]
    for h in h2s:
        starts.append(starts[-1] + h.shape[0] // td)
    nt = starts[-1]

    def stream_spec(s):
        return pl.BlockSpec((td, d), lambda i: (jnp.clip(i - starts[s], 0, starts[s + 1] - starts[s] - 1), 0))

    return pl.pallas_call(
        functools.partial(_dispatch_kernel, td=td, starts=tuple(starts)), grid=(nt,),
        in_specs=[pl.BlockSpec((1, 1, 2 * td), lambda i: (i, 0, 0), memory_space=pltpu.SMEM),
                  pl.BlockSpec(memory_space=pltpu.SMEM)] + [stream_spec(s) for s in range(len(h2s))],
        out_specs=pl.BlockSpec(memory_space=pl.ANY),
        out_shape=jax.ShapeDtypeStruct((nrows, d), F32),
        scratch_shapes=[pltpu.VMEM((MOE_TILE, d), F32), pltpu.SemaphoreType.DMA(()), pltpu.SemaphoreType.DMA(())],
        compiler_params=_params("arbitrary"), name="dispatch",
    )(dest.reshape(nt, 1, 2 * td), ztiles, *h2s)


def _ffn_kernel(be_ref, nu_ref, x_ref, w1_ref, w3_ref, w2_ref, y_ref, w1_bf, w3_bf, w2_bf):
    i = pl.program_id(0)
    used = i < nu_ref[0]

    @pl.when(used & ((i == 0) | (be_ref[i] != be_ref[jnp.maximum(i - 1, 0)])))
    def _():
        w1_bf[...] = w1_ref[0, 0].astype(BF16)
        w3_bf[...] = w3_ref[0, 0].astype(BF16)
        w2_bf[...] = w2_ref[0, 0].astype(BF16)

    @pl.when(used)
    def _():
        xb = x_ref[...].astype(BF16)
        a = jnp.dot(xb, w1_bf[...], preferred_element_type=F32)
        b = jnp.dot(xb, w3_bf[...], preferred_element_type=F32)
        y_ref[...] = jnp.dot((_silu(a) * b).astype(BF16), w2_bf[...], preferred_element_type=F32)

    @pl.when(i >= nu_ref[0])
    def _():
        y_ref[...] = jnp.zeros_like(y_ref)


def _ffn(xs, block_expert, n_used, w1, w3, w2, layer):
    nrows, d = xs.shape
    de = w1.shape[3]
    tm = MOE_TILE
    grid_spec = pltpu.PrefetchScalarGridSpec(
        num_scalar_prefetch=2, grid=(nrows // tm,),
        in_specs=[pl.BlockSpec((tm, d), lambda i, be, nu: (jnp.minimum(i, nu[0] - 1), 0)),
                  pl.BlockSpec((1, 1, d, de), lambda i, be, nu: (layer, be[i], 0, 0)),
                  pl.BlockSpec((1, 1, d, de), lambda i, be, nu: (layer, be[i], 0, 0)),
                  pl.BlockSpec((1, 1, de, d), lambda i, be, nu: (layer, be[i], 0, 0))],
        out_specs=pl.BlockSpec((tm, d), lambda i, be, nu: (i, 0)),
        scratch_shapes=[pltpu.VMEM((d, de), BF16), pltpu.VMEM((d, de), BF16), pltpu.VMEM((de, d), BF16)])
    return pl.pallas_call(
        _ffn_kernel, grid_spec=grid_spec, out_shape=jax.ShapeDtypeStruct((nrows, d), F32),
        compiler_params=_params("arbitrary"), name="ffn",
    )(block_expert, n_used, xs, w1, w3, w2)


def _combine_kernel(dest_ref, dnext_ref, x_ref, mod_ref, rw_ref, fw_ref, y_hbm, o_ref, buf, sems, *, tc, final):
    i = pl.program_id(0)
    nt = pl.num_programs(0)
    slot = i % 2

    def gather(d_ref, s):
        def issue(j, carry):
            for k in range(2):
                pltpu.make_async_copy(y_hbm.at[pl.ds(d_ref[0, 0, 2 * j + k], 1)],
                                      buf.at[s, k, pl.ds(j, 1)], sems.at[s]).start()
            return carry
        lax.fori_loop(0, tc, issue, 0, unroll=ROW_DMA_UNROLL)

    @pl.when(i == 0)
    def _():
        gather(dest_ref, 0)

    @pl.when(i + 1 < nt)
    def _():
        gather(dnext_ref, 1 - slot)

    pltpu.make_async_copy(buf.at[slot], buf.at[slot], sems.at[slot]).wait()
    rw = rw_ref[...]
    f = rw[:, 0:1] * buf[slot, 0] + rw[:, 1:2] * buf[slot, 1]
    x = x_ref[...] + mod_ref[0][5:6] * f
    if final:
        x = x * lax.rsqrt(jnp.mean(x * x, axis=-1, keepdims=True) + EPS) * fw_ref[...]
    o_ref[...] = x


def _combine(x1, dest, rw, y, mod, mod_row, seq, fw, final):
    ntok, d = x1.shape
    tc = ROW_TILE
    nt = ntok // tc
    per_seq = seq // tc
    mod_map = (lambda i: (i // per_seq, 0, 0)) if mod_row is None else (lambda i: (mod_row, 0, 0))
    return pl.pallas_call(
        functools.partial(_combine_kernel, tc=tc, final=final), grid=(nt,),
        in_specs=[pl.BlockSpec((1, 1, 2 * tc), lambda i: (i, 0, 0), memory_space=pltpu.SMEM),
                  pl.BlockSpec((1, 1, 2 * tc), lambda i: (jnp.minimum(i + 1, nt - 1), 0, 0), memory_space=pltpu.SMEM),
                  pl.BlockSpec((tc, d), lambda i: (i, 0)),
                  pl.BlockSpec((1, 8, d), mod_map),
                  pl.BlockSpec((tc, 8), lambda i: (i, 0)),
                  _const_spec((1, d)),
                  pl.BlockSpec(memory_space=pl.ANY)],
        out_specs=pl.BlockSpec((tc, d), lambda i: (i, 0)),
        out_shape=jax.ShapeDtypeStruct((ntok, d), F32),
        scratch_shapes=[pltpu.VMEM((2, 2, tc, d), F32), pltpu.SemaphoreType.DMA((2,))],
        compiler_params=_params("arbitrary"), name="combine",
    )(dest.reshape(nt, 1, 2 * tc), dest.reshape(nt, 1, 2 * tc), x1, mod, rw, fw, y)


def _moe(streams, cnt, mod, w1, w3, w2, layer, fw, final):
    counts = cnt[:, 0].astype(I32)
    padded = (counts + MOE_TILE - 1) // MOE_TILE * MOE_TILE
    pad_end = jnp.cumsum(padded)
    offs = pad_end - padded
    ntok = sum(s[0].shape[0] for s in streams)
    n_tiles = -(-2 * ntok // MOE_TILE) + N_EXPERTS
    tile_start = jnp.arange(n_tiles, dtype=I32) * MOE_TILE
    block_expert = jnp.minimum(jnp.sum((pad_end[None, :] <= tile_start[:, None]).astype(I32), axis=1), N_EXPERTS - 1)
    n_used = (pad_end[-1:] // MOE_TILE).astype(I32)
    dests = [(offs[s[2][:, 0:2]] + s[2][:, 2:4]).reshape(-1) for s in streams]
    tail = n_used + jnp.arange(N_EXPERTS, dtype=I32)
    ztiles = jnp.concatenate([jnp.where(padded > 0, pad_end // MOE_TILE - 1, -1),
                              jnp.where(tail < n_tiles, tail, -1)]).astype(I32)
    xs = _dispatch([s[1] for s in streams], jnp.concatenate(dests), ztiles, n_tiles * MOE_TILE)
    y = _ffn(xs, block_expert, n_used, w1, w3, w2, layer)
    return [_combine(x1, dest, rw, y, mod, mod_row, seq, fw, final)
            for (x1, _, _, rw, seq, mod_row), dest in zip(streams, dests)]


def _pad_rows(a, rows):
    return jnp.concatenate([a, jnp.zeros((rows - a.shape[0],) + a.shape[1:], a.dtype)], axis=0)


def _pad_cols(a, cols, value=0.0):
    return jnp.concatenate([a, jnp.full(a.shape[:-1] + (cols - a.shape[-1],), value, a.dtype)], axis=-1)


def _rope_tables(seq):
    rows = seq // GRID_W
    row = jnp.repeat(jnp.arange(rows, dtype=F32), GRID_W)
    col = jnp.tile(jnp.arange(GRID_W, dtype=F32), rows)
    pairs = HEAD_DIM // 4
    inv_freq = ROPE_THETA ** (-jnp.arange(pairs, dtype=F32) / pairs)
    ang = jnp.concatenate([row[:, None] * inv_freq, col[:, None] * inv_freq], axis=-1)
    cos, sin = jnp.cos(ang), jnp.sin(ang)
    cos_t = jnp.tile(cos, (1, LANES // (HEAD_DIM // 2)))
    sin_t = jnp.tile(jnp.concatenate([-sin, sin], axis=-1), (1, LANES // HEAD_DIM))
    return cos_t, sin_t


def kernel(x, c, ctx, c_ctx, norm1_w, norm2_w, w_ada, b_ada, w_in, attn_sink, ssm_conv_w, ssm_conv_b,
           ssm_dt_bias, ssm_a_log, ssm_d, ssm_norm_w, sconv_w, w_gate, b_gate, w_branch, w_o,
           w_router, router_bias, moe_w1, moe_w3, moe_w2, final_norm_w):
    nb, seq, d = x.shape
    lc = ctx.shape[1]
    depth = w_in.shape[0]
    n_lat = nb * seq
    n_ctx = nb * lc
    assert seq % LATENT_TILE == 0 and seq % (FFT_N2 * F32_ROWS) == 0 and seq % GRID_W == 0
    assert lc % CTX_TILE == 0 and n_ctx % ROW_TILE == 0 and seq % ROW_TILE == 0
    assert nb < 8

    xl = x.reshape(n_lat, d)
    xc = ctx.reshape(n_ctx, d)
    c8 = _pad_rows(jnp.concatenate([c, c_ctx[None]], axis=0), 8)
    mods = _ada(c8, w_ada, b_ada)
    mods = _pad_cols(mods.reshape(depth, 8, 6, d).swapaxes(-1, -2), 8).swapaxes(-1, -2)

    cos_l, sin_l = _rope_tables(seq)
    cos_c, sin_c = jnp.ones((lc, LANES), F32), jnp.zeros((lc, LANES), F32)

    wr = w_router.T.reshape(N_EXPERT_GROUPS, EXPERTS_PER_GROUP, d).transpose(1, 0, 2).reshape(N_EXPERTS, d)
    wr = wr.astype(BF16)
    rb = jnp.broadcast_to(router_bias.reshape(N_EXPERT_GROUPS, EXPERTS_PER_GROUP).T.reshape(N_EXPERTS, 1),
                          (N_EXPERTS, LANES))

    for layer in range(depth):
        last = layer + 1 == depth
        mod = mods[layer]
        wi = w_in[layer]
        w_pad = jnp.concatenate([wi[:, :768], _pad_cols(wi[:, 2048:2048 + DT_W], 2 * LANES), wi[:, 768:2048],
                                 wi[:, 2048 + DT_W:]], axis=1).astype(BF16)
        g1 = norm1_w[layer][None]
        g2 = norm2_w[layer][None]
        cw = _pad_rows(ssm_conv_w[layer], 8)
        cb = ssm_conv_b[layer][None]
        dtb = _pad_cols(ssm_dt_bias[layer].reshape(1, DT_W), LANES)
        alog = _pad_cols(ssm_a_log[layer].reshape(1, DT_W), LANES)
        dsk = jnp.repeat(ssm_d[layer], SSM_HEAD_DIM)[None]
        nw = ssm_norm_w[layer][None]
        sw = _pad_rows(sconv_w[layer], 8)
        wg = w_gate[layer].astype(BF16)
        bg = b_gate[layer][None]
        wb = w_branch[layer].astype(BF16)
        wo = w_o[layer].astype(BF16)
        sink = attn_sink[layer]

        q, k, v, z, xbc, dtr, fu, su = _inproj(xl, nb, seq, LATENT_TILE, mod, None, g1, w_pad, cos_l, sin_l)
        qc, kc, vc, zc, xbcc, dtrc, fuc, suc = _inproj(xc, nb, lc, CTX_TILE, mod, nb, g1, w_pad, cos_c, sin_c)

        att = _attention(sink, q, k, v, kc, vc, ATT_TILE, True)

        zero_state = jnp.zeros((nb, SSM_GROUPS, SSM_STATE, SSM_INNER // SSM_GROUPS), F32)
        ssd_c = functools.partial(_ssd, xbcc, dtrc, cw, cb, dtb, alog)
        ssd_l = functools.partial(_ssd, xbc, dtr, cw, cb, dtb, alog)
        if last:
            act_c, fin_f = ssd_c(zero_state, False, False, emit_act=True)
            (fin_b,) = ssd_c(zero_state, True, False, act=act_c)
        else:
            yc_f, act_c, fin_f = ssd_c(zero_state, False, True, emit_act=True)
            ssm_c, fin_b = ssd_c(zero_state, True, True, fin=(yc_f, zc, dsk, nw), act=act_c)
        y_f, act_l, _ = ssd_l(fin_f, False, True, emit_act=True)
        ssm, _ = ssd_l(fin_b, True, True, fin=(y_f, z, dsk, nw), act=act_l)

        fft = _fourier_long(fu)

        x1, h2, ri, rw, cnt = _merge(xl, nb, seq, LATENT_TILE, mod, None, g1, g2, att, ssm, fft, su, sw, wg, bg, wb, wo,
                                     wr, rb, jnp.zeros((N_EXPERTS, LANES), F32))
        streams = [(x1, h2, ri, rw, seq, None)]
        if not last:
            att_c = _attention(sink, qc, None, None, kc, vc, lc, False)
            fft_c = _fourier_short(fuc)
            x1c, h2c, ric, rwc, cnt = _merge(xc, nb, lc, CTX_TILE, mod, nb, g1, g2, att_c, ssm_c, fft_c, suc, sw, wg,
                                             bg, wb, wo, wr, rb, cnt)
            streams.append((x1c, h2c, ric, rwc, lc, nb))
        new = _moe(streams, cnt, mod, moe_w1, moe_w3, moe_w2, layer, final_norm_w[None], last)
        xl = new[0]
        if not last:
            xc = new[1]
    return xl.reshape(nb, seq, d)
```
